```python
import jax, jax.numpy as jnp
from jax import lax
import numpy as np

D_MODEL = 1024
BATCH = 1
SEQ = 16384
DEPTH = 1
DEC_BATCH = 32
DEC_SEQ = 16
PAST_LEN = 2048

CHUNK = 64
Q_BLOCK = 128
N_MEM = 256
EPS = 1e-6
H_MLA = 8
NOPE_DIM = 64
ROPE_DIM = 32
V_DIM = 64
Q_LORA = 384
KV_LORA = 256
ROPE_BASE = 10000.0
QK_DIM = NOPE_DIM + ROPE_DIM
H_SB = 8
SB_DIM = 64
H_MEM = 4
MEM_DIM = 128
N_EXPERTS = 32
TOP_K = 4
D_FF = 1024
SWIGLU_LIMIT = 7.0
SWIGLU_ALPHA = 1.702
MOE_BLOCK = 128
MLA_W = H_MLA * V_DIM
SB_W = H_SB * SB_DIM
MEM_W = H_MEM * MEM_DIM
IN_SPLITS = (Q_LORA, KV_LORA, ROPE_DIM, SB_W, SB_W, SB_W, MEM_W, D_MODEL, D_MODEL, D_MODEL)
IN_COLS = sum(IN_SPLITS)

kernel_name = 'hybrid_mla_stickbreaking_memory_moe_step'


def rmsnorm(x, g):
    xf = x.astype(jnp.float32)
    inv = lax.rsqrt(jnp.mean(xf * xf, axis=-1, keepdims=True) + EPS)
    return (xf * inv).astype(x.dtype) * g


def rope(x, pos):
    half = ROPE_DIM // 2
    inv_freq = ROPE_BASE ** (-jnp.arange(half, dtype=jnp.float32) / half)
    ang = pos.astype(jnp.float32)[:, None] * inv_freq
    ang = ang.reshape(ang.shape[:1] + (1,) * (x.ndim - 3) + (half,))
    cos, sin = jnp.cos(ang).astype(x.dtype), jnp.sin(ang).astype(x.dtype)
    x1, x2 = x[..., :half], x[..., half:]
    return jnp.concatenate([x1 * cos - x2 * sin, x2 * cos + x1 * sin], axis=-1)


def qk_gain(g_nope, g_rope_half):
    return jnp.concatenate([g_nope, g_rope_half, g_rope_half])


def chunk_attention(q, k, v, q_pos, k_pos):
    s = jnp.einsum('bqhd,bkhd->bhqk', q, k, preferred_element_type=jnp.float32) * (q.shape[-1] ** -0.5)
    allowed = (k_pos[None, :] // CHUNK) <= (q_pos[:, None] // CHUNK)
    p = jax.nn.softmax(jnp.where(allowed, s, -jnp.inf), axis=-1).astype(v.dtype)
    return jnp.einsum('bhqk,bkhd->bqhd', p, v)


def stick_breaking(q, k, v, q_pos, k_pos):
    z = jnp.einsum('bqhd,bkhd->bhqk', q, k, preferred_element_type=jnp.float32) * (SB_DIM ** -0.5)
    before = k_pos[None, :] < q_pos[:, None]
    log_keep = jnp.where(before, jax.nn.log_sigmoid(-z), 0.0)
    between = lax.cumsum(log_keep, axis=3, reverse=True) - log_keep
    a = jnp.where(before, jnp.exp(jax.nn.log_sigmoid(z) + between), 0.0)
    return jnp.einsum('bhqk,bkhd->bqhd', a.astype(v.dtype), v)


def sweep_query_blocks(fn, q, k, v, q_pos, k_pos):
    b, s, h, d = q.shape
    nb = s // Q_BLOCK
    qb = jnp.moveaxis(q.reshape(b, nb, Q_BLOCK, h, d), 1, 0)
    pb = q_pos.reshape(nb, Q_BLOCK)
    out = lax.map(lambda a: fn(a[0], k, v, a[1], k_pos), (qb, pb))
    return jnp.moveaxis(out, 0, 1).reshape(b, s, h, v.shape[-1])


def token_mixer_inputs(h, pos, lw):
    b, s = h.shape[0], h.shape[1]
    offsets = [int(o) for o in np.cumsum(IN_SPLITS)[:-1]]
    z = h @ lw['w_in']
    q_lat, kv_lat, k_pe, sb_q, sb_k, sb_v, mem_q, g_a, g_b, g_m = jnp.split(z, offsets, axis=-1)
    q = jnp.einsum('bsc,chd->bshd', rmsnorm(q_lat, lw['g_q_lat']), lw['w_q_b'])
    q = jnp.concatenate([q[..., :NOPE_DIM], rope(q[..., NOPE_DIM:], pos)], axis=-1)
    q = rmsnorm(q, qk_gain(lw['g_q_nope'], lw['g_q_rope']))
    kv_lat = rmsnorm(kv_lat, lw['g_kv_lat'])
    k_pe = rope(k_pe, pos)
    sq = sb_q.reshape(b, s, H_SB, SB_DIM)
    sk = sb_k.reshape(b, s, H_SB, SB_DIM)
    sv = sb_v.reshape(b, s, H_SB, SB_DIM)
    mq = rmsnorm(mem_q.reshape(b, s, H_MEM, MEM_DIM), lw['g_mem_q'])
    gates = (jax.nn.sigmoid(g_a), jax.nn.sigmoid(g_b), jax.nn.sigmoid(g_m))
    return q, kv_lat, k_pe, sq, sk, sv, mq, gates


def mla_keys(lat, k_pe, lw):
    kv = jnp.einsum('bsc,chd->bshd', lat, lw['w_kv_b'])
    k_nope, v = kv[..., :NOPE_DIM], kv[..., NOPE_DIM:]
    k_rot = jnp.broadcast_to(k_pe[:, :, None, :], k_nope.shape[:3] + (ROPE_DIM,))
    k = rmsnorm(jnp.concatenate([k_nope, k_rot], axis=-1), qk_gain(lw['g_k_nope'], lw['g_k_rope']))
    return k, v


def memory_kv(mem, lw):
    kv = jnp.einsum('bmd,dhe->bmhe', rmsnorm(mem, lw['g_mem']), lw['w_mem_kv'])
    return rmsnorm(kv[..., :MEM_DIM], lw['g_mem_k']), kv[..., MEM_DIM:]


def memory_attention(q, mk, mv):
    s = jnp.einsum('bqhd,bmhd->bhqm', q, mk, preferred_element_type=jnp.float32) * (MEM_DIM ** -0.5)
    p = jax.nn.softmax(s, axis=-1).astype(mv.dtype)
    return jnp.einsum('bhqm,bmhd->bqhd', p, mv)


def moe(x, lw):
    shape = x.shape
    t = x.reshape(-1, D_MODEL)
    n_tok = t.shape[0]
    logits = (t @ lw['w_router']).astype(jnp.float32) + lw['b_router'].astype(jnp.float32)
    top_val, top_idx = lax.top_k(logits, TOP_K)
    gate = jax.nn.softmax(top_val, axis=-1)
    n = n_tok * TOP_K
    e = top_idx.reshape(n)
    order = jnp.argsort(e)
    e_sorted = e[order]
    tok_sorted = order // TOP_K
    gate_sorted = gate.reshape(n)[order].astype(x.dtype)
    counts = jnp.bincount(e, length=N_EXPERTS)
    start = jnp.cumsum(counts) - counts
    padded = (counts + MOE_BLOCK - 1) // MOE_BLOCK * MOE_BLOCK
    pend = jnp.cumsum(padded)
    dest = (pend - padded)[e_sorted] + jnp.arange(n) - start[e_sorted]
    n_blocks = -(-n // MOE_BLOCK) + N_EXPERTS
    buf = jnp.zeros((n_blocks * MOE_BLOCK, D_MODEL), x.dtype).at[dest].set(t[tok_sorted])
    block_expert = jnp.minimum(jnp.searchsorted(pend, jnp.arange(n_blocks) * MOE_BLOCK, side='right'), N_EXPERTS - 1)

    def expert_block(a):
        xb, ex = a
        gu = xb @ lw['w_gate_up'][ex] + lw['b_gate_up'][ex]
        g = jnp.minimum(gu[:, :D_FF], SWIGLU_LIMIT)
        u = jnp.clip(gu[:, D_FF:], -SWIGLU_LIMIT, SWIGLU_LIMIT)
        hid = (u + 1.0) * (g * jax.nn.sigmoid(SWIGLU_ALPHA * g))
        return hid @ lw['w_down'][ex] + lw['b_down'][ex]

    yb = lax.map(expert_block, (buf.reshape(n_blocks, MOE_BLOCK, D_MODEL), block_expert))
    y = yb.reshape(-1, D_MODEL)[dest] * gate_sorted[:, None]
    return jax.ops.segment_sum(y, tok_sorted, num_segments=n_tok).reshape(shape)


def merge_and_ffn(x, o_mla, o_sb, o_mem, gates, lw):
    b, s = x.shape[0], x.shape[1]
    g_a, g_b, g_m = gates
    u = (g_a * (o_mla.reshape(b, s, MLA_W) @ lw['w_o_mla'])
         + g_b * (o_sb.reshape(b, s, SB_W) @ lw['w_o_sb'])
         + g_m * (o_mem.reshape(b, s, MEM_W) @ lw['w_o_mem']))
    x = x + u @ lw['w_out']
    return x + moe(rmsnorm(x, lw['g_ffn']), lw)


def prompt_layer(x, mem, lw):
    pos = jnp.arange(x.shape[1])
    q, lat, kpe, sq, sk, sv, mq, gates = token_mixer_inputs(rmsnorm(x, lw['g_attn']), pos, lw)
    k, v = mla_keys(lat, kpe, lw)
    o_mla = sweep_query_blocks(chunk_attention, q, k, v, pos, pos)
    o_sb = sweep_query_blocks(stick_breaking, sq, sk, sv, pos, pos)
    mk, mv = memory_kv(mem, lw)
    o_mem = memory_attention(mq, mk, mv)
    y = merge_and_ffn(x, o_mla, o_sb, o_mem, gates, lw)
    return y, lat, kpe, sk, sv, mk, mv


def sample_layer(x, c_lat, c_rope, c_sbk, c_sbv, c_mk, c_mv, lw):
    past = c_lat.shape[1]
    pos = past + jnp.arange(x.shape[1])
    k_pos = jnp.arange(past + x.shape[1])
    q, lat, kpe, sq, sk, sv, mq, gates = token_mixer_inputs(rmsnorm(x, lw['g_attn']), pos, lw)
    k, v = mla_keys(jnp.concatenate([c_lat, lat], axis=1), jnp.concatenate([c_rope, kpe], axis=1), lw)
    o_mla = chunk_attention(q, k, v, pos, k_pos)
    o_sb = stick_breaking(sq, jnp.concatenate([c_sbk, sk], axis=1), jnp.concatenate([c_sbv, sv], axis=1), pos, k_pos)
    o_mem = memory_attention(mq, c_mk, c_mv)
    y = merge_and_ffn(x, o_mla, o_sb, o_mem, gates, lw)
    return y, lat, kpe, sk, sv


def setup_inputs(seed: int = 0) -> dict:
    keys = iter(jax.random.split(jax.random.key(seed), 40))

    def nrm(shape, scale=1.0):
        return scale * jax.random.normal(next(keys), shape, jnp.float32)

    def gain(shape):
        return 1.0 + 0.02 * nrm(shape)

    L = DEPTH
    return {
        'x_prompt': nrm((BATCH, SEQ, D_MODEL)),
        'x_sample': nrm((DEC_BATCH, DEC_SEQ, D_MODEL)),
        'mem_prompt': nrm((BATCH, N_MEM, D_MODEL)),
        'cache_mla_latent': nrm((L, DEC_BATCH, PAST_LEN, KV_LORA)),
        'cache_mla_rope': nrm((L, DEC_BATCH, PAST_LEN, ROPE_DIM)),
        'cache_sb_k': nrm((L, DEC_BATCH, PAST_LEN, H_SB, SB_DIM)),
        'cache_sb_v': nrm((L, DEC_BATCH, PAST_LEN, H_SB, SB_DIM)),
        'cache_mem_k': nrm((L, DEC_BATCH, N_MEM, H_MEM, MEM_DIM)),
        'cache_mem_v': nrm((L, DEC_BATCH, N_MEM, H_MEM, MEM_DIM)),
        'g_attn': gain((L, D_MODEL)),
        'w_in': nrm((L, D_MODEL, IN_COLS), D_MODEL ** -0.5),
        'g_q_lat': gain((L, Q_LORA)),
        'w_q_b': nrm((L, Q_LORA, H_MLA, QK_DIM), Q_LORA ** -0.5),
        'g_q_nope': gain((L, NOPE_DIM)),
        'g_q_rope': gain((L, ROPE_DIM // 2)),
        'g_kv_lat': gain((L, KV_LORA)),
        'w_kv_b': nrm((L, KV_LORA, H_MLA, NOPE_DIM + V_DIM), KV_LORA ** -0.5),
        'g_k_nope': gain((L, NOPE_DIM)),
        'g_k_rope': gain((L, ROPE_DIM // 2)),
        'g_mem': gain((L, D_MODEL)),
        'w_mem_kv': nrm((L, D_MODEL, H_MEM, 2 * MEM_DIM), D_MODEL ** -0.5),
        'g_mem_q': gain((L, MEM_DIM)),
        'g_mem_k': gain((L, MEM_DIM)),
        'w_o_mla': nrm((L, MLA_W, D_MODEL), MLA_W ** -0.5),
        'w_o_sb': nrm((L, SB_W, D_MODEL), SB_W ** -0.5),
        'w_o_mem': nrm((L, MEM_W, D_MODEL), MEM_W ** -0.5),
        'w_out': nrm((L, D_MODEL, D_MODEL), D_MODEL ** -0.5),
        'g_ffn': gain((L, D_MODEL)),
        'w_router': nrm((L, D_MODEL, N_EXPERTS), D_MODEL ** -0.5),
        'b_router': nrm((L, N_EXPERTS), 0.01),
        'w_gate_up': nrm((L, N_EXPERTS, D_MODEL, 2 * D_FF), D_MODEL ** -0.5),
        'b_gate_up': nrm((L, N_EXPERTS, 2 * D_FF), 0.01),
        'w_down': nrm((L, N_EXPERTS, D_FF, D_MODEL), D_FF ** -0.5),
        'b_down': nrm((L, N_EXPERTS, D_MODEL), 0.01),
    }


def reference(x_prompt, x_sample, mem_prompt, cache_mla_latent, cache_mla_rope, cache_sb_k, cache_sb_v,
              cache_mem_k, cache_mem_v, g_attn, w_in, g_q_lat, w_q_b, g_q_nope, g_q_rope, g_kv_lat, w_kv_b,
              g_k_nope, g_k_rope, g_mem, w_mem_kv, g_mem_q, g_mem_k, w_o_mla, w_o_sb, w_o_mem, w_out, g_ffn,
              w_router, b_router, w_gate_up, b_gate_up, w_down, b_down):
    xp, xs = x_prompt, x_sample
    lat_p, rope_p, sbk_p, sbv_p, mk_p, mv_p = [], [], [], [], [], []
    lat_s, rope_s, sbk_s, sbv_s = [], [], [], []
    for l in range(DEPTH):
        lw = {'g_attn': g_attn[l], 'w_in': w_in[l], 'g_q_lat': g_q_lat[l], 'w_q_b': w_q_b[l],
              'g_q_nope': g_q_nope[l], 'g_q_rope': g_q_rope[l], 'g_kv_lat': g_kv_lat[l], 'w_kv_b': w_kv_b[l],
              'g_k_nope': g_k_nope[l], 'g_k_rope': g_k_rope[l], 'g_mem': g_mem[l], 'w_mem_kv': w_mem_kv[l],
              'g_mem_q': g_mem_q[l], 'g_mem_k': g_mem_k[l], 'w_o_mla': w_o_mla[l], 'w_o_sb': w_o_sb[l],
              'w_o_mem': w_o_mem[l], 'w_out': w_out[l], 'g_ffn': g_ffn[l], 'w_router': w_router[l],
              'b_router': b_router[l], 'w_gate_up': w_gate_up[l], 'b_gate_up': b_gate_up[l],
              'w_down': w_down[l], 'b_down': b_down[l]}
        xp, a, b, c, d, e, f = prompt_layer(xp, mem_prompt, lw)
        lat_p.append(a); rope_p.append(b); sbk_p.append(c); sbv_p.append(d); mk_p.append(e); mv_p.append(f)
        xs, a, b, c, d = sample_layer(xs, cache_mla_latent[l], cache_mla_rope[l], cache_sb_k[l], cache_sb_v[l],
                                      cache_mem_k[l], cache_mem_v[l], lw)
        lat_s.append(a); rope_s.append(b); sbk_s.append(c); sbv_s.append(d)
    return (xp, xs, jnp.stack(lat_p), jnp.stack(rope_p), jnp.stack(sbk_p), jnp.stack(sbv_p),
            jnp.stack(mk_p), jnp.stack(mv_p), jnp.stack(lat_s), jnp.stack(rope_s), jnp.stack(sbk_s), jnp.stack(sbv_s))
```

```python
import functools

import numpy as np
import jax
import jax.numpy as jnp
from jax import lax
from jax.experimental import pallas as pl
from jax.experimental.pallas import tpu as pltpu

F32 = jnp.float32
BF16 = jnp.bfloat16

D_MODEL = 1024
CHUNK = 64
EPS = 1e-6
H_MLA = 8
NOPE_DIM = 64
ROPE_DIM = 32
ROPE_HALF = ROPE_DIM // 2
V_DIM = 64
Q_LORA = 384
KV_LORA = 256
ROPE_BASE = 10000.0
QK_DIM = NOPE_DIM + ROPE_DIM
QK_W = H_MLA * QK_DIM
H_SB = 8
SB_DIM = 64
H_MEM = 4
MEM_DIM = 128
N_EXPERTS = 32
TOP_K = 4
D_FF = 1024
SWIGLU_LIMIT = 7.0
SWIGLU_ALPHA = 1.702
MLA_W = H_MLA * V_DIM
SB_W = H_SB * SB_DIM
MEM_W = H_MEM * MEM_DIM

LANES = 128
A_QLAT = 0
A_KVLAT = A_QLAT + Q_LORA
A_KPE = A_KVLAT + KV_LORA
A_KPE_SW = A_KPE + LANES
A_SBQ = A_KPE_SW + LANES
A_SBK = A_SBQ + SB_W
A_SBV = A_SBK + SB_W
A_MEMQ = A_SBV + SB_W
A_COLS = A_MEMQ + MEM_W
KPE_LANE = NOPE_DIM

TM_PROJ = 256
TM_KV = 512
TQ_MLA = 512
TQ_SB = 256
TM_MEM = 512
TM_MERGE = 256
BM_MOE = 256
TT_COMBINE = 256
VMEM_LIMIT = 56 * 1024 * 1024

SB_UNDERFLOW = -120.0
NEG_BIG = -3.0e38


def _cparams(*sem):
    return pltpu.CompilerParams(dimension_semantics=sem, vmem_limit_bytes=VMEM_LIMIT)


def _split(x):
    hi = x.astype(BF16)
    lo = (x - hi.astype(F32)).astype(BF16)
    return hi, lo


def _dot(a, b):
    return jnp.dot(a, b, preferred_element_type=F32)


def _dot2(x, m):
    hi, lo = _split(x)
    return _dot(hi, m) + _dot(lo, m)


def _dot_nt(a, b):
    return lax.dot_general(a, b, (((1,), (1,)), ((), ())), preferred_element_type=F32)


def _rms(x):
    return x * lax.rsqrt(jnp.mean(x * x, axis=-1, keepdims=True) + EPS)


def _softplus(z):
    return jnp.maximum(z, 0.0) + jnp.log1p(jnp.exp(-jnp.abs(z)))


def _full(shape):
    n = len(shape)
    return pl.BlockSpec(shape, lambda *_: (0,) * n)


def _proj_kernel(x_ref, gattn_ref, wa_ref, gqlat_ref, wq_ref, wqs_ref, ind_ref, indt_ref, gq_ref, gkv_ref,
                 bcs_ref, ocos_ref, osin_ref, gmq_ref,
                 q_ref, lat_ref, kpe_ref, sq_ref, sk_ref, sv_ref, skh_ref, svh_ref, mq_ref):
    h = (_rms(x_ref[...]) * gattn_ref[...]).astype(BF16)
    z = _dot(h, wa_ref[...])
    bc = bcs_ref[0, 0:1, :]
    bs = bcs_ref[0, 1:2, :]
    oc = ocos_ref[...]
    osn = osin_ref[...]
    cos_f = bc * oc - bs * osn
    sin_f = bs * oc + bc * osn
    qn = (_rms(z[:, A_QLAT:A_QLAT + Q_LORA]) * gqlat_ref[...]).astype(BF16)
    qr = _dot(qn, wq_ref[...]) * cos_f + _dot(qn, wqs_ref[...]) * sin_f
    ssq = _dot2(qr * qr, ind_ref[...])
    inv = lax.rsqrt(ssq * (1.0 / QK_DIM) + EPS)
    qo = qr * _dot2(inv, indt_ref[...]) * gq_ref[...]
    for hd in range(H_MLA):
        q_ref[hd] = qo[:, hd * QK_DIM:(hd + 1) * QK_DIM].astype(BF16)
    lat_ref[...] = _rms(z[:, A_KVLAT:A_KVLAT + KV_LORA]) * gkv_ref[...]
    kr = z[:, A_KPE:A_KPE + LANES] * cos_f[:, :LANES] + z[:, A_KPE_SW:A_KPE_SW + LANES] * sin_f[:, :LANES]
    kpe_ref[...] = kr[:, KPE_LANE:KPE_LANE + ROPE_DIM]
    sbq = z[:, A_SBQ:A_SBQ + SB_W] * (SB_DIM ** -0.5)
    sbk = z[:, A_SBK:A_SBK + SB_W]
    sbv = z[:, A_SBV:A_SBV + SB_W]
    sk_ref[...] = sbk
    sv_ref[...] = sbv
    for hd in range(H_SB):
        sl = slice(hd * SB_DIM, (hd + 1) * SB_DIM)
        sq_ref[hd] = sbq[:, sl].astype(BF16)
        skh_ref[hd] = sbk[:, sl].astype(BF16)
        svh_ref[hd] = sbv[:, sl].astype(BF16)
    mqs = []
    for hd in range(H_MEM):
        mqs.append(_rms(z[:, A_MEMQ + hd * MEM_DIM:A_MEMQ + (hd + 1) * MEM_DIM]))
    mq_ref[...] = (jnp.concatenate(mqs, axis=-1) * gmq_ref[...]).astype(BF16)


def _rope_tables(base_pos, off_pos):
    lane = np.arange(QK_W) % QK_DIM
    inv_freq = ROPE_BASE ** (-np.arange(ROPE_HALF, dtype=np.float64) / ROPE_HALF)
    freq = np.where(lane >= NOPE_DIM, inv_freq[(lane - NOPE_DIM) % ROPE_HALF], 0.0)
    ab = np.asarray(base_pos, np.float64)[:, None] * freq
    ao = np.asarray(off_pos, np.float64)[:, None] * freq
    bcs = np.stack([np.cos(ab), np.sin(ab)], axis=1).astype(np.float32)
    return jnp.asarray(bcs), jnp.asarray(np.cos(ao), F32), jnp.asarray(np.sin(ao), F32)


def _proj(x2d, base_pos, off_pos, p):
    rows = x2d.shape[0]
    tm = TM_PROJ
    nt = rows // tm
    bcs, ocos, osin = _rope_tables(base_pos, off_pos)
    row = lambda w: pl.BlockSpec((tm, w), lambda i: (i, 0))
    hm = lambda d: pl.BlockSpec((H_MLA, tm, d), lambda i: (0, i, 0))
    out_shape = (
        jax.ShapeDtypeStruct((H_MLA, rows, QK_DIM), BF16),
        jax.ShapeDtypeStruct((rows, KV_LORA), F32),
        jax.ShapeDtypeStruct((rows, ROPE_DIM), F32),
        jax.ShapeDtypeStruct((H_SB, rows, SB_DIM), BF16),
        jax.ShapeDtypeStruct((rows, SB_W), F32),
        jax.ShapeDtypeStruct((rows, SB_W), F32),
        jax.ShapeDtypeStruct((H_SB, rows, SB_DIM), BF16),
        jax.ShapeDtypeStruct((H_SB, rows, SB_DIM), BF16),
        jax.ShapeDtypeStruct((rows, MEM_W), BF16),
    )
    return pl.pallas_call(
        _proj_kernel,
        grid=(nt,),
        in_specs=[row(D_MODEL), _full((1, D_MODEL)), _full((D_MODEL, A_COLS)), _full((1, Q_LORA)),
                  _full((Q_LORA, QK_W)), _full((Q_LORA, QK_W)), _full((QK_W, LANES)), _full((LANES, QK_W)),
                  _full((1, QK_W)), _full((1, KV_LORA)),
                  pl.BlockSpec((1, 2, QK_W), lambda i: (i, 0, 0)), _full((tm, QK_W)), _full((tm, QK_W)),
                  _full((1, MEM_W))],
        out_specs=(hm(QK_DIM), row(KV_LORA), row(ROPE_DIM), hm(SB_DIM), row(SB_W), row(SB_W), hm(SB_DIM),
                   hm(SB_DIM), row(MEM_W)),
        out_shape=out_shape,
        compiler_params=_cparams("parallel"),
        name="proj",
    )(x2d, p["g_attn"], p["w_a"], p["g_q_lat"], p["w_q"], p["w_q_sw"], p["ind96"], p["ind96_t"], p["g_q"],
      p["g_kv_lat"], bcs, ocos, osin, p["g_mem_q"])


def _expand_keys(lat, kpe, wk, pk, ind, indt, gk):
    kf = _dot(lat.astype(BF16), wk) + _dot2(kpe, pk)
    inv = lax.rsqrt(_dot2(kf * kf, ind) * (1.0 / QK_DIM) + EPS)
    return kf * _dot2(inv, indt) * gk


def _kv_expand_kernel(lat_ref, kpe_ref, wk_ref, pk_ref, ind_ref, indt_ref, gk_ref, wv_ref, k_ref, v_ref):
    lat = lat_ref[...]
    ko = _expand_keys(lat, kpe_ref[...], wk_ref[...], pk_ref[...], ind_ref[...], indt_ref[...], gk_ref[...])
    v = _dot(lat.astype(BF16), wv_ref[...])
    for hd in range(H_MLA):
        k_ref[hd] = ko[:, hd * QK_DIM:(hd + 1) * QK_DIM].astype(BF16)
        v_ref[hd] = v[:, hd * V_DIM:(hd + 1) * V_DIM].astype(BF16)


def _kv_expand(lat, kpe, p):
    rows = lat.shape[0]
    tm = min(TM_KV, rows)
    row = lambda w: pl.BlockSpec((tm, w), lambda i: (i, 0))
    hm = lambda d: pl.BlockSpec((H_MLA, tm, d), lambda i: (0, i, 0))
    return pl.pallas_call(
        _kv_expand_kernel,
        grid=(rows // tm,),
        in_specs=[row(KV_LORA), row(ROPE_DIM), _full((KV_LORA, QK_W)), _full((ROPE_DIM, QK_W)),
                  _full((QK_W, LANES)), _full((LANES, QK_W)), _full((1, QK_W)), _full((KV_LORA, MLA_W))],
        out_specs=(hm(QK_DIM), hm(V_DIM)),
        out_shape=(jax.ShapeDtypeStruct((H_MLA, rows, QK_DIM), BF16),
                   jax.ShapeDtypeStruct((H_MLA, rows, V_DIM), BF16)),
        compiler_params=_cparams("parallel"),
        name="kv_expand",
    )(lat, kpe, p["w_k"], p["p_kpe"], p["ind96"], p["ind96_t"], p["g_k"], p["w_v"])


def _mla_attn_kernel(q_ref, k_ref, v_ref, o_ref, m_sc, l_sc, acc_sc):
    i = pl.program_id(1)
    tq = q_ref.shape[1]
    q = q_ref[0]
    m_sc[...] = jnp.full(m_sc.shape, NEG_BIG, F32)
    l_sc[...] = jnp.zeros(l_sc.shape, F32)
    acc_sc[...] = jnp.zeros(acc_sc.shape, F32)

    def tile(j, masked):
        start = pl.multiple_of(j * tq, tq)
        k = k_ref[0, pl.ds(start, tq), :]
        v = v_ref[0, pl.ds(start, tq), :]
        s = _dot_nt(q, k)
        if masked:
            rq = lax.broadcasted_iota(jnp.int32, s.shape, 0) // CHUNK
            ck = lax.broadcasted_iota(jnp.int32, s.shape, 1) // CHUNK
            s = jnp.where(ck <= rq, s, NEG_BIG)
        m_prev = m_sc[...]
        m_new = jnp.maximum(m_prev, jnp.max(s, axis=-1, keepdims=True))
        pr = jnp.exp(s - m_new)
        alpha = jnp.exp(m_prev - m_new)
        l_sc[...] = alpha * l_sc[...] + jnp.sum(pr, axis=-1, keepdims=True)
        acc_sc[...] = alpha * acc_sc[...] + _dot(pr.astype(BF16), v)
        m_sc[...] = m_new

    def body(j, carry):
        tile(j, False)
        return carry

    lax.fori_loop(0, i, body, 0)
    tile(i, True)
    o_ref[0] = (acc_sc[...] / l_sc[...]).astype(BF16)


def _mla_attn(q, k, v):
    rows = q.shape[1]
    tq = min(TQ_MLA, rows)
    return pl.pallas_call(
        _mla_attn_kernel,
        grid=(H_MLA, rows // tq),
        in_specs=[pl.BlockSpec((1, tq, QK_DIM), lambda h, i: (h, i, 0)),
                  pl.BlockSpec((1, rows, QK_DIM), lambda h, i: (h, 0, 0)),
                  pl.BlockSpec((1, rows, V_DIM), lambda h, i: (h, 0, 0))],
        out_specs=pl.BlockSpec((1, tq, V_DIM), lambda h, i: (h, i, 0)),
        out_shape=jax.ShapeDtypeStruct((H_MLA, rows, V_DIM), BF16),
        scratch_shapes=[pltpu.VMEM((tq, 1), F32), pltpu.VMEM((tq, 1), F32), pltpu.VMEM((tq, V_DIM), F32)],
        compiler_params=_cparams("parallel", "arbitrary"),
        name="mla_attn",
    )(q, k, v)


def _sb_attn_kernel(q_ref, k_ref, v_ref, tri_ref, o_ref, acc_sc, c_sc):
    i = pl.program_id(1)
    tq = q_ref.shape[1]
    q = q_ref[0]
    tri = tri_ref[...]
    start = pl.multiple_of(i * tq, tq)
    z = _dot_nt(q, k_ref[0, pl.ds(start, tq), :])
    before = lax.broadcasted_iota(jnp.int32, z.shape, 1) < lax.broadcasted_iota(jnp.int32, z.shape, 0)
    lk = jnp.where(before, -_softplus(z), 0.0)
    a = jnp.where(before, jnp.exp(z + lk + _dot2(lk, tri)), 0.0)
    acc_sc[...] = _dot(a.astype(BF16), v_ref[0, pl.ds(start, tq), :])
    c0 = jnp.sum(lk, axis=-1, keepdims=True)
    c_sc[...] = c0

    def cond(carry):
        j, cmax = carry
        return jnp.logical_and(j >= 0, cmax > SB_UNDERFLOW)

    def body(carry):
        j, _ = carry
        st = pl.multiple_of(j * tq, tq)
        zz = _dot_nt(q, k_ref[0, pl.ds(st, tq), :])
        c = c_sc[...]
        lkk = -_softplus(zz)
        aa = jnp.exp(zz + lkk + _dot2(lkk, tri) + c)
        acc_sc[...] += _dot(aa.astype(BF16), v_ref[0, pl.ds(st, tq), :])
        cn = c + jnp.sum(lkk, axis=-1, keepdims=True)
        c_sc[...] = cn
        return j - 1, jnp.max(cn)

    lax.while_loop(cond, body, (i - 1, jnp.max(c0)))
    o_ref[0] = acc_sc[...].astype(BF16)


def _tri(n):
    return jnp.asarray(np.tril(np.ones((n, n), np.float32), -1), BF16)


def _sb_attn(q, k, v):
    rows = q.shape[1]
    tq = min(TQ_SB, rows)
    return pl.pallas_call(
        _sb_attn_kernel,
        grid=(H_SB, rows // tq),
        in_specs=[pl.BlockSpec((1, tq, SB_DIM), lambda h, i: (h, i, 0)),
                  pl.BlockSpec((1, rows, SB_DIM), lambda h, i: (h, 0, 0)),
                  pl.BlockSpec((1, rows, SB_DIM), lambda h, i: (h, 0, 0)),
                  _full((tq, tq))],
        out_specs=pl.BlockSpec((1, tq, SB_DIM), lambda h, i: (h, i, 0)),
        out_shape=jax.ShapeDtypeStruct((H_SB, rows, SB_DIM), BF16),
        scratch_shapes=[pltpu.VMEM((tq, SB_DIM), F32), pltpu.VMEM((tq, 1), F32)],
        compiler_params=_cparams("parallel", "arbitrary"),
        name="sb_attn",
    )(q, k, v, _tri(tq))


def _mem_kv_kernel(mem_ref, gmem_ref, w_ref, gk_ref, mk_ref, mv_ref):
    mn = (_rms(mem_ref[...]) * gmem_ref[...]).astype(BF16)
    kv = _dot(mn, w_ref[...])
    ks = [_rms(kv[:, hd * MEM_DIM:(hd + 1) * MEM_DIM]) for hd in range(H_MEM)]
    mk_ref[...] = jnp.concatenate(ks, axis=-1) * gk_ref[...]
    mv_ref[...] = kv[:, MEM_W:]


def _mem_kv(mem2d, p):
    n = mem2d.shape[0]
    return pl.pallas_call(
        _mem_kv_kernel,
        grid=(1,),
        in_specs=[_full((n, D_MODEL)), _full((1, D_MODEL)), _full((D_MODEL, 2 * MEM_W)), _full((1, MEM_W))],
        out_specs=(_full((n, MEM_W)), _full((n, MEM_W))),
        out_shape=(jax.ShapeDtypeStruct((n, MEM_W), F32), jax.ShapeDtypeStruct((n, MEM_W), F32)),
        compiler_params=_cparams("arbitrary"),
        name="mem_kv",
    )(mem2d, p["g_mem"], p["w_mem"], p["g_mem_k"])


def _mem_heads(mq, mk, mv):
    outs = []
    for hd in range(H_MEM):
        sl = slice(hd * MEM_DIM, (hd + 1) * MEM_DIM)
        s = _dot_nt(mq[:, sl], mk[:, sl])
        pr = jnp.exp(s - jnp.max(s, axis=-1, keepdims=True))
        o = _dot(pr.astype(BF16), mv[:, sl])
        outs.append(o / jnp.sum(pr, axis=-1, keepdims=True))
    return jnp.concatenate(outs, axis=-1)


def _mem_attn_kernel(mq_ref, mk_ref, mv_ref, o_ref):
    o_ref[...] = _mem_heads(mq_ref[...], mk_ref[...].astype(BF16), mv_ref[...].astype(BF16)).astype(BF16)


def _mem_attn(mq, mk, mv):
    rows = mq.shape[0]
    tm = min(TM_MEM, rows)
    n = mk.shape[0]
    return pl.pallas_call(
        _mem_attn_kernel,
        grid=(rows // tm,),
        in_specs=[pl.BlockSpec((tm, MEM_W), lambda i: (i, 0)), _full((n, MEM_W)), _full((n, MEM_W))],
        out_specs=pl.BlockSpec((tm, MEM_W), lambda i: (i, 0)),
        out_shape=jax.ShapeDtypeStruct((rows, MEM_W), BF16),
        compiler_params=_cparams("parallel"),
        name="mem_attn",
    )(mq, mk, mv)


def _sample_attn_kernel(past, q_ref, kn_ref, vn_ref, sq_ref, skn_ref, svn_ref, mq_ref,
                        clat_ref, crope_ref, csk_ref, csv_ref, cmk_ref, cmv_ref,
                        wk_ref, pk_ref, ind_ref, indt_ref, gk_ref, wv_ref, tri_ref, tris_ref,
                        omla_ref, osb_ref, omem_ref):
    ds = q_ref.shape[1]
    tc = tri_ref.shape[0]
    clat = clat_ref[0, 0]
    kc = _expand_keys(clat, crope_ref[0, 0], wk_ref[...], pk_ref[...], ind_ref[...], indt_ref[...], gk_ref[...])
    vc = _dot(clat.astype(BF16), wv_ref[...])
    q_chunk = (past + lax.broadcasted_iota(jnp.int32, (ds, 1), 0)) // CHUNK
    ok_c = (lax.broadcasted_iota(jnp.int32, (ds, past), 1) // CHUNK) <= q_chunk
    ok_n = ((past + lax.broadcasted_iota(jnp.int32, (ds, ds), 1)) // CHUNK) <= q_chunk
    for hd in range(H_MLA):
        q = q_ref[hd]
        s1 = jnp.where(ok_c, _dot_nt(q, kc[:, hd * QK_DIM:(hd + 1) * QK_DIM].astype(BF16)), NEG_BIG)
        s2 = jnp.where(ok_n, _dot_nt(q, kn_ref[hd]), NEG_BIG)
        m = jnp.maximum(jnp.max(s1, axis=-1, keepdims=True), jnp.max(s2, axis=-1, keepdims=True))
        p1 = jnp.exp(s1 - m)
        p2 = jnp.exp(s2 - m)
        den = jnp.sum(p1, axis=-1, keepdims=True) + jnp.sum(p2, axis=-1, keepdims=True)
        o = _dot(p1.astype(BF16), vc[:, hd * V_DIM:(hd + 1) * V_DIM].astype(BF16)) + _dot(p2.astype(BF16), vn_ref[hd])
        omla_ref[hd] = (o / den).astype(BF16)

    tri = tri_ref[...]
    tris = tris_ref[...]
    before_n = lax.broadcasted_iota(jnp.int32, (ds, ds), 1) < lax.broadcasted_iota(jnp.int32, (ds, ds), 0)
    csk = csk_ref[0, 0]
    csv = csv_ref[0, 0]
    for hd in range(H_SB):
        sl = slice(hd * SB_DIM, (hd + 1) * SB_DIM)
        q = sq_ref[hd]
        z2 = _dot_nt(q, skn_ref[hd])
        l2 = jnp.where(before_n, -_softplus(z2), 0.0)
        a2 = jnp.where(before_n, jnp.exp(z2 + l2 + _dot2(l2, tris)), 0.0)
        o = _dot(a2.astype(BF16), svn_ref[hd])
        c = jnp.sum(l2, axis=-1, keepdims=True)
        z1 = _dot_nt(q, csk[:, sl].astype(BF16))
        l1 = -_softplus(z1)
        v1 = csv[:, sl].astype(BF16)
        for cb in reversed(range(past // tc)):
            cs = slice(cb * tc, (cb + 1) * tc)
            lc = l1[:, cs]
            a1 = jnp.exp(z1[:, cs] + lc + _dot2(lc, tri) + c)
            o = o + _dot(a1.astype(BF16), v1[cs, :])
            c = c + jnp.sum(lc, axis=-1, keepdims=True)
        osb_ref[hd] = o.astype(BF16)

    omem_ref[...] = _mem_heads(mq_ref[...], cmk_ref[0, 0].astype(BF16), cmv_ref[0, 0].astype(BF16)).astype(BF16)


def _sample_attn(q, kn, vn, sq, skn, svn, mq, c_lat, c_rope, c_sbk, c_sbv, c_mk, c_mv, layer, p):
    nb, past = c_lat.shape[1], c_lat.shape[2]
    ds = q.shape[1] // nb
    n_mem = c_mk.shape[2]
    tc = min(256, past)
    hm = lambda d: pl.BlockSpec((H_MLA, ds, d), lambda b: (0, b, 0))
    cache = lambda n, w: pl.BlockSpec((1, 1, n, w), lambda b: (layer, b, 0, 0))
    rows = q.shape[1]
    return pl.pallas_call(
        functools.partial(_sample_attn_kernel, past),
        grid=(nb,),
        in_specs=[hm(QK_DIM), hm(QK_DIM), hm(V_DIM), hm(SB_DIM), hm(SB_DIM), hm(SB_DIM),
                  pl.BlockSpec((ds, MEM_W), lambda b: (b, 0)),
                  cache(past, KV_LORA), cache(past, ROPE_DIM), cache(past, SB_W), cache(past, SB_W),
                  cache(n_mem, MEM_W), cache(n_mem, MEM_W),
                  _full((KV_LORA, QK_W)), _full((ROPE_DIM, QK_W)), _full((QK_W, LANES)), _full((LANES, QK_W)),
                  _full((1, QK_W)), _full((KV_LORA, MLA_W)), _full((tc, tc)), _full((ds, ds))],
        out_specs=(hm(V_DIM), hm(SB_DIM), pl.BlockSpec((ds, MEM_W), lambda b: (b, 0))),
        out_shape=(jax.ShapeDtypeStruct((H_MLA, rows, V_DIM), BF16),
                   jax.ShapeDtypeStruct((H_SB, rows, SB_DIM), BF16),
                   jax.ShapeDtypeStruct((rows, MEM_W), BF16)),
        compiler_params=_cparams("parallel"),
        name="sample_attn",
    )(q, kn, vn, sq, skn, svn, mq, c_lat, c_rope, c_sbk, c_sbv, c_mk, c_mv,
      p["w_k"], p["p_kpe"], p["ind96"], p["ind96_t"], p["g_k"], p["w_v"], _tri(tc), _tri(ds))


def _merge_kernel(x_ref, gattn_ref, wg_ref, omla_ref, osb_ref, omem_ref, woa_ref, wob_ref, wom_ref, wout_ref,
                  gffn_ref, wrh_ref, wrl_ref, br_ref, x1_ref, idx_ref, gate_ref):
    x = x_ref[...]
    h = (_rms(x) * gattn_ref[...]).astype(BF16)
    g = 1.0 / (1.0 + jnp.exp(-_dot(h, wg_ref[...])))
    ua = _dot(omla_ref[0], woa_ref[0])
    ub = _dot(osb_ref[0], wob_ref[0])
    for hd in range(1, H_MLA):
        ua = ua + _dot(omla_ref[hd], woa_ref[hd])
        ub = ub + _dot(osb_ref[hd], wob_ref[hd])
    um = _dot(omem_ref[...], wom_ref[...])
    u = g[:, :D_MODEL] * ua + g[:, D_MODEL:2 * D_MODEL] * ub + g[:, 2 * D_MODEL:] * um
    x1 = x + _dot(u.astype(BF16), wout_ref[...])
    x1_ref[...] = x1
    xh, xl = _split(_rms(x1) * gffn_ref[...])
    lg = _dot(xh, wrh_ref[...]) + _dot(xh, wrl_ref[...]) + _dot(xl, wrh_ref[...]) + br_ref[...]
    lane = lax.broadcasted_iota(jnp.int32, lg.shape, 1).astype(F32)
    vals, ids = [], []
    for _ in range(TOP_K):
        m = jnp.max(lg, axis=-1, keepdims=True)
        sel = jnp.min(jnp.where(lg == m, lane, float(LANES)), axis=-1, keepdims=True)
        vals.append(m)
        ids.append(sel)
        lg = jnp.where(lane == sel, NEG_BIG, lg)
    es = [jnp.exp(v - vals[0]) for v in vals]
    den = es[0] + es[1] + es[2] + es[3]
    idx_o = jnp.zeros(lg.shape, F32)
    gate_o = jnp.zeros(lg.shape, F32)
    for k in range(TOP_K):
        idx_o = jnp.where(lane == float(k), ids[k], idx_o)
        gate_o = jnp.where(lane == float(k), es[k] / den, gate_o)
    idx_ref[...] = idx_o.astype(jnp.int32)
    gate_ref[...] = gate_o


def _merge(x2d, omla, osb, omem, p):
    rows = x2d.shape[0]
    tm = TM_MERGE
    row = lambda w: pl.BlockSpec((tm, w), lambda i: (i, 0))
    hm = lambda d: pl.BlockSpec((H_MLA, tm, d), lambda i: (0, i, 0))
    return pl.pallas_call(
        _merge_kernel,
        grid=(rows // tm,),
        in_specs=[row(D_MODEL), _full((1, D_MODEL)), _full((D_MODEL, 3 * D_MODEL)), hm(V_DIM), hm(SB_DIM),
                  row(MEM_W), _full((H_MLA, V_DIM, D_MODEL)), _full((H_SB, SB_DIM, D_MODEL)),
                  _full((MEM_W, D_MODEL)), _full((D_MODEL, D_MODEL)), _full((1, D_MODEL)),
                  _full((D_MODEL, LANES)), _full((D_MODEL, LANES)), _full((1, LANES))],
        out_specs=(row(D_MODEL), row(LANES), row(LANES)),
        out_shape=(jax.ShapeDtypeStruct((rows, D_MODEL), F32), jax.ShapeDtypeStruct((rows, LANES), jnp.int32),
                   jax.ShapeDtypeStruct((rows, LANES), F32)),
        compiler_params=_cparams("parallel"),
        name="merge",
    )(x2d, p["g_attn"], p["w_g"], omla, osb, omem, p["w_o_mla"], p["w_o_sb"], p["w_o_mem"], p["w_out"],
      p["g_ffn"], p["w_r_hi"], p["w_r_lo"], p["b_r"])


def _row_copy(src_hbm, idx, dst_vmem, row, sem):
    return pltpu.make_async_copy(src_hbm.at[pl.ds(idx, 1), :], dst_vmem.at[pl.ds(row, 1), :], sem)


def _gather_start(src_hbm, idx_ref, dst_vmem, sem, n):
    def body(r, carry):
        _row_copy(src_hbm, idx_ref[0, 0, r], dst_vmem, r, sem).start()
        return carry

    lax.fori_loop(0, n, body, 0, unroll=8)


def _gather_wait(src_hbm, dst_vmem, sem, n):
    def body(r, carry):
        _row_copy(src_hbm, 0, dst_vmem, r, sem).wait()
        return carry

    lax.fori_loop(0, n, body, 0, unroll=8)


def _moe_kernel(be_ref, nact_ref, tok_ref, tokn_ref, gate_ref, x_hbm, gffn_ref, wgu_ref, bgu_ref, wd_ref, bd_ref,
                o_ref, xbuf, sem, wgu_sc, wd_sc):
    b = pl.program_id(0)
    bm = o_ref.shape[0]
    nact = nact_ref[0]
    slot = b % 2

    @pl.when(b >= nact)
    def _():
        o_ref[...] = jnp.zeros(o_ref.shape, o_ref.dtype)

    @pl.when(b < nact)
    def _():
        @pl.when(b == 0)
        def _():
            _gather_start(x_hbm, tok_ref, xbuf.at[0], sem.at[0], bm)

        @pl.when(b + 1 < nact)
        def _():
            _gather_start(x_hbm, tokn_ref, xbuf.at[1 - slot], sem.at[1 - slot], bm)

        changed = jnp.logical_or(b == 0, be_ref[b] != be_ref[jnp.maximum(b - 1, 0)])

        @pl.when(changed)
        def _():
            wgu_sc[...] = wgu_ref[0].astype(BF16)
            wd_sc[...] = wd_ref[0].astype(BF16)

        _gather_wait(x_hbm, xbuf.at[slot], sem.at[slot], bm)
        xb = (_rms(xbuf[slot]) * gffn_ref[...]).astype(BF16)
        gu = _dot(xb, wgu_sc[...]) + bgu_ref[0]
        g = jnp.minimum(gu[:, :D_FF], SWIGLU_LIMIT)
        u = jnp.clip(gu[:, D_FF:], -SWIGLU_LIMIT, SWIGLU_LIMIT)
        hid = (u + 1.0) * (g / (1.0 + jnp.exp(-SWIGLU_ALPHA * g)))
        y = _dot(hid.astype(BF16), wd_sc[...]) + bd_ref[0]
        o_ref[...] = y * gate_ref[...]


def _moe_experts(x1, slot_tok, slot_gate, block_expert, n_active, p):
    n_blocks = block_expert.shape[0]
    bm = BM_MOE
    tok3 = slot_tok.reshape(n_blocks, 1, bm)
    grid_spec = pltpu.PrefetchScalarGridSpec(
        num_scalar_prefetch=2,
        grid=(n_blocks,),
        in_specs=[
            pl.BlockSpec((1, 1, bm), lambda b, be, na: (b, 0, 0), memory_space=pltpu.SMEM),
            pl.BlockSpec((1, 1, bm), lambda b, be, na: (jnp.minimum(b + 1, n_blocks - 1), 0, 0),
                         memory_space=pltpu.SMEM),
            pl.BlockSpec((bm, 1), lambda b, be, na: (b, 0)),
            pl.BlockSpec(memory_space=pl.ANY),
            pl.BlockSpec((1, D_MODEL), lambda b, be, na: (0, 0)),
            pl.BlockSpec((1, D_MODEL, 2 * D_FF), lambda b, be, na: (be[b], 0, 0)),
            pl.BlockSpec((1, 1, 2 * D_FF), lambda b, be, na: (be[b], 0, 0)),
            pl.BlockSpec((1, D_FF, D_MODEL), lambda b, be, na: (be[b], 0, 0)),
            pl.BlockSpec((1, 1, D_MODEL), lambda b, be, na: (be[b], 0, 0)),
        ],
        out_specs=pl.BlockSpec((bm, D_MODEL), lambda b, be, na: (b, 0)),
        scratch_shapes=[pltpu.VMEM((2, bm, D_MODEL), F32), pltpu.SemaphoreType.DMA((2,)),
                        pltpu.VMEM((D_MODEL, 2 * D_FF), BF16), pltpu.VMEM((D_FF, D_MODEL), BF16)],
    )
    return pl.pallas_call(
        _moe_kernel,
        grid_spec=grid_spec,
        out_shape=jax.ShapeDtypeStruct((n_blocks * bm, D_MODEL), F32),
        compiler_params=_cparams("arbitrary"),
        name="moe_experts",
    )(block_expert, n_active, tok3, tok3, slot_gate.reshape(n_blocks * bm, 1), x1, p["g_ffn"],
      p["w_gate_up"], p["b_gate_up"], p["w_down"], p["b_down"])


def _combine_kernel(pos_ref, posn_ref, y_hbm, x1_ref, o_ref, buf, sem):
    t = pl.program_id(0)
    nt = pl.num_programs(0)
    tt = o_ref.shape[0]
    n = TOP_K * tt
    slot = t % 2

    @pl.when(t == 0)
    def _():
        _gather_start(y_hbm, pos_ref, buf.at[0], sem.at[0], n)

    @pl.when(t + 1 < nt)
    def _():
        _gather_start(y_hbm, posn_ref, buf.at[1 - slot], sem.at[1 - slot], n)

    _gather_wait(y_hbm, buf.at[slot], sem.at[slot], n)
    acc = x1_ref[...]
    for k in range(TOP_K):
        acc = acc + buf[slot, k * tt:(k + 1) * tt, :]
    o_ref[...] = acc


def _moe_combine(x1, yb, pos):
    rows = x1.shape[0]
    tt = TT_COMBINE
    nt = rows // tt
    pos3 = pos.reshape(nt, tt, TOP_K).transpose(0, 2, 1).reshape(nt, 1, TOP_K * tt)
    return pl.pallas_call(
        _combine_kernel,
        grid=(nt,),
        in_specs=[pl.BlockSpec((1, 1, TOP_K * tt), lambda t: (t, 0, 0), memory_space=pltpu.SMEM),
                  pl.BlockSpec((1, 1, TOP_K * tt), lambda t: (jnp.minimum(t + 1, nt - 1), 0, 0),
                               memory_space=pltpu.SMEM),
                  pl.BlockSpec(memory_space=pl.ANY),
                  pl.BlockSpec((tt, D_MODEL), lambda t: (t, 0))],
        out_specs=pl.BlockSpec((tt, D_MODEL), lambda t: (t, 0)),
        out_shape=jax.ShapeDtypeStruct((rows, D_MODEL), F32),
        scratch_shapes=[pltpu.VMEM((2, TOP_K * tt, D_MODEL), F32), pltpu.SemaphoreType.DMA((2,))],
        compiler_params=_cparams("arbitrary"),
        name="moe_combine",
    )(pos3, pos3, yb, x1)


def _moe(x1, idx, gate, p):
    rows = x1.shape[0]
    n = rows * TOP_K
    bm = BM_MOE
    e = idx.reshape(n)
    onehot = (e[:, None] == jnp.arange(N_EXPERTS, dtype=jnp.int32)[None, :]).astype(jnp.int32)
    csum = jnp.cumsum(onehot, axis=0)
    rank = jnp.sum(csum * onehot, axis=1) - 1
    counts = csum[-1]
    padded = (counts + bm - 1) // bm * bm
    pend = jnp.cumsum(padded)
    dest = (pend - padded)[e] + rank
    n_blocks = -(-n // bm) + N_EXPERTS
    slot_tok = jnp.zeros((n_blocks * bm,), jnp.int32).at[dest].set(jnp.arange(n, dtype=jnp.int32) // TOP_K)
    slot_gate = jnp.zeros((n_blocks * bm,), F32).at[dest].set(gate.reshape(n))
    block_expert = jnp.minimum(
        jnp.searchsorted(pend, jnp.arange(n_blocks, dtype=jnp.int32) * bm, side="right"), N_EXPERTS - 1
    ).astype(jnp.int32)
    n_active = (pend[-1:] // bm).astype(jnp.int32)
    yb = _moe_experts(x1, slot_tok, slot_gate, block_expert, n_active, p)
    return _moe_combine(x1, yb, dest.reshape(rows, TOP_K).astype(jnp.int32))


def _pack_layer(l, g_attn, w_in, g_q_lat, w_q_b, g_q_nope, g_q_rope, g_kv_lat, w_kv_b, g_k_nope, g_k_rope, g_mem,
                w_mem_kv, g_mem_q, g_mem_k, w_o_mla, w_o_sb, w_o_mem, w_out, g_ffn, w_router, b_router,
                w_gate_up, b_gate_up, w_down, b_down):
    w = w_in[l]
    off = np.cumsum((Q_LORA, KV_LORA, ROPE_DIM, SB_W, SB_W, SB_W, MEM_W))
    o_kpe, o_sbq = int(off[1]), int(off[2])
    o_gate = int(off[6])
    kpe1 = w[:, o_kpe:o_kpe + ROPE_HALF]
    kpe2 = w[:, o_kpe + ROPE_HALF:o_kpe + ROPE_DIM]
    zl = jnp.zeros((D_MODEL, KPE_LANE), F32)
    zr = jnp.zeros((D_MODEL, LANES - KPE_LANE - ROPE_DIM), F32)
    w_a = jnp.concatenate([w[:, :o_kpe], zl, kpe1, kpe2, zr, zl, -kpe2, kpe1, zr, w[:, o_sbq:o_gate]], axis=1)
    wq = w_q_b[l]
    z_n = jnp.zeros((Q_LORA, H_MLA, NOPE_DIM), F32)
    wq_sw = jnp.concatenate([z_n, -wq[..., NOPE_DIM + ROPE_HALF:], wq[..., NOPE_DIM:NOPE_DIM + ROPE_HALF]], axis=-1)
    lane = np.arange(QK_W)
    ind = np.zeros((QK_W, LANES), np.float32)
    ind[lane, lane // QK_DIM] = 1.0
    pk = np.zeros((ROPE_DIM, QK_W), np.float32)
    for hd in range(H_MLA):
        pk[np.arange(ROPE_DIM), hd * QK_DIM + NOPE_DIM + np.arange(ROPE_DIM)] = 1.0
    wkv = w_kv_b[l]
    w_k = jnp.concatenate([wkv[..., :NOPE_DIM], jnp.zeros((KV_LORA, H_MLA, ROPE_DIM), F32)], axis=-1)
    g_q = jnp.tile(jnp.concatenate([g_q_nope[l], g_q_rope[l], g_q_rope[l]]), H_MLA) * (QK_DIM ** -0.5)
    g_k = jnp.tile(jnp.concatenate([g_k_nope[l], g_k_rope[l], g_k_rope[l]]), H_MLA)
    wm = w_mem_kv[l]
    w_mem = jnp.concatenate([wm[..., :MEM_DIM].reshape(D_MODEL, MEM_W), wm[..., MEM_DIM:].reshape(D_MODEL, MEM_W)], 1)
    w_r = jnp.concatenate([w_router[l], jnp.zeros((D_MODEL, LANES - N_EXPERTS), F32)], axis=1)
    w_r_hi = w_r.astype(BF16)
    b_r = jnp.concatenate([b_router[l].astype(F32), jnp.full((LANES - N_EXPERTS,), NEG_BIG, F32)])
    return {
        "g_attn": g_attn[l][None], "w_a": w_a.astype(BF16), "g_q_lat": g_q_lat[l][None],
        "w_q": wq.reshape(Q_LORA, QK_W).astype(BF16), "w_q_sw": wq_sw.reshape(Q_LORA, QK_W).astype(BF16),
        "ind96": jnp.asarray(ind, BF16), "ind96_t": jnp.asarray(ind.T, BF16), "g_q": g_q[None],
        "g_kv_lat": g_kv_lat[l][None], "g_mem_q": jnp.tile(g_mem_q[l], H_MEM)[None] * (MEM_DIM ** -0.5),
        "w_k": w_k.reshape(KV_LORA, QK_W).astype(BF16), "p_kpe": jnp.asarray(pk, BF16), "g_k": g_k[None],
        "w_v": wkv[..., NOPE_DIM:].reshape(KV_LORA, MLA_W).astype(BF16),
        "g_mem": g_mem[l][None], "w_mem": w_mem.astype(BF16), "g_mem_k": jnp.tile(g_mem_k[l], H_MEM)[None],
        "w_g": w[:, o_gate:].astype(BF16),
        "w_o_mla": w_o_mla[l].reshape(H_MLA, V_DIM, D_MODEL).astype(BF16),
        "w_o_sb": w_o_sb[l].reshape(H_SB, SB_DIM, D_MODEL).astype(BF16),
        "w_o_mem": w_o_mem[l].astype(BF16), "w_out": w_out[l].astype(BF16), "g_ffn": g_ffn[l][None],
        "w_r_hi": w_r_hi, "w_r_lo": (w_r - w_r_hi.astype(F32)).astype(BF16), "b_r": b_r[None],
        "w_gate_up": w_gate_up[l], "b_gate_up": b_gate_up[l][:, None, :], "w_down": w_down[l],
        "b_down": b_down[l][:, None, :],
    }


def kernel(x_prompt, x_sample, mem_prompt, cache_mla_latent, cache_mla_rope, cache_sb_k, cache_sb_v, cache_mem_k, cache_mem_v, g_attn, w_in, g_q_lat, w_q_b, g_q_nope, g_q_rope, g_kv_lat, w_kv_b, g_k_nope, g_k_rope, g_mem, w_mem_kv, g_mem_q, g_mem_k, w_o_mla, w_o_sb, w_o_mem, w_out, g_ffn, w_router, b_router, w_gate_up, b_gate_up, w_down, b_down):
    depth = g_attn.shape[0]
    bp, sp, _ = x_prompt.shape
    bs, ss, _ = x_sample.shape
    past = cache_mla_latent.shape[2]
    n_mem = mem_prompt.shape[1]
    assert bp == 1 and sp % TQ_MLA == 0 and sp % TM_KV == 0 and (bs * ss) % TM_PROJ == 0 and TM_PROJ % ss == 0
    rows_s = bs * ss
    c_sbk = cache_sb_k.reshape(depth, bs, past, SB_W)
    c_sbv = cache_sb_v.reshape(depth, bs, past, SB_W)
    c_mk = cache_mem_k.reshape(depth, bs, n_mem, MEM_W)
    c_mv = cache_mem_v.reshape(depth, bs, n_mem, MEM_W)
    xp = x_prompt.reshape(sp, D_MODEL)
    xs = x_sample.reshape(rows_s, D_MODEL)
    mem2d = mem_prompt.reshape(n_mem, D_MODEL)
    base_p, off_p = np.arange(sp // TM_PROJ) * TM_PROJ, np.arange(TM_PROJ)
    base_s, off_s = np.full((rows_s // TM_PROJ,), past), np.arange(TM_PROJ) % ss
    outs = [[] for _ in range(10)]
    for l in range(depth):
        p = _pack_layer(l, g_attn, w_in, g_q_lat, w_q_b, g_q_nope, g_q_rope, g_kv_lat, w_kv_b, g_k_nope, g_k_rope,
                        g_mem, w_mem_kv, g_mem_q, g_mem_k, w_o_mla, w_o_sb, w_o_mem, w_out, g_ffn, w_router,
                        b_router, w_gate_up, b_gate_up, w_down, b_down)
        q, lat, kpe, sq, sk, sv, skh, svh, mq = _proj(xp, base_p, off_p, p)
        kh, vh = _kv_expand(lat, kpe, p)
        o_mla = _mla_attn(q, kh, vh)
        o_sb = _sb_attn(sq, skh, svh)
        mk, mv = _mem_kv(mem2d, p)
        o_mem = _mem_attn(mq, mk, mv)
        x1p, idx_p, gate_p = _merge(xp, o_mla, o_sb, o_mem, p)
        q_s, lat_s, kpe_s, sq_s, sk_s, sv_s, skh_s, svh_s, mq_s = _proj(xs, base_s, off_s, p)
        kn, vn = _kv_expand(lat_s, kpe_s, p)
        o_mla_s, o_sb_s, o_mem_s = _sample_attn(q_s, kn, vn, sq_s, skh_s, svh_s, mq_s, cache_mla_latent,
                                                cache_mla_rope, c_sbk, c_sbv, c_mk, c_mv, l, p)
        x1s, idx_s, gate_s = _merge(xs, o_mla_s, o_sb_s, o_mem_s, p)
        x1 = jnp.concatenate([x1p, x1s], axis=0)
        idx = jnp.concatenate([idx_p[:, :TOP_K], idx_s[:, :TOP_K]], axis=0)
        gate = jnp.concatenate([gate_p[:, :TOP_K], gate_s[:, :TOP_K]], axis=0)
        y = _moe(x1, idx, gate, p)
        xp, xs = y[:sp], y[sp:]
        for lst, val in zip(outs, (lat.reshape(bp, sp, KV_LORA), kpe.reshape(bp, sp, ROPE_DIM),
                                   sk.reshape(bp, sp, H_SB, SB_DIM), sv.reshape(bp, sp, H_SB, SB_DIM),
                                   mk.reshape(bp, n_mem, H_MEM, MEM_DIM), mv.reshape(bp, n_mem, H_MEM, MEM_DIM),
                                   lat_s.reshape(bs, ss, KV_LORA), kpe_s.reshape(bs, ss, ROPE_DIM),
                                   sk_s.reshape(bs, ss, H_SB, SB_DIM), sv_s.reshape(bs, ss, H_SB, SB_DIM))):
            lst.append(val)
    return (xp.reshape(bp, sp, D_MODEL), xs.reshape(bs, ss, D_MODEL)) + tuple(jnp.stack(o) for o in outs)
```

```python
import functools

import numpy as np
import jax
import jax.numpy as jnp
from jax import lax
from jax.experimental import pallas as pl
from jax.experimental.pallas import tpu as pltpu

F32 = jnp.float32
BF16 = jnp.bfloat16

D_MODEL = 1024
CHUNK = 64
EPS = 1e-6
H_MLA = 8
NOPE_DIM = 64
ROPE_DIM = 32
ROPE_HALF = ROPE_DIM // 2
V_DIM = 64
Q_LORA = 384
KV_LORA = 256
ROPE_BASE = 10000.0
QK_DIM = NOPE_DIM + ROPE_DIM
QK_W = H_MLA * QK_DIM
H_SB = 8
SB_DIM = 64
H_MEM = 4
MEM_DIM = 128
N_EXPERTS = 32
TOP_K = 4
D_FF = 1024
SWIGLU_LIMIT = 7.0
SWIGLU_ALPHA = 1.702
MLA_W = H_MLA * V_DIM
SB_W = H_SB * SB_DIM
MEM_W = H_MEM * MEM_DIM

LANES = 128
VT_ROWS = LANES
MLA_CHAINS = 4
MLA_KSPLIT = 2
LOG2E = 1.4426950408889634
A_QLAT = 0
A_KVLAT = A_QLAT + Q_LORA
A_KPE = A_KVLAT + KV_LORA
A_KPE_SW = A_KPE + LANES
A_SBQ = A_KPE_SW + LANES
A_SBK = A_SBQ + SB_W
A_SBV = A_SBK + SB_W
A_MEMQ = A_SBV + SB_W
A_COLS = A_MEMQ + MEM_W
KPE_LANE = NOPE_DIM

TM_PROJ = 256
TM_KV = 512
TQ_MLA = 1024
TQ_SB = 256
TM_MEM = 512
TM_MERGE = 256
BM_MOE = 256
TT_COMBINE = 256
VMEM_LIMIT = 56 * 1024 * 1024

SB_UNDERFLOW = -120.0
NEG_BIG = -3.0e38


def _cparams(*sem):
    return pltpu.CompilerParams(dimension_semantics=sem, vmem_limit_bytes=VMEM_LIMIT)


def _split(x):
    hi = x.astype(BF16)
    lo = (x - hi.astype(F32)).astype(BF16)
    return hi, lo


def _dot(a, b):
    return jnp.dot(a, b, preferred_element_type=F32)


def _dot2(x, m):
    hi, lo = _split(x)
    return _dot(hi, m) + _dot(lo, m)


def _dot_nt(a, b):
    return lax.dot_general(a, b, (((1,), (1,)), ((), ())), preferred_element_type=F32)


def _rms(x):
    return x * lax.rsqrt(jnp.mean(x * x, axis=-1, keepdims=True) + EPS)


def _softplus(z):
    return jnp.maximum(z, 0.0) + jnp.log1p(jnp.exp(-jnp.abs(z)))


def _full(shape):
    n = len(shape)
    return pl.BlockSpec(shape, lambda *_: (0,) * n)


def _proj_kernel(x_ref, gattn_ref, wa_ref, gqlat_ref, wq_ref, wqs_ref, ind_ref, indt_ref, gq_ref, gkv_ref,
                 bcs_ref, ocos_ref, osin_ref, gmq_ref,
                 q_ref, lat_ref, kpe_ref, sq_ref, sk_ref, sv_ref, skh_ref, svh_ref, mq_ref):
    h = (_rms(x_ref[...]) * gattn_ref[...]).astype(BF16)
    z = _dot(h, wa_ref[...])
    bc = bcs_ref[0, 0:1, :]
    bs = bcs_ref[0, 1:2, :]
    oc = ocos_ref[...]
    osn = osin_ref[...]
    cos_f = bc * oc - bs * osn
    sin_f = bs * oc + bc * osn
    qn = (_rms(z[:, A_QLAT:A_QLAT + Q_LORA]) * gqlat_ref[...]).astype(BF16)
    qr = _dot(qn, wq_ref[...]) * cos_f + _dot(qn, wqs_ref[...]) * sin_f
    ssq = _dot2(qr * qr, ind_ref[...])
    inv = lax.rsqrt(ssq * (1.0 / QK_DIM) + EPS)
    qo = qr * _dot2(inv, indt_ref[...]) * gq_ref[...]
    for hd in range(H_MLA):
        q_ref[hd] = qo[:, hd * QK_DIM:(hd + 1) * QK_DIM].astype(BF16)
    lat_ref[...] = _rms(z[:, A_KVLAT:A_KVLAT + KV_LORA]) * gkv_ref[...]
    kr = z[:, A_KPE:A_KPE + LANES] * cos_f[:, :LANES] + z[:, A_KPE_SW:A_KPE_SW + LANES] * sin_f[:, :LANES]
    kpe_ref[...] = kr[:, KPE_LANE:KPE_LANE + ROPE_DIM]
    sbq = z[:, A_SBQ:A_SBQ + SB_W] * (SB_DIM ** -0.5)
    sbk = z[:, A_SBK:A_SBK + SB_W]
    sbv = z[:, A_SBV:A_SBV + SB_W]
    sk_ref[...] = sbk
    sv_ref[...] = sbv
    for hd in range(H_SB):
        sl = slice(hd * SB_DIM, (hd + 1) * SB_DIM)
        sq_ref[hd] = sbq[:, sl].astype(BF16)
        skh_ref[hd] = sbk[:, sl].astype(BF16)
        svh_ref[hd] = sbv[:, sl].astype(BF16)
    mqs = []
    for hd in range(H_MEM):
        mqs.append(_rms(z[:, A_MEMQ + hd * MEM_DIM:A_MEMQ + (hd + 1) * MEM_DIM]))
    mq_ref[...] = (jnp.concatenate(mqs, axis=-1) * gmq_ref[...]).astype(BF16)


def _rope_tables(base_pos, off_pos):
    lane = np.arange(QK_W) % QK_DIM
    inv_freq = ROPE_BASE ** (-np.arange(ROPE_HALF, dtype=np.float64) / ROPE_HALF)
    freq = np.where(lane >= NOPE_DIM, inv_freq[(lane - NOPE_DIM) % ROPE_HALF], 0.0)
    ab = np.asarray(base_pos, np.float64)[:, None] * freq
    ao = np.asarray(off_pos, np.float64)[:, None] * freq
    bcs = np.stack([np.cos(ab), np.sin(ab)], axis=1).astype(np.float32)
    return jnp.asarray(bcs), jnp.asarray(np.cos(ao), F32), jnp.asarray(np.sin(ao), F32)


def _proj(x2d, base_pos, off_pos, p):
    rows = x2d.shape[0]
    tm = TM_PROJ
    nt = rows // tm
    bcs, ocos, osin = _rope_tables(base_pos, off_pos)
    row = lambda w: pl.BlockSpec((tm, w), lambda i: (i, 0))
    hm = lambda d: pl.BlockSpec((H_MLA, tm, d), lambda i: (0, i, 0))
    out_shape = (
        jax.ShapeDtypeStruct((H_MLA, rows, QK_DIM), BF16),
        jax.ShapeDtypeStruct((rows, KV_LORA), F32),
        jax.ShapeDtypeStruct((rows, ROPE_DIM), F32),
        jax.ShapeDtypeStruct((H_SB, rows, SB_DIM), BF16),
        jax.ShapeDtypeStruct((rows, SB_W), F32),
        jax.ShapeDtypeStruct((rows, SB_W), F32),
        jax.ShapeDtypeStruct((H_SB, rows, SB_DIM), BF16),
        jax.ShapeDtypeStruct((H_SB, rows, SB_DIM), BF16),
        jax.ShapeDtypeStruct((rows, MEM_W), BF16),
    )
    return pl.pallas_call(
        _proj_kernel,
        grid=(nt,),
        in_specs=[row(D_MODEL), _full((1, D_MODEL)), _full((D_MODEL, A_COLS)), _full((1, Q_LORA)),
                  _full((Q_LORA, QK_W)), _full((Q_LORA, QK_W)), _full((QK_W, LANES)), _full((LANES, QK_W)),
                  _full((1, QK_W)), _full((1, KV_LORA)),
                  pl.BlockSpec((1, 2, QK_W), lambda i: (i, 0, 0)), _full((tm, QK_W)), _full((tm, QK_W)),
                  _full((1, MEM_W))],
        out_specs=(hm(QK_DIM), row(KV_LORA), row(ROPE_DIM), hm(SB_DIM), row(SB_W), row(SB_W), hm(SB_DIM),
                   hm(SB_DIM), row(MEM_W)),
        out_shape=out_shape,
        compiler_params=_cparams("parallel"),
        name="proj",
    )(x2d, p["g_attn"], p["w_a"], p["g_q_lat"], p["w_q"], p["w_q_sw"], p["ind96"], p["ind96_t"], p["g_q"],
      p["g_kv_lat"], bcs, ocos, osin, p["g_mem_q"])


def _expand_keys(lat, kpe, wk, pk, ind, indt, gk):
    kf = _dot(lat.astype(BF16), wk) + _dot2(kpe, pk)
    inv = lax.rsqrt(_dot2(kf * kf, ind) * (1.0 / QK_DIM) + EPS)
    return kf * _dot2(inv, indt) * gk


def _kv_expand_kernel(lat_ref, kpe_ref, wk_ref, pk_ref, ind_ref, indt_ref, gk_ref, wv_ref, wvt_ref,
                      k_ref, v_ref, vt_ref):
    lat = lat_ref[...]
    latb = lat.astype(BF16)
    ko = _expand_keys(lat, kpe_ref[...], wk_ref[...], pk_ref[...], ind_ref[...], indt_ref[...], gk_ref[...])
    v = _dot(latb, wv_ref[...])
    vt = _dot_nt(wvt_ref[...], latb)
    ones_row = (lax.broadcasted_iota(jnp.int32, (VT_ROWS - V_DIM, vt.shape[1]), 0) == 0).astype(F32)
    for hd in range(H_MLA):
        k_ref[hd] = ko[:, hd * QK_DIM:(hd + 1) * QK_DIM].astype(BF16)
        v_ref[hd] = v[:, hd * V_DIM:(hd + 1) * V_DIM].astype(BF16)
        vt_ref[hd] = jnp.concatenate([vt[hd * V_DIM:(hd + 1) * V_DIM, :], ones_row], axis=0).astype(BF16)


def _kv_expand(lat, kpe, p):
    rows = lat.shape[0]
    tm = min(TM_KV, rows)
    row = lambda w: pl.BlockSpec((tm, w), lambda i: (i, 0))
    hm = lambda d: pl.BlockSpec((H_MLA, tm, d), lambda i: (0, i, 0))
    return pl.pallas_call(
        _kv_expand_kernel,
        grid=(rows // tm,),
        in_specs=[row(KV_LORA), row(ROPE_DIM), _full((KV_LORA, QK_W)), _full((ROPE_DIM, QK_W)),
                  _full((QK_W, LANES)), _full((LANES, QK_W)), _full((1, QK_W)), _full((KV_LORA, MLA_W)),
                  _full((MLA_W, KV_LORA))],
        out_specs=(hm(QK_DIM), hm(V_DIM), pl.BlockSpec((H_MLA, VT_ROWS, tm), lambda i: (0, 0, i))),
        out_shape=(jax.ShapeDtypeStruct((H_MLA, rows, QK_DIM), BF16),
                   jax.ShapeDtypeStruct((H_MLA, rows, V_DIM), BF16),
                   jax.ShapeDtypeStruct((H_MLA, VT_ROWS, rows), BF16)),
        compiler_params=_cparams("parallel"),
        name="kv_expand",
    )(lat, kpe, p["w_k"], p["p_kpe"], p["ind96"], p["ind96_t"], p["g_k"], p["w_v"], p["w_v_t"])


def _mla_update(st, vt, m, acc):
    m_new = jnp.maximum(m, jnp.max(st, axis=0, keepdims=True))
    pr = jnp.exp2(st - m_new).astype(BF16)
    return m_new, acc * jnp.exp2(m - m_new) + _dot(vt, pr)


def _mla_step(qc, k, vt, m, acc, mask):
    st = _dot_nt(k, qc)
    if mask is not None:
        st = jnp.where(mask, st, NEG_BIG)
    return _mla_update(st, vt, m, acc)


def _mla_attn_kernel(q_ref, k_ref, vt_ref, o_ref):
    i = pl.program_id(1)
    tq = q_ref.shape[1]
    tc = tq // MLA_CHAINS
    tk = tq // MLA_KSPLIT
    nsc = 2 * MLA_CHAINS
    qs = [q_ref[0, c * tc:(c + 1) * tc, :] for c in range(MLA_CHAINS)]

    def scores(t):
        k = k_ref[0, pl.ds(pl.multiple_of(t * tk, tk), tk), :]
        out = []
        for c in range(MLA_CHAINS):
            st = _dot_nt(k, qs[c])
            out += [st, jnp.max(st, axis=0, keepdims=True)]
        return out

    def update(t, sc, st):
        vt = vt_ref[0, :, pl.ds(pl.multiple_of(t * tk, tk), tk)]
        out = []
        for c in range(MLA_CHAINS):
            m, acc = st[2 * c], st[2 * c + 1]
            m_new = jnp.maximum(m, sc[2 * c + 1])
            pr = jnp.exp2(sc[2 * c] - m_new).astype(BF16)
            out += [m_new, acc * jnp.exp2(m - m_new) + _dot(vt, pr)]
        return out

    state = []
    for _ in range(MLA_CHAINS):
        state += [jnp.full((1, tc), NEG_BIG, F32), jnp.zeros((VT_ROWS, tc), F32)]

    def body(j, carry):
        sc, st = carry[:nsc], carry[nsc:]
        for u in range(MLA_KSPLIT):
            t = j * MLA_KSPLIT + u
            nxt = scores(t + 1)
            st = update(t, sc, st)
            sc = nxt
        return tuple(sc) + tuple(st)

    carry = lax.fori_loop(0, i, body, tuple(scores(0)) + tuple(state))
    sc, state = carry[:nsc], carry[nsc:]
    key_chunk = lax.broadcasted_iota(jnp.int32, (tk, tc), 0) // CHUNK
    qry_chunk = lax.broadcasted_iota(jnp.int32, (tk, tc), 1) // CHUNK
    for u in range(MLA_KSPLIT):
        t = i * MLA_KSPLIT + u
        nxt = scores(t + 1) if u + 1 < MLA_KSPLIT else None
        masked = []
        for c in range(MLA_CHAINS):
            st = jnp.where(key_chunk + u * (tk // CHUNK) <= qry_chunk + c * (tc // CHUNK), sc[2 * c], NEG_BIG)
            masked += [st, jnp.max(st, axis=0, keepdims=True)]
        state = update(t, masked, state)
        sc = nxt
    for c in range(MLA_CHAINS):
        ot = state[2 * c + 1].T
        o_ref[0, c * tc:(c + 1) * tc, :] = (ot[:, :V_DIM] / ot[:, V_DIM:V_DIM + 1]).astype(BF16)


def _mla_attn(q, k, vt):
    rows = q.shape[1]
    tq = min(TQ_MLA, rows)
    return pl.pallas_call(
        _mla_attn_kernel,
        grid=(H_MLA, rows // tq),
        in_specs=[pl.BlockSpec((1, tq, QK_DIM), lambda h, i: (h, i, 0)),
                  pl.BlockSpec((1, rows, QK_DIM), lambda h, i: (h, 0, 0)),
                  pl.BlockSpec((1, VT_ROWS, rows), lambda h, i: (h, 0, 0))],
        out_specs=pl.BlockSpec((1, tq, V_DIM), lambda h, i: (h, i, 0)),
        out_shape=jax.ShapeDtypeStruct((H_MLA, rows, V_DIM), BF16),
        compiler_params=_cparams("parallel", "arbitrary"),
        name="mla_attn",
    )(q, k, vt)


def _sb_attn_kernel(q_ref, k_ref, v_ref, tri_ref, o_ref, acc_sc, c_sc):
    i = pl.program_id(1)
    tq = q_ref.shape[1]
    q = q_ref[0]
    tri = tri_ref[...]
    start = pl.multiple_of(i * tq, tq)
    z = _dot_nt(q, k_ref[0, pl.ds(start, tq), :])
    before = lax.broadcasted_iota(jnp.int32, z.shape, 1) < lax.broadcasted_iota(jnp.int32, z.shape, 0)
    lk = jnp.where(before, -_softplus(z), 0.0)
    a = jnp.where(before, jnp.exp(z + lk + _dot2(lk, tri)), 0.0)
    acc_sc[...] = _dot(a.astype(BF16), v_ref[0, pl.ds(start, tq), :])
    c0 = jnp.sum(lk, axis=-1, keepdims=True)
    c_sc[...] = c0

    def cond(carry):
        j, cmax = carry
        return jnp.logical_and(j >= 0, cmax > SB_UNDERFLOW)

    def body(carry):
        j, _ = carry
        st = pl.multiple_of(j * tq, tq)
        zz = _dot_nt(q, k_ref[0, pl.ds(st, tq), :])
        c = c_sc[...]
        lkk = -_softplus(zz)
        aa = jnp.exp(zz + lkk + _dot2(lkk, tri) + c)
        acc_sc[...] += _dot(aa.astype(BF16), v_ref[0, pl.ds(st, tq), :])
        cn = c + jnp.sum(lkk, axis=-1, keepdims=True)
        c_sc[...] = cn
        return j - 1, jnp.max(cn)

    lax.while_loop(cond, body, (i - 1, jnp.max(c0)))
    o_ref[0] = acc_sc[...].astype(BF16)


def _tri(n):
    return jnp.asarray(np.tril(np.ones((n, n), np.float32), -1), BF16)


def _sb_attn(q, k, v):
    rows = q.shape[1]
    tq = min(TQ_SB, rows)
    return pl.pallas_call(
        _sb_attn_kernel,
        grid=(H_SB, rows // tq),
        in_specs=[pl.BlockSpec((1, tq, SB_DIM), lambda h, i: (h, i, 0)),
                  pl.BlockSpec((1, rows, SB_DIM), lambda h, i: (h, 0, 0)),
                  pl.BlockSpec((1, rows, SB_DIM), lambda h, i: (h, 0, 0)),
                  _full((tq, tq))],
        out_specs=pl.BlockSpec((1, tq, SB_DIM), lambda h, i: (h, i, 0)),
        out_shape=jax.ShapeDtypeStruct((H_SB, rows, SB_DIM), BF16),
        scratch_shapes=[pltpu.VMEM((tq, SB_DIM), F32), pltpu.VMEM((tq, 1), F32)],
        compiler_params=_cparams("parallel", "arbitrary"),
        name="sb_attn",
    )(q, k, v, _tri(tq))


def _mem_kv_kernel(mem_ref, gmem_ref, w_ref, gk_ref, mk_ref, mv_ref):
    mn = (_rms(mem_ref[...]) * gmem_ref[...]).astype(BF16)
    kv = _dot(mn, w_ref[...])
    ks = [_rms(kv[:, hd * MEM_DIM:(hd + 1) * MEM_DIM]) for hd in range(H_MEM)]
    mk_ref[...] = jnp.concatenate(ks, axis=-1) * gk_ref[...]
    mv_ref[...] = kv[:, MEM_W:]


def _mem_kv(mem2d, p):
    n = mem2d.shape[0]
    return pl.pallas_call(
        _mem_kv_kernel,
        grid=(1,),
        in_specs=[_full((n, D_MODEL)), _full((1, D_MODEL)), _full((D_MODEL, 2 * MEM_W)), _full((1, MEM_W))],
        out_specs=(_full((n, MEM_W)), _full((n, MEM_W))),
        out_shape=(jax.ShapeDtypeStruct((n, MEM_W), F32), jax.ShapeDtypeStruct((n, MEM_W), F32)),
        compiler_params=_cparams("arbitrary"),
        name="mem_kv",
    )(mem2d, p["g_mem"], p["w_mem"], p["g_mem_k"])


def _mem_heads(mq, mk, mv):
    outs = []
    for hd in range(H_MEM):
        sl = slice(hd * MEM_DIM, (hd + 1) * MEM_DIM)
        s = _dot_nt(mq[:, sl], mk[:, sl])
        pr = jnp.exp(s - jnp.max(s, axis=-1, keepdims=True))
        o = _dot(pr.astype(BF16), mv[:, sl])
        outs.append(o / jnp.sum(pr, axis=-1, keepdims=True))
    return jnp.concatenate(outs, axis=-1)


def _mem_attn_kernel(mq_ref, mk_ref, mv_ref, o_ref):
    o_ref[...] = _mem_heads(mq_ref[...], mk_ref[...].astype(BF16), mv_ref[...].astype(BF16)).astype(BF16)


def _mem_attn(mq, mk, mv):
    rows = mq.shape[0]
    tm = min(TM_MEM, rows)
    n = mk.shape[0]
    return pl.pallas_call(
        _mem_attn_kernel,
        grid=(rows // tm,),
        in_specs=[pl.BlockSpec((tm, MEM_W), lambda i: (i, 0)), _full((n, MEM_W)), _full((n, MEM_W))],
        out_specs=pl.BlockSpec((tm, MEM_W), lambda i: (i, 0)),
        out_shape=jax.ShapeDtypeStruct((rows, MEM_W), BF16),
        compiler_params=_cparams("parallel"),
        name="mem_attn",
    )(mq, mk, mv)


def _sample_attn_kernel(past, q_ref, kn_ref, vn_ref, sq_ref, skn_ref, svn_ref, mq_ref,
                        clat_ref, crope_ref, csk_ref, csv_ref, cmk_ref, cmv_ref,
                        wk_ref, pk_ref, ind_ref, indt_ref, gk_ref, wv_ref, tri_ref, tris_ref,
                        omla_ref, osb_ref, omem_ref):
    ds = q_ref.shape[1]
    tc = tri_ref.shape[0]
    clat = clat_ref[0, 0]
    kc = _expand_keys(clat, crope_ref[0, 0], wk_ref[...], pk_ref[...], ind_ref[...], indt_ref[...], gk_ref[...])
    vc = _dot(clat.astype(BF16), wv_ref[...])
    q_chunk = (past + lax.broadcasted_iota(jnp.int32, (ds, 1), 0)) // CHUNK
    ok_c = (lax.broadcasted_iota(jnp.int32, (ds, past), 1) // CHUNK) <= q_chunk
    ok_n = ((past + lax.broadcasted_iota(jnp.int32, (ds, ds), 1)) // CHUNK) <= q_chunk
    for hd in range(H_MLA):
        q = q_ref[hd]
        s1 = jnp.where(ok_c, _dot_nt(q, kc[:, hd * QK_DIM:(hd + 1) * QK_DIM].astype(BF16)), NEG_BIG)
        s2 = jnp.where(ok_n, _dot_nt(q, kn_ref[hd]), NEG_BIG)
        m = jnp.maximum(jnp.max(s1, axis=-1, keepdims=True), jnp.max(s2, axis=-1, keepdims=True))
        p1 = jnp.exp2(s1 - m)
        p2 = jnp.exp2(s2 - m)
        den = jnp.sum(p1, axis=-1, keepdims=True) + jnp.sum(p2, axis=-1, keepdims=True)
        o = _dot(p1.astype(BF16), vc[:, hd * V_DIM:(hd + 1) * V_DIM].astype(BF16)) + _dot(p2.astype(BF16), vn_ref[hd])
        omla_ref[hd] = (o / den).astype(BF16)

    tri = tri_ref[...]
    tris = tris_ref[...]
    before_n = lax.broadcasted_iota(jnp.int32, (ds, ds), 1) < lax.broadcasted_iota(jnp.int32, (ds, ds), 0)
    csk = csk_ref[0, 0]
    csv = csv_ref[0, 0]
    for hd in range(H_SB):
        sl = slice(hd * SB_DIM, (hd + 1) * SB_DIM)
        q = sq_ref[hd]
        z2 = _dot_nt(q, skn_ref[hd])
        l2 = jnp.where(before_n, -_softplus(z2), 0.0)
        a2 = jnp.where(before_n, jnp.exp(z2 + l2 + _dot2(l2, tris)), 0.0)
        o = _dot(a2.astype(BF16), svn_ref[hd])
        c = jnp.sum(l2, axis=-1, keepdims=True)
        z1 = _dot_nt(q, csk[:, sl].astype(BF16))
        l1 = -_softplus(z1)
        v1 = csv[:, sl].astype(BF16)
        for cb in reversed(range(past // tc)):
            cs = slice(cb * tc, (cb + 1) * tc)
            lc = l1[:, cs]
            a1 = jnp.exp(z1[:, cs] + lc + _dot2(lc, tri) + c)
            o = o + _dot(a1.astype(BF16), v1[cs, :])
            c = c + jnp.sum(lc, axis=-1, keepdims=True)
        osb_ref[hd] = o.astype(BF16)

    omem_ref[...] = _mem_heads(mq_ref[...], cmk_ref[0, 0].astype(BF16), cmv_ref[0, 0].astype(BF16)).astype(BF16)


def _sample_attn(q, kn, vn, sq, skn, svn, mq, c_lat, c_rope, c_sbk, c_sbv, c_mk, c_mv, layer, p):
    nb, past = c_lat.shape[1], c_lat.shape[2]
    ds = q.shape[1] // nb
    n_mem = c_mk.shape[2]
    tc = min(256, past)
    hm = lambda d: pl.BlockSpec((H_MLA, ds, d), lambda b: (0, b, 0))
    cache = lambda n, w: pl.BlockSpec((1, 1, n, w), lambda b: (layer, b, 0, 0))
    rows = q.shape[1]
    return pl.pallas_call(
        functools.partial(_sample_attn_kernel, past),
        grid=(nb,),
        in_specs=[hm(QK_DIM), hm(QK_DIM), hm(V_DIM), hm(SB_DIM), hm(SB_DIM), hm(SB_DIM),
                  pl.BlockSpec((ds, MEM_W), lambda b: (b, 0)),
                  cache(past, KV_LORA), cache(past, ROPE_DIM), cache(past, SB_W), cache(past, SB_W),
                  cache(n_mem, MEM_W), cache(n_mem, MEM_W),
                  _full((KV_LORA, QK_W)), _full((ROPE_DIM, QK_W)), _full((QK_W, LANES)), _full((LANES, QK_W)),
                  _full((1, QK_W)), _full((KV_LORA, MLA_W)), _full((tc, tc)), _full((ds, ds))],
        out_specs=(hm(V_DIM), hm(SB_DIM), pl.BlockSpec((ds, MEM_W), lambda b: (b, 0))),
        out_shape=(jax.ShapeDtypeStruct((H_MLA, rows, V_DIM), BF16),
                   jax.ShapeDtypeStruct((H_SB, rows, SB_DIM), BF16),
                   jax.ShapeDtypeStruct((rows, MEM_W), BF16)),
        compiler_params=_cparams("parallel"),
        name="sample_attn",
    )(q, kn, vn, sq, skn, svn, mq, c_lat, c_rope, c_sbk, c_sbv, c_mk, c_mv,
      p["w_k"], p["p_kpe"], p["ind96"], p["ind96_t"], p["g_k"], p["w_v"], _tri(tc), _tri(ds))


def _merge_kernel(x_ref, gattn_ref, wg_ref, omla_ref, osb_ref, omem_ref, woa_ref, wob_ref, wom_ref, wout_ref,
                  gffn_ref, wrh_ref, wrl_ref, br_ref, x1_ref, idx_ref, gate_ref):
    x = x_ref[...]
    h = (_rms(x) * gattn_ref[...]).astype(BF16)
    g = 1.0 / (1.0 + jnp.exp(-_dot(h, wg_ref[...])))
    ua = _dot(omla_ref[0], woa_ref[0])
    ub = _dot(osb_ref[0], wob_ref[0])
    for hd in range(1, H_MLA):
        ua = ua + _dot(omla_ref[hd], woa_ref[hd])
        ub = ub + _dot(osb_ref[hd], wob_ref[hd])
    um = _dot(omem_ref[...], wom_ref[...])
    u = g[:, :D_MODEL] * ua + g[:, D_MODEL:2 * D_MODEL] * ub + g[:, 2 * D_MODEL:] * um
    x1 = x + _dot(u.astype(BF16), wout_ref[...])
    x1_ref[...] = x1
    xh, xl = _split(_rms(x1) * gffn_ref[...])
    lg = _dot(xh, wrh_ref[...]) + _dot(xh, wrl_ref[...]) + _dot(xl, wrh_ref[...]) + br_ref[...]
    lane = lax.broadcasted_iota(jnp.int32, lg.shape, 1).astype(F32)
    vals, ids = [], []
    for _ in range(TOP_K):
        m = jnp.max(lg, axis=-1, keepdims=True)
        sel = jnp.min(jnp.where(lg == m, lane, float(LANES)), axis=-1, keepdims=True)
        vals.append(m)
        ids.append(sel)
        lg = jnp.where(lane == sel, NEG_BIG, lg)
    es = [jnp.exp(v - vals[0]) for v in vals]
    den = es[0] + es[1] + es[2] + es[3]
    idx_o = jnp.zeros(lg.shape, F32)
    gate_o = jnp.zeros(lg.shape, F32)
    for k in range(TOP_K):
        idx_o = jnp.where(lane == float(k), ids[k], idx_o)
        gate_o = jnp.where(lane == float(k), es[k] / den, gate_o)
    idx_ref[...] = idx_o.astype(jnp.int32)
    gate_ref[...] = gate_o


def _merge(x2d, omla, osb, omem, p):
    rows = x2d.shape[0]
    tm = TM_MERGE
    row = lambda w: pl.BlockSpec((tm, w), lambda i: (i, 0))
    hm = lambda d: pl.BlockSpec((H_MLA, tm, d), lambda i: (0, i, 0))
    return pl.pallas_call(
        _merge_kernel,
        grid=(rows // tm,),
        in_specs=[row(D_MODEL), _full((1, D_MODEL)), _full((D_MODEL, 3 * D_MODEL)), hm(V_DIM), hm(SB_DIM),
                  row(MEM_W), _full((H_MLA, V_DIM, D_MODEL)), _full((H_SB, SB_DIM, D_MODEL)),
                  _full((MEM_W, D_MODEL)), _full((D_MODEL, D_MODEL)), _full((1, D_MODEL)),
                  _full((D_MODEL, LANES)), _full((D_MODEL, LANES)), _full((1, LANES))],
        out_specs=(row(D_MODEL), row(LANES), row(LANES)),
        out_shape=(jax.ShapeDtypeStruct((rows, D_MODEL), F32), jax.ShapeDtypeStruct((rows, LANES), jnp.int32),
                   jax.ShapeDtypeStruct((rows, LANES), F32)),
        compiler_params=_cparams("parallel"),
        name="merge",
    )(x2d, p["g_attn"], p["w_g"], omla, osb, omem, p["w_o_mla"], p["w_o_sb"], p["w_o_mem"], p["w_out"],
      p["g_ffn"], p["w_r_hi"], p["w_r_lo"], p["b_r"])


def _row_copy(src_hbm, idx, dst_vmem, row, sem):
    return pltpu.make_async_copy(src_hbm.at[pl.ds(idx, 1), :], dst_vmem.at[pl.ds(row, 1), :], sem)


def _gather_start(src_hbm, idx_ref, dst_vmem, sem, n):
    def body(r, carry):
        _row_copy(src_hbm, idx_ref[0, 0, r], dst_vmem, r, sem).start()
        return carry

    lax.fori_loop(0, n, body, 0, unroll=8)


def _gather_wait(src_hbm, dst_vmem, sem, n):
    def body(r, carry):
        _row_copy(src_hbm, 0, dst_vmem, r, sem).wait()
        return carry

    lax.fori_loop(0, n, body, 0, unroll=8)


def _moe_kernel(be_ref, nact_ref, tok_ref, tokn_ref, gate_ref, x_hbm, gffn_ref, wgu_ref, bgu_ref, wd_ref, bd_ref,
                o_ref, xbuf, sem, wgu_sc, wd_sc):
    b = pl.program_id(0)
    bm = o_ref.shape[0]
    nact = nact_ref[0]
    slot = b % 2

    @pl.when(b >= nact)
    def _():
        o_ref[...] = jnp.zeros(o_ref.shape, o_ref.dtype)

    @pl.when(b < nact)
    def _():
        @pl.when(b == 0)
        def _():
            _gather_start(x_hbm, tok_ref, xbuf.at[0], sem.at[0], bm)

        @pl.when(b + 1 < nact)
        def _():
            _gather_start(x_hbm, tokn_ref, xbuf.at[1 - slot], sem.at[1 - slot], bm)

        changed = jnp.logical_or(b == 0, be_ref[b] != be_ref[jnp.maximum(b - 1, 0)])

        @pl.when(changed)
        def _():
            wgu_sc[...] = wgu_ref[0].astype(BF16)
            wd_sc[...] = wd_ref[0].astype(BF16)

        _gather_wait(x_hbm, xbuf.at[slot], sem.at[slot], bm)
        xb = (_rms(xbuf[slot]) * gffn_ref[...]).astype(BF16)
        gu = _dot(xb, wgu_sc[...]) + bgu_ref[0]
        g = jnp.minimum(gu[:, :D_FF], SWIGLU_LIMIT)
        u = jnp.clip(gu[:, D_FF:], -SWIGLU_LIMIT, SWIGLU_LIMIT)
        hid = (u + 1.0) * (g / (1.0 + jnp.exp(-SWIGLU_ALPHA * g)))
        y = _dot(hid.astype(BF16), wd_sc[...]) + bd_ref[0]
        o_ref[...] = y * gate_ref[...]


def _moe_experts(x1, slot_tok, slot_gate, block_expert, n_active, p):
    n_blocks = block_expert.shape[0]
    bm = BM_MOE
    tok3 = slot_tok.reshape(n_blocks, 1, bm)
    grid_spec = pltpu.PrefetchScalarGridSpec(
        num_scalar_prefetch=2,
        grid=(n_blocks,),
        in_specs=[
            pl.BlockSpec((1, 1, bm), lambda b, be, na: (b, 0, 0), memory_space=pltpu.SMEM),
            pl.BlockSpec((1, 1, bm), lambda b, be, na: (jnp.minimum(b + 1, n_blocks - 1), 0, 0),
                         memory_space=pltpu.SMEM),
            pl.BlockSpec((bm, 1), lambda b, be, na: (b, 0)),
            pl.BlockSpec(memory_space=pl.ANY),
            pl.BlockSpec((1, D_MODEL), lambda b, be, na: (0, 0)),
            pl.BlockSpec((1, D_MODEL, 2 * D_FF), lambda b, be, na: (be[b], 0, 0)),
            pl.BlockSpec((1, 1, 2 * D_FF), lambda b, be, na: (be[b], 0, 0)),
            pl.BlockSpec((1, D_FF, D_MODEL), lambda b, be, na: (be[b], 0, 0)),
            pl.BlockSpec((1, 1, D_MODEL), lambda b, be, na: (be[b], 0, 0)),
        ],
        out_specs=pl.BlockSpec((bm, D_MODEL), lambda b, be, na: (b, 0)),
        scratch_shapes=[pltpu.VMEM((2, bm, D_MODEL), F32), pltpu.SemaphoreType.DMA((2,)),
                        pltpu.VMEM((D_MODEL, 2 * D_FF), BF16), pltpu.VMEM((D_FF, D_MODEL), BF16)],
    )
    return pl.pallas_call(
        _moe_kernel,
        grid_spec=grid_spec,
        out_shape=jax.ShapeDtypeStruct((n_blocks * bm, D_MODEL), F32),
        compiler_params=_cparams("arbitrary"),
        name="moe_experts",
    )(block_expert, n_active, tok3, tok3, slot_gate.reshape(n_blocks * bm, 1), x1, p["g_ffn"],
      p["w_gate_up"], p["b_gate_up"], p["w_down"], p["b_down"])


def _combine_kernel(pos_ref, posn_ref, y_hbm, x1_ref, o_ref, buf, sem):
    t = pl.program_id(0)
    nt = pl.num_programs(0)
    tt = o_ref.shape[0]
    n = TOP_K * tt
    slot = t % 2

    @pl.when(t == 0)
    def _():
        _gather_start(y_hbm, pos_ref, buf.at[0], sem.at[0], n)

    @pl.when(t + 1 < nt)
    def _():
        _gather_start(y_hbm, posn_ref, buf.at[1 - slot], sem.at[1 - slot], n)

    _gather_wait(y_hbm, buf.at[slot], sem.at[slot], n)
    acc = x1_ref[...]
    for k in range(TOP_K):
        acc = acc + buf[slot, k * tt:(k + 1) * tt, :]
    o_ref[...] = acc


def _moe_combine(x1, yb, pos):
    rows = x1.shape[0]
    tt = TT_COMBINE
    nt = rows // tt
    pos3 = pos.reshape(nt, tt, TOP_K).transpose(0, 2, 1).reshape(nt, 1, TOP_K * tt)
    return pl.pallas_call(
        _combine_kernel,
        grid=(nt,),
        in_specs=[pl.BlockSpec((1, 1, TOP_K * tt), lambda t: (t, 0, 0), memory_space=pltpu.SMEM),
                  pl.BlockSpec((1, 1, TOP_K * tt), lambda t: (jnp.minimum(t + 1, nt - 1), 0, 0),
                               memory_space=pltpu.SMEM),
                  pl.BlockSpec(memory_space=pl.ANY),
                  pl.BlockSpec((tt, D_MODEL), lambda t: (t, 0))],
        out_specs=pl.BlockSpec((tt, D_MODEL), lambda t: (t, 0)),
        out_shape=jax.ShapeDtypeStruct((rows, D_MODEL), F32),
        scratch_shapes=[pltpu.VMEM((2, TOP_K * tt, D_MODEL), F32), pltpu.SemaphoreType.DMA((2,))],
        compiler_params=_cparams("arbitrary"),
        name="moe_combine",
    )(pos3, pos3, yb, x1)


def _moe(x1, idx, gate, p):
    rows = x1.shape[0]
    n = rows * TOP_K
    bm = BM_MOE
    e = idx.reshape(n)
    onehot = (e[:, None] == jnp.arange(N_EXPERTS, dtype=jnp.int32)[None, :]).astype(jnp.int32)
    csum = jnp.cumsum(onehot, axis=0)
    rank = jnp.sum(csum * onehot, axis=1) - 1
    counts = csum[-1]
    padded = (counts + bm - 1) // bm * bm
    pend = jnp.cumsum(padded)
    dest = (pend - padded)[e] + rank
    n_blocks = -(-n // bm) + N_EXPERTS
    slot_tok = jnp.zeros((n_blocks * bm,), jnp.int32).at[dest].set(jnp.arange(n, dtype=jnp.int32) // TOP_K)
    slot_gate = jnp.zeros((n_blocks * bm,), F32).at[dest].set(gate.reshape(n))
    starts = jnp.arange(n_blocks, dtype=jnp.int32) * bm
    block_expert = jnp.minimum(
        jnp.sum((pend[None, :] <= starts[:, None]).astype(jnp.int32), axis=1), N_EXPERTS - 1).astype(jnp.int32)
    n_active = (pend[-1:] // bm).astype(jnp.int32)
    yb = _moe_experts(x1, slot_tok, slot_gate, block_expert, n_active, p)
    return _moe_combine(x1, yb, dest.reshape(rows, TOP_K).astype(jnp.int32))


def _pack_layer(l, g_attn, w_in, g_q_lat, w_q_b, g_q_nope, g_q_rope, g_kv_lat, w_kv_b, g_k_nope, g_k_rope, g_mem,
                w_mem_kv, g_mem_q, g_mem_k, w_o_mla, w_o_sb, w_o_mem, w_out, g_ffn, w_router, b_router,
                w_gate_up, b_gate_up, w_down, b_down):
    w = w_in[l]
    off = np.cumsum((Q_LORA, KV_LORA, ROPE_DIM, SB_W, SB_W, SB_W, MEM_W))
    o_kpe, o_sbq = int(off[1]), int(off[2])
    o_gate = int(off[6])
    kpe1 = w[:, o_kpe:o_kpe + ROPE_HALF]
    kpe2 = w[:, o_kpe + ROPE_HALF:o_kpe + ROPE_DIM]
    zl = jnp.zeros((D_MODEL, KPE_LANE), F32)
    zr = jnp.zeros((D_MODEL, LANES - KPE_LANE - ROPE_DIM), F32)
    w_a = jnp.concatenate([w[:, :o_kpe], zl, kpe1, kpe2, zr, zl, -kpe2, kpe1, zr, w[:, o_sbq:o_gate]], axis=1)
    wq = w_q_b[l]
    z_n = jnp.zeros((Q_LORA, H_MLA, NOPE_DIM), F32)
    wq_sw = jnp.concatenate([z_n, -wq[..., NOPE_DIM + ROPE_HALF:], wq[..., NOPE_DIM:NOPE_DIM + ROPE_HALF]], axis=-1)
    lane = np.arange(QK_W)
    ind = np.zeros((QK_W, LANES), np.float32)
    ind[lane, lane // QK_DIM] = 1.0
    pk = np.zeros((ROPE_DIM, QK_W), np.float32)
    for hd in range(H_MLA):
        pk[np.arange(ROPE_DIM), hd * QK_DIM + NOPE_DIM + np.arange(ROPE_DIM)] = 1.0
    wkv = w_kv_b[l]
    w_k = jnp.concatenate([wkv[..., :NOPE_DIM], jnp.zeros((KV_LORA, H_MLA, ROPE_DIM), F32)], axis=-1)
    g_q = jnp.tile(jnp.concatenate([g_q_nope[l], g_q_rope[l], g_q_rope[l]]), H_MLA) * (QK_DIM ** -0.5 * LOG2E)
    g_k = jnp.tile(jnp.concatenate([g_k_nope[l], g_k_rope[l], g_k_rope[l]]), H_MLA)
    wm = w_mem_kv[l]
    w_mem = jnp.concatenate([wm[..., :MEM_DIM].reshape(D_MODEL, MEM_W), wm[..., MEM_DIM:].reshape(D_MODEL, MEM_W)], 1)
    w_r = jnp.concatenate([w_router[l], jnp.zeros((D_MODEL, LANES - N_EXPERTS), F32)], axis=1)
    w_r_hi = w_r.astype(BF16)
    b_r = jnp.concatenate([b_router[l].astype(F32), jnp.full((LANES - N_EXPERTS,), NEG_BIG, F32)])
    return {
        "g_attn": g_attn[l][None], "w_a": w_a.astype(BF16), "g_q_lat": g_q_lat[l][None],
        "w_q": wq.reshape(Q_LORA, QK_W).astype(BF16), "w_q_sw": wq_sw.reshape(Q_LORA, QK_W).astype(BF16),
        "ind96": jnp.asarray(ind, BF16), "ind96_t": jnp.asarray(ind.T, BF16), "g_q": g_q[None],
        "g_kv_lat": g_kv_lat[l][None], "g_mem_q": jnp.tile(g_mem_q[l], H_MEM)[None] * (MEM_DIM ** -0.5),
        "w_k": w_k.reshape(KV_LORA, QK_W).astype(BF16), "p_kpe": jnp.asarray(pk, BF16), "g_k": g_k[None],
        "w_v": wkv[..., NOPE_DIM:].reshape(KV_LORA, MLA_W).astype(BF16),
        "w_v_t": wkv[..., NOPE_DIM:].reshape(KV_LORA, MLA_W).T.astype(BF16),
        "g_mem": g_mem[l][None], "w_mem": w_mem.astype(BF16), "g_mem_k": jnp.tile(g_mem_k[l], H_MEM)[None],
        "w_g": w[:, o_gate:].astype(BF16),
        "w_o_mla": w_o_mla[l].reshape(H_MLA, V_DIM, D_MODEL).astype(BF16),
        "w_o_sb": w_o_sb[l].reshape(H_SB, SB_DIM, D_MODEL).astype(BF16),
        "w_o_mem": w_o_mem[l].astype(BF16), "w_out": w_out[l].astype(BF16), "g_ffn": g_ffn[l][None],
        "w_r_hi": w_r_hi, "w_r_lo": (w_r - w_r_hi.astype(F32)).astype(BF16), "b_r": b_r[None],
        "w_gate_up": w_gate_up[l], "b_gate_up": b_gate_up[l][:, None, :], "w_down": w_down[l],
        "b_down": b_down[l][:, None, :],
    }


def kernel(x_prompt, x_sample, mem_prompt, cache_mla_latent, cache_mla_rope, cache_sb_k, cache_sb_v, cache_mem_k, cache_mem_v, g_attn, w_in, g_q_lat, w_q_b, g_q_nope, g_q_rope, g_kv_lat, w_kv_b, g_k_nope, g_k_rope, g_mem, w_mem_kv, g_mem_q, g_mem_k, w_o_mla, w_o_sb, w_o_mem, w_out, g_ffn, w_router, b_router, w_gate_up, b_gate_up, w_down, b_down):
    depth = g_attn.shape[0]
    bp, sp, _ = x_prompt.shape
    bs, ss, _ = x_sample.shape
    past = cache_mla_latent.shape[2]
    n_mem = mem_prompt.shape[1]
    assert bp == 1 and sp % TQ_MLA == 0 and sp % TM_KV == 0 and (bs * ss) % TM_PROJ == 0 and TM_PROJ % ss == 0
    rows_s = bs * ss
    c_sbk = cache_sb_k.reshape(depth, bs, past, SB_W)
    c_sbv = cache_sb_v.reshape(depth, bs, past, SB_W)
    c_mk = cache_mem_k.reshape(depth, bs, n_mem, MEM_W)
    c_mv = cache_mem_v.reshape(depth, bs, n_mem, MEM_W)
    xp = x_prompt.reshape(sp, D_MODEL)
    xs = x_sample.reshape(rows_s, D_MODEL)
    mem2d = mem_prompt.reshape(n_mem, D_MODEL)
    base_p, off_p = np.arange(sp // TM_PROJ) * TM_PROJ, np.arange(TM_PROJ)
    base_s, off_s = np.full((rows_s // TM_PROJ,), past), np.arange(TM_PROJ) % ss
    outs = [[] for _ in range(10)]
    for l in range(depth):
        p = _pack_layer(l, g_attn, w_in, g_q_lat, w_q_b, g_q_nope, g_q_rope, g_kv_lat, w_kv_b, g_k_nope, g_k_rope,
                        g_mem, w_mem_kv, g_mem_q, g_mem_k, w_o_mla, w_o_sb, w_o_mem, w_out, g_ffn, w_router,
                        b_router, w_gate_up, b_gate_up, w_down, b_down)
        q, lat, kpe, sq, sk, sv, skh, svh, mq = _proj(xp, base_p, off_p, p)
        kh, _, vth = _kv_expand(lat, kpe, p)
        o_mla = _mla_attn(q, kh, vth)
        o_sb = _sb_attn(sq, skh, svh)
        mk, mv = _mem_kv(mem2d, p)
        o_mem = _mem_attn(mq, mk, mv)
        x1p, idx_p, gate_p = _merge(xp, o_mla, o_sb, o_mem, p)
        q_s, lat_s, kpe_s, sq_s, sk_s, sv_s, skh_s, svh_s, mq_s = _proj(xs, base_s, off_s, p)
        kn, vn, _ = _kv_expand(lat_s, kpe_s, p)
        o_mla_s, o_sb_s, o_mem_s = _sample_attn(q_s, kn, vn, sq_s, skh_s, svh_s, mq_s, cache_mla_latent,
                                                cache_mla_rope, c_sbk, c_sbv, c_mk, c_mv, l, p)
        x1s, idx_s, gate_s = _merge(xs, o_mla_s, o_sb_s, o_mem_s, p)
        x1 = jnp.concatenate([x1p, x1s], axis=0)
        idx = jnp.concatenate([idx_p[:, :TOP_K], idx_s[:, :TOP_K]], axis=0)
        gate = jnp.concatenate([gate_p[:, :TOP_K], gate_s[:, :TOP_K]], axis=0)
        y = _moe(x1, idx, gate, p)
        xp, xs = y[:sp], y[sp:]
        for lst, val in zip(outs, (lat.reshape(bp, sp, KV_LORA), kpe.reshape(bp, sp, ROPE_DIM),
                                   sk.reshape(bp, sp, H_SB, SB_DIM), sv.reshape(bp, sp, H_SB, SB_DIM),
                                   mk.reshape(bp, n_mem, H_MEM, MEM_DIM), mv.reshape(bp, n_mem, H_MEM, MEM_DIM),
                                   lat_s.reshape(bs, ss, KV_LORA), kpe_s.reshape(bs, ss, ROPE_DIM),
                                   sk_s.reshape(bs, ss, H_SB, SB_DIM), sv_s.reshape(bs, ss, H_SB, SB_DIM))):
            lst.append(val)
    return (xp.reshape(bp, sp, D_MODEL), xs.reshape(bs, ss, D_MODEL)) + tuple(jnp.stack(o) for o in outs)
```

```python
import functools

import numpy as np
import jax
import jax.numpy as jnp
from jax import lax
from jax.experimental import pallas as pl
from jax.experimental.pallas import tpu as pltpu

F32 = jnp.float32
BF16 = jnp.bfloat16

D_MODEL = 1024
CHUNK = 64
EPS = 1e-6
H_MLA = 8
NOPE_DIM = 64
ROPE_DIM = 32
ROPE_HALF = ROPE_DIM // 2
V_DIM = 64
Q_LORA = 384
KV_LORA = 256
ROPE_BASE = 10000.0
QK_DIM = NOPE_DIM + ROPE_DIM
QK_W = H_MLA * QK_DIM
H_SB = 8
SB_DIM = 64
H_MEM = 4
MEM_DIM = 128
N_EXPERTS = 32
TOP_K = 4
D_FF = 1024
SWIGLU_LIMIT = 7.0
SWIGLU_ALPHA = 1.702
MLA_W = H_MLA * V_DIM
SB_W = H_SB * SB_DIM
MEM_W = H_MEM * MEM_DIM

LANES = 128
VT_ROWS = LANES
MLA_CHAINS = 4
MLA_KSPLIT = 2
LOG2E = 1.4426950408889634
A_QLAT = 0
A_KVLAT = A_QLAT + Q_LORA
A_KPE = A_KVLAT + KV_LORA
A_KPE_SW = A_KPE + LANES
A_SBQ = A_KPE_SW + LANES
A_SBK = A_SBQ + SB_W
A_SBV = A_SBK + SB_W
A_MEMQ = A_SBV + SB_W
A_COLS = A_MEMQ + MEM_W
KPE_LANE = NOPE_DIM

TM_PROJ = 256
TM_KV = 512
TQ_MLA = 1024
TQ_SB = 256
SB_HEADS = 2
TM_MEM = 512
TM_MERGE = 256
BM_MOE = 256
TT_COMBINE = 256
DMA_UNROLL = 8
TOP_K_SHIFT = TOP_K.bit_length() - 1
assert 1 << TOP_K_SHIFT == TOP_K
VMEM_LIMIT = 56 * 1024 * 1024

SB_UNDERFLOW = -120.0
NEG_BIG = -3.0e38


def _cparams(*sem):
    return pltpu.CompilerParams(dimension_semantics=sem, vmem_limit_bytes=VMEM_LIMIT)


def _split(x):
    hi = x.astype(BF16)
    lo = (x - hi.astype(F32)).astype(BF16)
    return hi, lo


def _dot(a, b):
    return jnp.dot(a, b, preferred_element_type=F32)


def _dot2(x, m):
    hi, lo = _split(x)
    return _dot(hi, m) + _dot(lo, m)


def _dot_nt(a, b):
    return lax.dot_general(a, b, (((1,), (1,)), ((), ())), preferred_element_type=F32)


def _rms(x):
    return x * lax.rsqrt(jnp.mean(x * x, axis=-1, keepdims=True) + EPS)


def _softplus(z):
    return jnp.maximum(z, 0.0) + jnp.log1p(jnp.exp(-jnp.abs(z)))


def _full(shape):
    n = len(shape)
    return pl.BlockSpec(shape, lambda *_: (0,) * n)


def _proj_kernel(x_ref, gattn_ref, wa_ref, gqlat_ref, wq_ref, wqs_ref, ind_ref, indt_ref, gq_ref, gkv_ref,
                 bcs_ref, ocos_ref, osin_ref, gmq_ref, wsvt_ref,
                 q_ref, lat_ref, kpe_ref, sq_ref, sk_ref, sv_ref, skh_ref, svh_ref, mq_ref, svt_ref):
    h = (_rms(x_ref[...]) * gattn_ref[...]).astype(BF16)
    z = _dot(h, wa_ref[...])
    svt = _dot_nt(wsvt_ref[...], h)
    pad = jnp.zeros((VT_ROWS - SB_DIM, svt.shape[1]), F32)
    for hd in range(H_SB):
        svt_ref[hd] = jnp.concatenate([svt[hd * SB_DIM:(hd + 1) * SB_DIM, :], pad], axis=0).astype(BF16)
    bc = bcs_ref[0, 0:1, :]
    bs = bcs_ref[0, 1:2, :]
    oc = ocos_ref[...]
    osn = osin_ref[...]
    cos_f = bc * oc - bs * osn
    sin_f = bs * oc + bc * osn
    qn = (_rms(z[:, A_QLAT:A_QLAT + Q_LORA]) * gqlat_ref[...]).astype(BF16)
    qr = _dot(qn, wq_ref[...]) * cos_f + _dot(qn, wqs_ref[...]) * sin_f
    ssq = _dot2(qr * qr, ind_ref[...])
    inv = lax.rsqrt(ssq * (1.0 / QK_DIM) + EPS)
    qo = qr * _dot2(inv, indt_ref[...]) * gq_ref[...]
    for hd in range(H_MLA):
        q_ref[hd] = qo[:, hd * QK_DIM:(hd + 1) * QK_DIM].astype(BF16)
    lat_ref[...] = _rms(z[:, A_KVLAT:A_KVLAT + KV_LORA]) * gkv_ref[...]
    kr = z[:, A_KPE:A_KPE + LANES] * cos_f[:, :LANES] + z[:, A_KPE_SW:A_KPE_SW + LANES] * sin_f[:, :LANES]
    kpe_ref[...] = kr[:, KPE_LANE:KPE_LANE + ROPE_DIM]
    sbq = z[:, A_SBQ:A_SBQ + SB_W] * (SB_DIM ** -0.5)
    sbk = z[:, A_SBK:A_SBK + SB_W]
    sbv = z[:, A_SBV:A_SBV + SB_W]
    sk_ref[...] = sbk
    sv_ref[...] = sbv
    for hd in range(H_SB):
        sl = slice(hd * SB_DIM, (hd + 1) * SB_DIM)
        sq_ref[hd] = sbq[:, sl].astype(BF16)
        skh_ref[hd] = sbk[:, sl].astype(BF16)
        svh_ref[hd] = sbv[:, sl].astype(BF16)
    mqs = []
    for hd in range(H_MEM):
        mqs.append(_rms(z[:, A_MEMQ + hd * MEM_DIM:A_MEMQ + (hd + 1) * MEM_DIM]))
    mq_ref[...] = (jnp.concatenate(mqs, axis=-1) * gmq_ref[...]).astype(BF16)


def _rope_tables(base_pos, off_pos):
    lane = np.arange(QK_W) % QK_DIM
    inv_freq = ROPE_BASE ** (-np.arange(ROPE_HALF, dtype=np.float64) / ROPE_HALF)
    freq = np.where(lane >= NOPE_DIM, inv_freq[(lane - NOPE_DIM) % ROPE_HALF], 0.0)
    ab = np.asarray(base_pos, np.float64)[:, None] * freq
    ao = np.asarray(off_pos, np.float64)[:, None] * freq
    bcs = np.stack([np.cos(ab), np.sin(ab)], axis=1).astype(np.float32)
    return jnp.asarray(bcs), jnp.asarray(np.cos(ao), F32), jnp.asarray(np.sin(ao), F32)


def _proj(x2d, base_pos, off_pos, p):
    rows = x2d.shape[0]
    tm = TM_PROJ
    nt = rows // tm
    bcs, ocos, osin = _rope_tables(base_pos, off_pos)
    row = lambda w: pl.BlockSpec((tm, w), lambda i: (i, 0))
    hm = lambda d: pl.BlockSpec((H_MLA, tm, d), lambda i: (0, i, 0))
    out_shape = (
        jax.ShapeDtypeStruct((H_MLA, rows, QK_DIM), BF16),
        jax.ShapeDtypeStruct((rows, KV_LORA), F32),
        jax.ShapeDtypeStruct((rows, ROPE_DIM), F32),
        jax.ShapeDtypeStruct((H_SB, rows, SB_DIM), BF16),
        jax.ShapeDtypeStruct((rows, SB_W), F32),
        jax.ShapeDtypeStruct((rows, SB_W), F32),
        jax.ShapeDtypeStruct((H_SB, rows, SB_DIM), BF16),
        jax.ShapeDtypeStruct((H_SB, rows, SB_DIM), BF16),
        jax.ShapeDtypeStruct((rows, MEM_W), BF16),
        jax.ShapeDtypeStruct((H_SB, VT_ROWS, rows), BF16),
    )
    return pl.pallas_call(
        _proj_kernel,
        grid=(nt,),
        in_specs=[row(D_MODEL), _full((1, D_MODEL)), _full((D_MODEL, A_COLS)), _full((1, Q_LORA)),
                  _full((Q_LORA, QK_W)), _full((Q_LORA, QK_W)), _full((QK_W, LANES)), _full((LANES, QK_W)),
                  _full((1, QK_W)), _full((1, KV_LORA)),
                  pl.BlockSpec((1, 2, QK_W), lambda i: (i, 0, 0)), _full((tm, QK_W)), _full((tm, QK_W)),
                  _full((1, MEM_W)), _full((SB_W, D_MODEL))],
        out_specs=(hm(QK_DIM), row(KV_LORA), row(ROPE_DIM), hm(SB_DIM), row(SB_W), row(SB_W), hm(SB_DIM),
                   hm(SB_DIM), row(MEM_W), pl.BlockSpec((H_SB, VT_ROWS, tm), lambda i: (0, 0, i))),
        out_shape=out_shape,
        compiler_params=_cparams("parallel"),
        name="proj",
    )(x2d, p["g_attn"], p["w_a"], p["g_q_lat"], p["w_q"], p["w_q_sw"], p["ind96"], p["ind96_t"], p["g_q"],
      p["g_kv_lat"], bcs, ocos, osin, p["g_mem_q"], p["w_sbv_t"])


def _expand_keys(lat, kpe, wk, pk, ind, indt, gk):
    kf = _dot(lat.astype(BF16), wk) + _dot2(kpe, pk)
    inv = lax.rsqrt(_dot2(kf * kf, ind) * (1.0 / QK_DIM) + EPS)
    return kf * _dot2(inv, indt) * gk


def _kv_expand_kernel(lat_ref, kpe_ref, wk_ref, pk_ref, ind_ref, indt_ref, gk_ref, wv_ref, wvt_ref,
                      k_ref, v_ref, vt_ref):
    lat = lat_ref[...]
    latb = lat.astype(BF16)
    ko = _expand_keys(lat, kpe_ref[...], wk_ref[...], pk_ref[...], ind_ref[...], indt_ref[...], gk_ref[...])
    v = _dot(latb, wv_ref[...])
    vt = _dot_nt(wvt_ref[...], latb)
    ones_row = (lax.broadcasted_iota(jnp.int32, (VT_ROWS - V_DIM, vt.shape[1]), 0) == 0).astype(F32)
    for hd in range(H_MLA):
        k_ref[hd] = ko[:, hd * QK_DIM:(hd + 1) * QK_DIM].astype(BF16)
        v_ref[hd] = v[:, hd * V_DIM:(hd + 1) * V_DIM].astype(BF16)
        vt_ref[hd] = jnp.concatenate([vt[hd * V_DIM:(hd + 1) * V_DIM, :], ones_row], axis=0).astype(BF16)


def _kv_expand(lat, kpe, p):
    rows = lat.shape[0]
    tm = min(TM_KV, rows)
    row = lambda w: pl.BlockSpec((tm, w), lambda i: (i, 0))
    hm = lambda d: pl.BlockSpec((H_MLA, tm, d), lambda i: (0, i, 0))
    return pl.pallas_call(
        _kv_expand_kernel,
        grid=(rows // tm,),
        in_specs=[row(KV_LORA), row(ROPE_DIM), _full((KV_LORA, QK_W)), _full((ROPE_DIM, QK_W)),
                  _full((QK_W, LANES)), _full((LANES, QK_W)), _full((1, QK_W)), _full((KV_LORA, MLA_W)),
                  _full((MLA_W, KV_LORA))],
        out_specs=(hm(QK_DIM), hm(V_DIM), pl.BlockSpec((H_MLA, VT_ROWS, tm), lambda i: (0, 0, i))),
        out_shape=(jax.ShapeDtypeStruct((H_MLA, rows, QK_DIM), BF16),
                   jax.ShapeDtypeStruct((H_MLA, rows, V_DIM), BF16),
                   jax.ShapeDtypeStruct((H_MLA, VT_ROWS, rows), BF16)),
        compiler_params=_cparams("parallel"),
        name="kv_expand",
    )(lat, kpe, p["w_k"], p["p_kpe"], p["ind96"], p["ind96_t"], p["g_k"], p["w_v"], p["w_v_t"])


def _mla_update(st, vt, m, acc):
    m_new = jnp.maximum(m, jnp.max(st, axis=0, keepdims=True))
    pr = jnp.exp2(st - m_new).astype(BF16)
    return m_new, acc * jnp.exp2(m - m_new) + _dot(vt, pr)


def _mla_step(qc, k, vt, m, acc, mask):
    st = _dot_nt(k, qc)
    if mask is not None:
        st = jnp.where(mask, st, NEG_BIG)
    return _mla_update(st, vt, m, acc)


def _mla_attn_kernel(q_ref, k_ref, vt_ref, o_ref):
    i = pl.program_id(1)
    tq = q_ref.shape[1]
    tc = tq // MLA_CHAINS
    tk = tq // MLA_KSPLIT
    nsc = 2 * MLA_CHAINS
    qs = [q_ref[0, c * tc:(c + 1) * tc, :] for c in range(MLA_CHAINS)]

    def scores(t):
        k = k_ref[0, pl.ds(pl.multiple_of(t * tk, tk), tk), :]
        out = []
        for c in range(MLA_CHAINS):
            st = _dot_nt(k, qs[c])
            out += [st, jnp.max(st, axis=0, keepdims=True)]
        return out

    def update(t, sc, st):
        vt = vt_ref[0, :, pl.ds(pl.multiple_of(t * tk, tk), tk)]
        out = []
        for c in range(MLA_CHAINS):
            m, acc = st[2 * c], st[2 * c + 1]
            m_new = jnp.maximum(m, sc[2 * c + 1])
            pr = jnp.exp2(sc[2 * c] - m_new).astype(BF16)
            out += [m_new, acc * jnp.exp2(m - m_new) + _dot(vt, pr)]
        return out

    state = []
    for _ in range(MLA_CHAINS):
        state += [jnp.full((1, tc), NEG_BIG, F32), jnp.zeros((VT_ROWS, tc), F32)]

    def body(j, carry):
        sc, st = carry[:nsc], carry[nsc:]
        for u in range(MLA_KSPLIT):
            t = j * MLA_KSPLIT + u
            nxt = scores(t + 1)
            st = update(t, sc, st)
            sc = nxt
        return tuple(sc) + tuple(st)

    carry = lax.fori_loop(0, i, body, tuple(scores(0)) + tuple(state))
    sc, state = carry[:nsc], carry[nsc:]
    key_chunk = lax.broadcasted_iota(jnp.int32, (tk, tc), 0) // CHUNK
    qry_chunk = lax.broadcasted_iota(jnp.int32, (tk, tc), 1) // CHUNK
    for u in range(MLA_KSPLIT):
        t = i * MLA_KSPLIT + u
        nxt = scores(t + 1) if u + 1 < MLA_KSPLIT else None
        masked = []
        for c in range(MLA_CHAINS):
            st = jnp.where(key_chunk + u * (tk // CHUNK) <= qry_chunk + c * (tc // CHUNK), sc[2 * c], NEG_BIG)
            masked += [st, jnp.max(st, axis=0, keepdims=True)]
        state = update(t, masked, state)
        sc = nxt
    for c in range(MLA_CHAINS):
        ot = state[2 * c + 1].T
        o_ref[0, c * tc:(c + 1) * tc, :] = (ot[:, :V_DIM] / ot[:, V_DIM:V_DIM + 1]).astype(BF16)


def _mla_attn(q, k, vt):
    rows = q.shape[1]
    tq = min(TQ_MLA, rows)
    return pl.pallas_call(
        _mla_attn_kernel,
        grid=(H_MLA, rows // tq),
        in_specs=[pl.BlockSpec((1, tq, QK_DIM), lambda h, i: (h, i, 0)),
                  pl.BlockSpec((1, rows, QK_DIM), lambda h, i: (h, 0, 0)),
                  pl.BlockSpec((1, VT_ROWS, rows), lambda h, i: (h, 0, 0))],
        out_specs=pl.BlockSpec((1, tq, V_DIM), lambda h, i: (h, i, 0)),
        out_shape=jax.ShapeDtypeStruct((H_MLA, rows, V_DIM), BF16),
        compiler_params=_cparams("parallel", "arbitrary"),
        name="mla_attn",
    )(q, k, vt)


def _sb_attn_kernel(q_ref, k_ref, vt_ref, triu_ref, o_ref):
    i = pl.program_id(1)
    nh, tq = q_ref.shape[0], q_ref.shape[1]
    triu = triu_ref[...]
    qs = [q_ref[c] for c in range(nh)]
    before = lax.broadcasted_iota(jnp.int32, (tq, tq), 0) < lax.broadcasted_iota(jnp.int32, (tq, tq), 1)

    def process(tiles, cs, accs):
        starts = [pl.multiple_of(j * tq, tq) for j, _, _ in tiles]
        zs, lks, bts = {}, {}, {}
        for t in range(len(tiles)):
            for c in range(nh):
                zs[t, c] = _dot_nt(k_ref[c, pl.ds(starts[t], tq), :], qs[c])
        for t, (_, diag, live) in enumerate(tiles):
            for c in range(nh):
                lk = -_softplus(zs[t, c])
                if diag:
                    lk = jnp.where(before, lk, 0.0)
                if live is not None:
                    lk = jnp.where(live, lk, 0.0)
                lks[t, c] = lk
        for key, lk in lks.items():
            hi, lo = _split(lk)
            bts[key] = _dot(triu, hi) + _dot(triu, lo)
        cs, accs = list(cs), list(accs)
        for t, (_, diag, live) in enumerate(tiles):
            for c in range(nh):
                a = jnp.exp(zs[t, c] + lks[t, c] + bts[t, c] + cs[c])
                if diag:
                    a = jnp.where(before, a, 0.0)
                if live is not None:
                    a = jnp.where(live, a, 0.0)
                accs[c] = accs[c] + _dot(vt_ref[c, :, pl.ds(starts[t], tq)], a.astype(BF16))
                cs[c] = cs[c] + jnp.sum(lks[t, c], axis=0, keepdims=True)
        return cs, accs

    def cmax(cs):
        out = jnp.max(cs[0])
        for c in cs[1:]:
            out = jnp.maximum(out, jnp.max(c))
        return out

    cs = [jnp.zeros((1, tq), F32)] * nh
    accs = [jnp.zeros((VT_ROWS, tq), F32)] * nh
    cs, accs = process([(i, True, None), (jnp.maximum(i - 1, 0), False, i > 0)], cs, accs)

    def cond(carry):
        return jnp.logical_and(carry[0] >= 0, carry[1] > SB_UNDERFLOW)

    def body(carry):
        j = carry[0]
        cs, accs = process([(j, False, None)], carry[2:2 + nh], carry[2 + nh:])
        return (j - 1, cmax(cs)) + tuple(cs) + tuple(accs)

    carry = lax.while_loop(cond, body, (i - 2, cmax(cs)) + tuple(cs) + tuple(accs))
    for c in range(nh):
        o_ref[c] = carry[2 + nh + c].T[:, :SB_DIM].astype(BF16)


def _tri(n):
    return jnp.asarray(np.tril(np.ones((n, n), np.float32), -1), BF16)


def _sb_attn(q, k, vt):
    rows = q.shape[1]
    tq = min(TQ_SB, rows)
    nh = SB_HEADS
    return pl.pallas_call(
        _sb_attn_kernel,
        grid=(H_SB // nh, rows // tq),
        in_specs=[pl.BlockSpec((nh, tq, SB_DIM), lambda h, i: (h, i, 0)),
                  pl.BlockSpec((nh, rows, SB_DIM), lambda h, i: (h, 0, 0)),
                  pl.BlockSpec((nh, VT_ROWS, rows), lambda h, i: (h, 0, 0)),
                  _full((tq, tq))],
        out_specs=pl.BlockSpec((nh, tq, SB_DIM), lambda h, i: (h, i, 0)),
        out_shape=jax.ShapeDtypeStruct((H_SB, rows, SB_DIM), BF16),
        compiler_params=_cparams("parallel", "arbitrary"),
        name="sb_attn",
    )(q, k, vt, _tri(tq).T)


def _mem_kv_kernel(mem_ref, gmem_ref, w_ref, gk_ref, mk_ref, mv_ref):
    mn = (_rms(mem_ref[...]) * gmem_ref[...]).astype(BF16)
    kv = _dot(mn, w_ref[...])
    ks = [_rms(kv[:, hd * MEM_DIM:(hd + 1) * MEM_DIM]) for hd in range(H_MEM)]
    mk_ref[...] = jnp.concatenate(ks, axis=-1) * gk_ref[...]
    mv_ref[...] = kv[:, MEM_W:]


def _mem_kv(mem2d, p):
    n = mem2d.shape[0]
    return pl.pallas_call(
        _mem_kv_kernel,
        grid=(1,),
        in_specs=[_full((n, D_MODEL)), _full((1, D_MODEL)), _full((D_MODEL, 2 * MEM_W)), _full((1, MEM_W))],
        out_specs=(_full((n, MEM_W)), _full((n, MEM_W))),
        out_shape=(jax.ShapeDtypeStruct((n, MEM_W), F32), jax.ShapeDtypeStruct((n, MEM_W), F32)),
        compiler_params=_cparams("arbitrary"),
        name="mem_kv",
    )(mem2d, p["g_mem"], p["w_mem"], p["g_mem_k"])


def _mem_heads(mq, mk, mv):
    outs = []
    for hd in range(H_MEM):
        sl = slice(hd * MEM_DIM, (hd + 1) * MEM_DIM)
        s = _dot_nt(mq[:, sl], mk[:, sl])
        pr = jnp.exp(s - jnp.max(s, axis=-1, keepdims=True))
        o = _dot(pr.astype(BF16), mv[:, sl])
        outs.append(o / jnp.sum(pr, axis=-1, keepdims=True))
    return jnp.concatenate(outs, axis=-1)


def _mem_attn_kernel(mq_ref, mk_ref, mv_ref, o_ref):
    o_ref[...] = _mem_heads(mq_ref[...], mk_ref[...].astype(BF16), mv_ref[...].astype(BF16)).astype(BF16)


def _mem_attn(mq, mk, mv):
    rows = mq.shape[0]
    tm = min(TM_MEM, rows)
    n = mk.shape[0]
    return pl.pallas_call(
        _mem_attn_kernel,
        grid=(rows // tm,),
        in_specs=[pl.BlockSpec((tm, MEM_W), lambda i: (i, 0)), _full((n, MEM_W)), _full((n, MEM_W))],
        out_specs=pl.BlockSpec((tm, MEM_W), lambda i: (i, 0)),
        out_shape=jax.ShapeDtypeStruct((rows, MEM_W), BF16),
        compiler_params=_cparams("parallel"),
        name="mem_attn",
    )(mq, mk, mv)


def _sample_attn_kernel(past, q_ref, kn_ref, vn_ref, sq_ref, skn_ref, svn_ref, mq_ref,
                        clat_ref, crope_ref, csk_ref, csv_ref, cmk_ref, cmv_ref,
                        wkt_ref, wkn_ref, ind_ref, ones_ref, erow_ref, gkn_ref, gkr_ref, wvh_ref, rep_ref,
                        tri_ref, tris_ref, omla_ref, osb_ref, omem_ref):
    ds = q_ref.shape[1]
    hs = H_MLA * ds
    ds_shift = ds.bit_length() - 1
    tc = tri_ref.shape[0]
    row = lax.broadcasted_iota(jnp.int32, (hs, 1), 0)
    qpos = past + (row & (ds - 1))
    rows_of = lambda x, hd: x[hd * ds:(hd + 1) * ds]

    latb = clat_ref[0, 0].astype(BF16)
    rope = crope_ref[0, 0]
    ropeb = rope.astype(BF16)
    kn = _dot(latb, wkn_ref[...])
    ssq = _dot((kn * kn).astype(BF16), ind_ref[...]) + _dot((rope * rope).astype(BF16), ones_ref[...])
    inv_hi, inv_lo = _split(lax.rsqrt(ssq * (1.0 / QK_DIM) + EPS))
    inv_rows = _dot_nt(erow_ref[...], inv_hi) + _dot_nt(erow_ref[...], inv_lo)
    qabs, qrope, s2 = [], [], []
    for hd in range(H_MLA):
        qh = q_ref[hd].astype(F32)
        qabs.append(_dot((qh[:, :NOPE_DIM] * gkn_ref[...]).astype(BF16), wkt_ref[hd]))
        qrope.append((qh[:, NOPE_DIM:] * gkr_ref[...]).astype(BF16))
        s2.append(_dot_nt(q_ref[hd], kn_ref[hd]))
    s1 = _dot_nt(jnp.concatenate(qabs, axis=0).astype(BF16), latb) + _dot_nt(jnp.concatenate(qrope, axis=0), ropeb)
    s1 = s1 * inv_rows
    s2 = jnp.concatenate(s2, axis=0)
    q_chunk = qpos // CHUNK
    s1 = jnp.where((lax.broadcasted_iota(jnp.int32, s1.shape, 1) // CHUNK) <= q_chunk, s1, NEG_BIG)
    s2 = jnp.where(((past + lax.broadcasted_iota(jnp.int32, s2.shape, 1)) // CHUNK) <= q_chunk, s2, NEG_BIG)
    m = jnp.maximum(jnp.max(s1, axis=-1, keepdims=True), jnp.max(s2, axis=-1, keepdims=True))
    p1 = jnp.exp2(s1 - m)
    p2 = jnp.exp2(s2 - m)
    den = jnp.sum(p1, axis=-1, keepdims=True) + jnp.sum(p2, axis=-1, keepdims=True)
    olat = _dot(p1.astype(BF16), latb)
    for hd in range(H_MLA):
        o = _dot(rows_of(olat, hd).astype(BF16), wvh_ref[hd]) + _dot(rows_of(p2, hd).astype(BF16), vn_ref[hd])
        omla_ref[hd] = (o / rows_of(den, hd)).astype(BF16)

    sq_all = jnp.concatenate([sq_ref[hd] for hd in range(H_SB)], axis=0)
    sqbd = _dot(sq_all, rep_ref[...])
    lane_head = lax.broadcasted_iota(jnp.int32, sqbd.shape, 1) // SB_DIM
    sqbd = jnp.where(lane_head == lax.shift_right_logical(row, ds_shift), sqbd, 0.0).astype(BF16)
    z1 = _dot_nt(sqbd, csk_ref[0, 0].astype(BF16))
    z2 = jnp.concatenate([_dot_nt(sq_ref[hd], skn_ref[hd]) for hd in range(H_SB)], axis=0)
    before_n = lax.broadcasted_iota(jnp.int32, z2.shape, 1) < (row & (ds - 1))
    l2 = jnp.where(before_n, -_softplus(z2), 0.0)
    a2 = jnp.where(before_n, jnp.exp(z2 + l2 + _dot2(l2, tris_ref[...])), 0.0)
    c = jnp.sum(l2, axis=-1, keepdims=True)
    l1 = -_softplus(z1)
    tri = tri_ref[...]
    a1 = [None] * (past // tc)
    for cb in reversed(range(past // tc)):
        cs = slice(cb * tc, (cb + 1) * tc)
        lc = l1[:, cs]
        a1[cb] = jnp.exp(z1[:, cs] + lc + _dot2(lc, tri) + c).astype(BF16)
        c = c + jnp.sum(lc, axis=-1, keepdims=True)
    osb = _dot(jnp.concatenate(a1, axis=1), csv_ref[0, 0].astype(BF16))
    for hd in range(H_SB):
        o = rows_of(osb, hd)[:, hd * SB_DIM:(hd + 1) * SB_DIM] + _dot(rows_of(a2, hd).astype(BF16), svn_ref[hd])
        osb_ref[hd] = o.astype(BF16)

    omem_ref[...] = _mem_heads(mq_ref[...], cmk_ref[0, 0].astype(BF16), cmv_ref[0, 0].astype(BF16)).astype(BF16)


def _sample_attn(q, kn, vn, sq, skn, svn, mq, c_lat, c_rope, c_sbk, c_sbv, c_mk, c_mv, layer, p):
    nb, past = c_lat.shape[1], c_lat.shape[2]
    ds = q.shape[1] // nb
    assert ds & (ds - 1) == 0, "row -> query index uses a bit mask"
    n_mem = c_mk.shape[2]
    tc = min(256, past)
    hs = H_MLA * ds
    erow = np.zeros((hs, LANES), np.float32)
    erow[np.arange(hs), np.arange(hs) // ds] = 1.0
    rep = np.tile(np.eye(SB_DIM, dtype=np.float32), (1, H_SB))
    hm = lambda d: pl.BlockSpec((H_MLA, ds, d), lambda b: (0, b, 0))
    cache = lambda n, w: pl.BlockSpec((1, 1, n, w), lambda b: (layer, b, 0, 0))
    rows = q.shape[1]
    return pl.pallas_call(
        functools.partial(_sample_attn_kernel, past),
        grid=(nb,),
        in_specs=[hm(QK_DIM), hm(QK_DIM), hm(V_DIM), hm(SB_DIM), hm(SB_DIM), hm(SB_DIM),
                  pl.BlockSpec((ds, MEM_W), lambda b: (b, 0)),
                  cache(past, KV_LORA), cache(past, ROPE_DIM), cache(past, SB_W), cache(past, SB_W),
                  cache(n_mem, MEM_W), cache(n_mem, MEM_W),
                  _full((H_MLA, NOPE_DIM, KV_LORA)), _full((KV_LORA, H_MLA * NOPE_DIM)),
                  _full((H_MLA * NOPE_DIM, LANES)), _full((ROPE_DIM, LANES)), _full((hs, LANES)),
                  _full((1, NOPE_DIM)), _full((1, ROPE_DIM)), _full((H_MLA, KV_LORA, V_DIM)),
                  _full((SB_DIM, SB_W)), _full((tc, tc)), _full((ds, ds))],
        out_specs=(hm(V_DIM), hm(SB_DIM), pl.BlockSpec((ds, MEM_W), lambda b: (b, 0))),
        out_shape=(jax.ShapeDtypeStruct((H_MLA, rows, V_DIM), BF16),
                   jax.ShapeDtypeStruct((H_SB, rows, SB_DIM), BF16),
                   jax.ShapeDtypeStruct((rows, MEM_W), BF16)),
        compiler_params=_cparams("parallel"),
        name="sample_attn",
    )(q, kn, vn, sq, skn, svn, mq, c_lat, c_rope, c_sbk, c_sbv, c_mk, c_mv,
      p["w_k_t"], p["w_k_nope"], p["ind64"], p["ones_rope"], jnp.asarray(erow, BF16), p["g_k_nope"], p["g_k_rope"],
      p["w_v_h"], jnp.asarray(rep, BF16), _tri(tc), _tri(ds))


def _merge_kernel(x_ref, gattn_ref, wg_ref, omla_ref, osb_ref, omem_ref, woa_ref, wob_ref, wom_ref, wout_ref,
                  gffn_ref, wrh_ref, wrl_ref, br_ref, x1_ref, idx_ref, gate_ref):
    x = x_ref[...]
    h = (_rms(x) * gattn_ref[...]).astype(BF16)
    g = 1.0 / (1.0 + jnp.exp(-_dot(h, wg_ref[...])))
    ua = _dot(omla_ref[0], woa_ref[0])
    ub = _dot(osb_ref[0], wob_ref[0])
    for hd in range(1, H_MLA):
        ua = ua + _dot(omla_ref[hd], woa_ref[hd])
        ub = ub + _dot(osb_ref[hd], wob_ref[hd])
    um = _dot(omem_ref[...], wom_ref[...])
    u = g[:, :D_MODEL] * ua + g[:, D_MODEL:2 * D_MODEL] * ub + g[:, 2 * D_MODEL:] * um
    x1 = x + _dot(u.astype(BF16), wout_ref[...])
    x1_ref[...] = x1
    xh, xl = _split(_rms(x1) * gffn_ref[...])
    lg = _dot(xh, wrh_ref[...]) + _dot(xh, wrl_ref[...]) + _dot(xl, wrh_ref[...]) + br_ref[...]
    lane = lax.broadcasted_iota(jnp.int32, lg.shape, 1).astype(F32)
    vals, ids = [], []
    for _ in range(TOP_K):
        m = jnp.max(lg, axis=-1, keepdims=True)
        sel = jnp.min(jnp.where(lg == m, lane, float(LANES)), axis=-1, keepdims=True)
        vals.append(m)
        ids.append(sel)
        lg = jnp.where(lane == sel, NEG_BIG, lg)
    es = [jnp.exp(v - vals[0]) for v in vals]
    den = es[0] + es[1] + es[2] + es[3]
    idx_o = jnp.zeros(lg.shape, F32)
    gate_o = jnp.zeros(lg.shape, F32)
    for k in range(TOP_K):
        idx_o = jnp.where(lane == float(k), ids[k], idx_o)
        gate_o = jnp.where(lane == float(k), es[k] / den, gate_o)
    idx_ref[...] = idx_o.astype(jnp.int32)
    gate_ref[...] = gate_o


def _merge(x2d, omla, osb, omem, p):
    rows = x2d.shape[0]
    tm = TM_MERGE
    row = lambda w: pl.BlockSpec((tm, w), lambda i: (i, 0))
    hm = lambda d: pl.BlockSpec((H_MLA, tm, d), lambda i: (0, i, 0))
    return pl.pallas_call(
        _merge_kernel,
        grid=(rows // tm,),
        in_specs=[row(D_MODEL), _full((1, D_MODEL)), _full((D_MODEL, 3 * D_MODEL)), hm(V_DIM), hm(SB_DIM),
                  row(MEM_W), _full((H_MLA, V_DIM, D_MODEL)), _full((H_SB, SB_DIM, D_MODEL)),
                  _full((MEM_W, D_MODEL)), _full((D_MODEL, D_MODEL)), _full((1, D_MODEL)),
                  _full((D_MODEL, LANES)), _full((D_MODEL, LANES)), _full((1, LANES))],
        out_specs=(row(D_MODEL), row(LANES), row(LANES)),
        out_shape=(jax.ShapeDtypeStruct((rows, D_MODEL), F32), jax.ShapeDtypeStruct((rows, LANES), jnp.int32),
                   jax.ShapeDtypeStruct((rows, LANES), F32)),
        compiler_params=_cparams("parallel"),
        name="merge",
    )(x2d, p["g_attn"], p["w_g"], omla, osb, omem, p["w_o_mla"], p["w_o_sb"], p["w_o_mem"], p["w_out"],
      p["g_ffn"], p["w_r_hi"], p["w_r_lo"], p["b_r"])


def _row_dma(src, src_row, dst, dst_row, sem):
    return pltpu.make_async_copy(src.at[pl.ds(src_row, 1), :], dst.at[pl.ds(dst_row, 1), :], sem)


def _row_dma_wait(src, dst, sem, n):
    def body(r, carry):
        _row_dma(src, 0, dst, 0, sem).wait()
        return carry

    lax.fori_loop(0, n, body, 0, unroll=DMA_UNROLL)


def _dispatch_kernel(dest_ref, x_ref, xs_in, xs_out, sem):
    del xs_in
    n = dest_ref.shape[2]
    for r in range(n):
        _row_dma(x_ref, r // TOP_K, xs_out, dest_ref[0, 0, r], sem.at[0]).start()
    _row_dma_wait(x_ref, xs_out, sem.at[0], n)


def _moe_dispatch(x1, dest, xs):
    rows = x1.shape[0]
    tt = TT_COMBINE
    nt = rows // tt
    return pl.pallas_call(
        _dispatch_kernel,
        grid=(nt,),
        in_specs=[pl.BlockSpec((1, 1, TOP_K * tt), lambda t: (t, 0, 0), memory_space=pltpu.SMEM),
                  pl.BlockSpec((tt, D_MODEL), lambda t: (t, 0)),
                  pl.BlockSpec(memory_space=pl.ANY)],
        out_specs=pl.BlockSpec(memory_space=pl.ANY),
        out_shape=jax.ShapeDtypeStruct(xs.shape, xs.dtype),
        input_output_aliases={2: 0},
        scratch_shapes=[pltpu.SemaphoreType.DMA((1,))],
        compiler_params=_cparams("arbitrary"),
        name="moe_dispatch",
    )(dest.reshape(nt, 1, TOP_K * tt), x1, xs)


def _moe_kernel(be_ref, nact_ref, xs_ref, gffn_ref, wgu_ref, bgu_ref, wd_ref, bd_ref, o_ref, wgu_sc, wd_sc):
    b = pl.program_id(0)

    @pl.when(b >= nact_ref[0])
    def _():
        o_ref[...] = jnp.zeros(o_ref.shape, o_ref.dtype)

    @pl.when(b < nact_ref[0])
    def _():
        changed = jnp.logical_or(b == 0, be_ref[b] != be_ref[jnp.maximum(b - 1, 0)])

        @pl.when(changed)
        def _():
            wgu_sc[...] = wgu_ref[0].astype(BF16)
            wd_sc[...] = wd_ref[0].astype(BF16)

        xb = (_rms(xs_ref[...]) * gffn_ref[...]).astype(BF16)
        gu = _dot(xb, wgu_sc[...]) + bgu_ref[0]
        g = jnp.minimum(gu[:, :D_FF], SWIGLU_LIMIT)
        u = jnp.clip(gu[:, D_FF:], -SWIGLU_LIMIT, SWIGLU_LIMIT)
        hid = (u + 1.0) * (g / (1.0 + jnp.exp(-SWIGLU_ALPHA * g)))
        o_ref[...] = _dot(hid.astype(BF16), wd_sc[...]) + bd_ref[0]


def _moe_experts(xs, block_expert, n_active, p):
    n_blocks = block_expert.shape[0]
    bm = BM_MOE
    blk = lambda b, be, na: (jnp.minimum(b, na[0] - 1), 0)
    grid_spec = pltpu.PrefetchScalarGridSpec(
        num_scalar_prefetch=2,
        grid=(n_blocks,),
        in_specs=[
            pl.BlockSpec((bm, D_MODEL), blk),
            pl.BlockSpec((1, D_MODEL), lambda b, be, na: (0, 0)),
            pl.BlockSpec((1, D_MODEL, 2 * D_FF), lambda b, be, na: (be[b], 0, 0)),
            pl.BlockSpec((1, 1, 2 * D_FF), lambda b, be, na: (be[b], 0, 0)),
            pl.BlockSpec((1, D_FF, D_MODEL), lambda b, be, na: (be[b], 0, 0)),
            pl.BlockSpec((1, 1, D_MODEL), lambda b, be, na: (be[b], 0, 0)),
        ],
        out_specs=pl.BlockSpec((bm, D_MODEL), lambda b, be, na: (b, 0)),
        scratch_shapes=[pltpu.VMEM((D_MODEL, 2 * D_FF), BF16), pltpu.VMEM((D_FF, D_MODEL), BF16)],
    )
    return pl.pallas_call(
        _moe_kernel,
        grid_spec=grid_spec,
        out_shape=jax.ShapeDtypeStruct((n_blocks * bm, D_MODEL), F32),
        compiler_params=_cparams("arbitrary"),
        name="moe_experts",
    )(block_expert, n_active, xs, p["g_ffn"], p["w_gate_up"], p["b_gate_up"], p["w_down"], p["b_down"])


def _combine_gather(y_hbm, pos_ref, dst, sem, n):
    for r in range(n):
        _row_dma(y_hbm, pos_ref[0, 0, r], dst, r, sem).start()


def _combine_kernel(nta, pos_ref, posn_ref, y_hbm, x1a_ref, ga_ref, x1b_ref, gb_ref, oa_ref, ob_ref, buf, sem):
    t = pl.program_id(0)
    nt = pl.num_programs(0)
    tt = oa_ref.shape[0]
    n = TOP_K * tt
    slot = t % 2

    @pl.when(t == 0)
    def _():
        _combine_gather(y_hbm, pos_ref, buf.at[0], sem.at[0], n)

    @pl.when(t + 1 < nt)
    def _():
        _combine_gather(y_hbm, posn_ref, buf.at[1 - slot], sem.at[1 - slot], n)

    _row_dma_wait(y_hbm, buf.at[slot], sem.at[slot], n)

    def combine(x1_ref, gate_ref, o_ref):
        acc = x1_ref[...]
        gate = gate_ref[...]
        for k in range(TOP_K):
            acc = acc + gate[:, k:k + 1] * buf[slot, k * tt:(k + 1) * tt, :]
        o_ref[...] = acc

    @pl.when(t < nta)
    def _():
        combine(x1a_ref, ga_ref, oa_ref)

    @pl.when(t >= nta)
    def _():
        combine(x1b_ref, gb_ref, ob_ref)


def _moe_combine(x1a, gate_a, x1b, gate_b, yb, pos):
    tt = TT_COMBINE
    nta, ntb = x1a.shape[0] // tt, x1b.shape[0] // tt
    nt = nta + ntb
    pos3 = pos.reshape(nt, tt, TOP_K).transpose(0, 2, 1).reshape(nt, 1, TOP_K * tt)
    ia = lambda t: (jnp.minimum(t, nta - 1), 0)
    ib = lambda t: (jnp.maximum(t - nta, 0), 0)
    return pl.pallas_call(
        functools.partial(_combine_kernel, nta),
        grid=(nt,),
        in_specs=[pl.BlockSpec((1, 1, TOP_K * tt), lambda t: (t, 0, 0), memory_space=pltpu.SMEM),
                  pl.BlockSpec((1, 1, TOP_K * tt), lambda t: (jnp.minimum(t + 1, nt - 1), 0, 0),
                               memory_space=pltpu.SMEM),
                  pl.BlockSpec(memory_space=pl.ANY),
                  pl.BlockSpec((tt, D_MODEL), ia), pl.BlockSpec((tt, LANES), ia),
                  pl.BlockSpec((tt, D_MODEL), ib), pl.BlockSpec((tt, LANES), ib)],
        out_specs=(pl.BlockSpec((tt, D_MODEL), ia), pl.BlockSpec((tt, D_MODEL), ib)),
        out_shape=(jax.ShapeDtypeStruct(x1a.shape, F32), jax.ShapeDtypeStruct(x1b.shape, F32)),
        scratch_shapes=[pltpu.VMEM((2, TOP_K * tt, D_MODEL), F32), pltpu.SemaphoreType.DMA((2,))],
        compiler_params=_cparams("arbitrary"),
        name="moe_combine",
    )(pos3, pos3, yb, x1a, gate_a, x1b, gate_b)


def _moe_route(idx):
    rows = idx.shape[0]
    n = rows * TOP_K
    bm = BM_MOE
    e = idx.reshape(n)
    onehot = (e[:, None] == jnp.arange(N_EXPERTS, dtype=jnp.int32)[None, :]).astype(jnp.int32)
    csum = jnp.cumsum(onehot, axis=0)
    rank = jnp.sum(csum * onehot, axis=1) - 1
    counts = csum[-1]
    padded = (counts + bm - 1) // bm * bm
    pend = jnp.cumsum(padded)
    dest = ((pend - padded)[e] + rank).astype(jnp.int32).reshape(rows, TOP_K)
    n_blocks = -(-n // bm) + N_EXPERTS
    starts = jnp.arange(n_blocks, dtype=jnp.int32) * bm
    block_expert = jnp.minimum(
        jnp.sum((pend[None, :] <= starts[:, None]).astype(jnp.int32), axis=1), N_EXPERTS - 1).astype(jnp.int32)
    n_active = (pend[-1:] // bm).astype(jnp.int32)
    return dest, block_expert, n_active


def _pack_layer(l, g_attn, w_in, g_q_lat, w_q_b, g_q_nope, g_q_rope, g_kv_lat, w_kv_b, g_k_nope, g_k_rope, g_mem,
                w_mem_kv, g_mem_q, g_mem_k, w_o_mla, w_o_sb, w_o_mem, w_out, g_ffn, w_router, b_router,
                w_gate_up, b_gate_up, w_down, b_down):
    w = w_in[l]
    off = np.cumsum((Q_LORA, KV_LORA, ROPE_DIM, SB_W, SB_W, SB_W, MEM_W))
    o_kpe, o_sbq = int(off[1]), int(off[2])
    o_gate = int(off[6])
    kpe1 = w[:, o_kpe:o_kpe + ROPE_HALF]
    kpe2 = w[:, o_kpe + ROPE_HALF:o_kpe + ROPE_DIM]
    zl = jnp.zeros((D_MODEL, KPE_LANE), F32)
    zr = jnp.zeros((D_MODEL, LANES - KPE_LANE - ROPE_DIM), F32)
    w_a = jnp.concatenate([w[:, :o_kpe], zl, kpe1, kpe2, zr, zl, -kpe2, kpe1, zr, w[:, o_sbq:o_gate]], axis=1)
    wq = w_q_b[l]
    z_n = jnp.zeros((Q_LORA, H_MLA, NOPE_DIM), F32)
    wq_sw = jnp.concatenate([z_n, -wq[..., NOPE_DIM + ROPE_HALF:], wq[..., NOPE_DIM:NOPE_DIM + ROPE_HALF]], axis=-1)
    lane = np.arange(QK_W)
    ind = np.zeros((QK_W, LANES), np.float32)
    ind[lane, lane // QK_DIM] = 1.0
    ind64 = np.zeros((H_MLA * NOPE_DIM, LANES), np.float32)
    ind64[np.arange(H_MLA * NOPE_DIM), np.arange(H_MLA * NOPE_DIM) // NOPE_DIM] = 1.0
    pk = np.zeros((ROPE_DIM, QK_W), np.float32)
    for hd in range(H_MLA):
        pk[np.arange(ROPE_DIM), hd * QK_DIM + NOPE_DIM + np.arange(ROPE_DIM)] = 1.0
    wkv = w_kv_b[l]
    w_k = jnp.concatenate([wkv[..., :NOPE_DIM], jnp.zeros((KV_LORA, H_MLA, ROPE_DIM), F32)], axis=-1)
    g_q = jnp.tile(jnp.concatenate([g_q_nope[l], g_q_rope[l], g_q_rope[l]]), H_MLA) * (QK_DIM ** -0.5 * LOG2E)
    g_k = jnp.tile(jnp.concatenate([g_k_nope[l], g_k_rope[l], g_k_rope[l]]), H_MLA)
    wm = w_mem_kv[l]
    w_mem = jnp.concatenate([wm[..., :MEM_DIM].reshape(D_MODEL, MEM_W), wm[..., MEM_DIM:].reshape(D_MODEL, MEM_W)], 1)
    w_r = jnp.concatenate([w_router[l], jnp.zeros((D_MODEL, LANES - N_EXPERTS), F32)], axis=1)
    w_r_hi = w_r.astype(BF16)
    b_r = jnp.concatenate([b_router[l].astype(F32), jnp.full((LANES - N_EXPERTS,), NEG_BIG, F32)])
    return {
        "g_attn": g_attn[l][None], "w_a": w_a.astype(BF16), "g_q_lat": g_q_lat[l][None],
        "w_q": wq.reshape(Q_LORA, QK_W).astype(BF16), "w_q_sw": wq_sw.reshape(Q_LORA, QK_W).astype(BF16),
        "ind96": jnp.asarray(ind, BF16), "ind96_t": jnp.asarray(ind.T, BF16), "g_q": g_q[None],
        "g_kv_lat": g_kv_lat[l][None], "g_mem_q": jnp.tile(g_mem_q[l], H_MEM)[None] * (MEM_DIM ** -0.5),
        "w_k": w_k.reshape(KV_LORA, QK_W).astype(BF16), "p_kpe": jnp.asarray(pk, BF16), "g_k": g_k[None],
        "w_v": wkv[..., NOPE_DIM:].reshape(KV_LORA, MLA_W).astype(BF16),
        "w_v_t": wkv[..., NOPE_DIM:].reshape(KV_LORA, MLA_W).T.astype(BF16),
        "w_v_h": wkv[..., NOPE_DIM:].transpose(1, 0, 2).astype(BF16),
        "w_k_t": wkv[..., :NOPE_DIM].transpose(1, 2, 0).astype(BF16),
        "w_k_nope": wkv[..., :NOPE_DIM].reshape(KV_LORA, H_MLA * NOPE_DIM).astype(BF16),
        "ind64": jnp.asarray(ind64, BF16), "ones_rope": jnp.ones((ROPE_DIM, LANES), BF16),
        "g_k_nope": g_k_nope[l][None], "g_k_rope": jnp.concatenate([g_k_rope[l], g_k_rope[l]])[None],
        "g_mem": g_mem[l][None], "w_mem": w_mem.astype(BF16), "g_mem_k": jnp.tile(g_mem_k[l], H_MEM)[None],
        "w_g": w[:, o_gate:].astype(BF16), "w_sbv_t": w[:, o_gate - MEM_W - SB_W:o_gate - MEM_W].T.astype(BF16),
        "w_o_mla": w_o_mla[l].reshape(H_MLA, V_DIM, D_MODEL).astype(BF16),
        "w_o_sb": w_o_sb[l].reshape(H_SB, SB_DIM, D_MODEL).astype(BF16),
        "w_o_mem": w_o_mem[l].astype(BF16), "w_out": w_out[l].astype(BF16), "g_ffn": g_ffn[l][None],
        "w_r_hi": w_r_hi, "w_r_lo": (w_r - w_r_hi.astype(F32)).astype(BF16), "b_r": b_r[None],
        "w_gate_up": w_gate_up[l], "b_gate_up": b_gate_up[l][:, None, :], "w_down": w_down[l],
        "b_down": b_down[l][:, None, :],
    }


def kernel(x_prompt, x_sample, mem_prompt, cache_mla_latent, cache_mla_rope, cache_sb_k, cache_sb_v, cache_mem_k, cache_mem_v, g_attn, w_in, g_q_lat, w_q_b, g_q_nope, g_q_rope, g_kv_lat, w_kv_b, g_k_nope, g_k_rope, g_mem, w_mem_kv, g_mem_q, g_mem_k, w_o_mla, w_o_sb, w_o_mem, w_out, g_ffn, w_router, b_router, w_gate_up, b_gate_up, w_down, b_down):
    depth = g_attn.shape[0]
    bp, sp, _ = x_prompt.shape
    bs, ss, _ = x_sample.shape
    past = cache_mla_latent.shape[2]
    n_mem = mem_prompt.shape[1]
    assert bp == 1 and sp % TQ_MLA == 0 and sp % TM_KV == 0 and (bs * ss) % TM_PROJ == 0 and TM_PROJ % ss == 0
    rows_s = bs * ss
    c_sbk = cache_sb_k.reshape(depth, bs, past, SB_W)
    c_sbv = cache_sb_v.reshape(depth, bs, past, SB_W)
    c_mk = cache_mem_k.reshape(depth, bs, n_mem, MEM_W)
    c_mv = cache_mem_v.reshape(depth, bs, n_mem, MEM_W)
    xp = x_prompt.reshape(sp, D_MODEL)
    xs = x_sample.reshape(rows_s, D_MODEL)
    mem2d = mem_prompt.reshape(n_mem, D_MODEL)
    base_p, off_p = np.arange(sp // TM_PROJ) * TM_PROJ, np.arange(TM_PROJ)
    base_s, off_s = np.full((rows_s // TM_PROJ,), past), np.arange(TM_PROJ) % ss
    outs = [[] for _ in range(10)]
    for l in range(depth):
        p = _pack_layer(l, g_attn, w_in, g_q_lat, w_q_b, g_q_nope, g_q_rope, g_kv_lat, w_kv_b, g_k_nope, g_k_rope,
                        g_mem, w_mem_kv, g_mem_q, g_mem_k, w_o_mla, w_o_sb, w_o_mem, w_out, g_ffn, w_router,
                        b_router, w_gate_up, b_gate_up, w_down, b_down)
        q, lat, kpe, sq, sk, sv, skh, _, mq, svt = _proj(xp, base_p, off_p, p)
        kh, _, vth = _kv_expand(lat, kpe, p)
        o_mla = _mla_attn(q, kh, vth)
        o_sb = _sb_attn(sq, skh, svt)
        mk, mv = _mem_kv(mem2d, p)
        o_mem = _mem_attn(mq, mk, mv)
        x1p, idx_p, gate_p = _merge(xp, o_mla, o_sb, o_mem, p)
        q_s, lat_s, kpe_s, sq_s, sk_s, sv_s, skh_s, svh_s, mq_s, _ = _proj(xs, base_s, off_s, p)
        kn, vn, _ = _kv_expand(lat_s, kpe_s, p)
        o_mla_s, o_sb_s, o_mem_s = _sample_attn(q_s, kn, vn, sq_s, skh_s, svh_s, mq_s, cache_mla_latent,
                                                cache_mla_rope, c_sbk, c_sbv, c_mk, c_mv, l, p)
        x1s, idx_s, gate_s = _merge(xs, o_mla_s, o_sb_s, o_mem_s, p)
        dest, block_expert, n_active = _moe_route(jnp.concatenate([idx_p[:, :TOP_K], idx_s[:, :TOP_K]], axis=0))
        slots = jnp.zeros((block_expert.shape[0] * BM_MOE, D_MODEL), F32)
        slots = _moe_dispatch(x1s, dest[sp:], _moe_dispatch(x1p, dest[:sp], slots))
        yb = _moe_experts(slots, block_expert, n_active, p)
        xp, xs = _moe_combine(x1p, gate_p, x1s, gate_s, yb, dest)
        for lst, val in zip(outs, (lat.reshape(bp, sp, KV_LORA), kpe.reshape(bp, sp, ROPE_DIM),
                                   sk.reshape(bp, sp, H_SB, SB_DIM), sv.reshape(bp, sp, H_SB, SB_DIM),
                                   mk.reshape(bp, n_mem, H_MEM, MEM_DIM), mv.reshape(bp, n_mem, H_MEM, MEM_DIM),
                                   lat_s.reshape(bs, ss, KV_LORA), kpe_s.reshape(bs, ss, ROPE_DIM),
                                   sk_s.reshape(bs, ss, H_SB, SB_DIM), sv_s.reshape(bs, ss, H_SB, SB_DIM))):
            lst.append(val)
    return (xp.reshape(bp, sp, D_MODEL), xs.reshape(bs, ss, D_MODEL)) + tuple(jnp.stack(o) for o in outs)
```

```python
import functools

import numpy as np
import jax
import jax.numpy as jnp
from jax import lax
from jax.experimental import pallas as pl
from jax.experimental.pallas import tpu as pltpu

F32 = jnp.float32
BF16 = jnp.bfloat16

D_MODEL = 1024
CHUNK = 64
EPS = 1e-6
H_MLA = 8
NOPE_DIM = 64
ROPE_DIM = 32
ROPE_HALF = ROPE_DIM // 2
V_DIM = 64
Q_LORA = 384
KV_LORA = 256
ROPE_BASE = 10000.0
QK_DIM = NOPE_DIM + ROPE_DIM
QK_W = H_MLA * QK_DIM
H_SB = 8
SB_DIM = 64
H_MEM = 4
MEM_DIM = 128
N_EXPERTS = 32
TOP_K = 4
D_FF = 1024
SWIGLU_LIMIT = 7.0
SWIGLU_ALPHA = 1.702
MLA_W = H_MLA * V_DIM
SB_W = H_SB * SB_DIM
MEM_W = H_MEM * MEM_DIM

LANES = 128
VT_ROWS = LANES
MLA_CHAINS = 4
MLA_KSPLIT = 2
assert MLA_KSPLIT % 2 == 0
LOG2E = 1.4426950408889634
A_QLAT = 0
A_KVLAT = A_QLAT + Q_LORA
A_KPE = A_KVLAT + KV_LORA
A_KPE_SW = A_KPE + LANES
A_SBQ = A_KPE_SW + LANES
A_SBK = A_SBQ + SB_W
A_SBV = A_SBK + SB_W
A_MEMQ = A_SBV + SB_W
A_COLS = A_MEMQ + MEM_W
KPE_LANE = NOPE_DIM

TM_PROJ = 256
TM_KV = 512
TQ_MLA = 1024
TQ_SB = 256
SB_HEADS = 2
TM_MEM = 512
TM_MERGE = 256
BM_MOE = 256
TT_COMBINE = 256
DMA_UNROLL = 8
TOP_K_SHIFT = TOP_K.bit_length() - 1
assert 1 << TOP_K_SHIFT == TOP_K
VMEM_LIMIT = 56 * 1024 * 1024

SB_UNDERFLOW = -120.0
NEG_BIG = -3.0e38


def _cparams(*sem):
    return pltpu.CompilerParams(dimension_semantics=sem, vmem_limit_bytes=VMEM_LIMIT)


def _split(x):
    hi = x.astype(BF16)
    lo = (x - hi.astype(F32)).astype(BF16)
    return hi, lo


def _dot(a, b):
    return jnp.dot(a, b, preferred_element_type=F32)


def _dot2(x, m):
    hi, lo = _split(x)
    return _dot(hi, m) + _dot(lo, m)


def _dot_nt(a, b):
    return lax.dot_general(a, b, (((1,), (1,)), ((), ())), preferred_element_type=F32)


def _rms(x):
    return x * lax.rsqrt(jnp.mean(x * x, axis=-1, keepdims=True) + EPS)


def _softplus(z):
    return jnp.maximum(z, 0.0) + jnp.log1p(jnp.exp(-jnp.abs(z)))


def _full(shape):
    n = len(shape)
    return pl.BlockSpec(shape, lambda *_: (0,) * n)


def _proj_kernel(x_ref, gattn_ref, wa_ref, gqlat_ref, wq_ref, wqs_ref, ind_ref, indt_ref, gq_ref, gkv_ref,
                 bcs_ref, ocos_ref, osin_ref, gmq_ref, wsvt_ref,
                 q_ref, lat_ref, kpe_ref, sq_ref, sk_ref, sv_ref, skh_ref, svh_ref, mq_ref, svt_ref):
    h = (_rms(x_ref[...]) * gattn_ref[...]).astype(BF16)
    z = _dot(h, wa_ref[...])
    svt = _dot_nt(wsvt_ref[...], h)
    pad = jnp.zeros((VT_ROWS - SB_DIM, svt.shape[1]), F32)
    for hd in range(H_SB):
        svt_ref[hd] = jnp.concatenate([svt[hd * SB_DIM:(hd + 1) * SB_DIM, :], pad], axis=0).astype(BF16)
    bc = bcs_ref[0, 0:1, :]
    bs = bcs_ref[0, 1:2, :]
    oc = ocos_ref[...]
    osn = osin_ref[...]
    cos_f = bc * oc - bs * osn
    sin_f = bs * oc + bc * osn
    qn = (_rms(z[:, A_QLAT:A_QLAT + Q_LORA]) * gqlat_ref[...]).astype(BF16)
    qr = _dot(qn, wq_ref[...]) * cos_f + _dot(qn, wqs_ref[...]) * sin_f
    ssq = _dot2(qr * qr, ind_ref[...])
    inv = lax.rsqrt(ssq * (1.0 / QK_DIM) + EPS)
    qo = qr * _dot2(inv, indt_ref[...]) * gq_ref[...]
    for hd in range(H_MLA):
        q_ref[hd] = qo[:, hd * QK_DIM:(hd + 1) * QK_DIM].astype(BF16)
    lat_ref[...] = _rms(z[:, A_KVLAT:A_KVLAT + KV_LORA]) * gkv_ref[...]
    kr = z[:, A_KPE:A_KPE + LANES] * cos_f[:, :LANES] + z[:, A_KPE_SW:A_KPE_SW + LANES] * sin_f[:, :LANES]
    kpe_ref[...] = kr[:, KPE_LANE:KPE_LANE + ROPE_DIM]
    sbq = z[:, A_SBQ:A_SBQ + SB_W] * (SB_DIM ** -0.5)
    sbk = z[:, A_SBK:A_SBK + SB_W]
    sbv = z[:, A_SBV:A_SBV + SB_W]
    sk_ref[...] = sbk
    sv_ref[...] = sbv
    for hd in range(H_SB):
        sl = slice(hd * SB_DIM, (hd + 1) * SB_DIM)
        sq_ref[hd] = sbq[:, sl].astype(BF16)
        skh_ref[hd] = sbk[:, sl].astype(BF16)
        svh_ref[hd] = sbv[:, sl].astype(BF16)
    mqs = []
    for hd in range(H_MEM):
        mqs.append(_rms(z[:, A_MEMQ + hd * MEM_DIM:A_MEMQ + (hd + 1) * MEM_DIM]))
    mq_ref[...] = (jnp.concatenate(mqs, axis=-1) * gmq_ref[...]).astype(BF16)


def _rope_tables(base_pos, off_pos):
    lane = np.arange(QK_W) % QK_DIM
    inv_freq = ROPE_BASE ** (-np.arange(ROPE_HALF, dtype=np.float64) / ROPE_HALF)
    freq = np.where(lane >= NOPE_DIM, inv_freq[(lane - NOPE_DIM) % ROPE_HALF], 0.0)
    ab = np.asarray(base_pos, np.float64)[:, None] * freq
    ao = np.asarray(off_pos, np.float64)[:, None] * freq
    bcs = np.stack([np.cos(ab), np.sin(ab)], axis=1).astype(np.float32)
    return jnp.asarray(bcs), jnp.asarray(np.cos(ao), F32), jnp.asarray(np.sin(ao), F32)


def _proj(x2d, base_pos, off_pos, p):
    rows = x2d.shape[0]
    tm = TM_PROJ
    nt = rows // tm
    bcs, ocos, osin = _rope_tables(base_pos, off_pos)
    row = lambda w: pl.BlockSpec((tm, w), lambda i: (i, 0))
    hm = lambda d: pl.BlockSpec((H_MLA, tm, d), lambda i: (0, i, 0))
    out_shape = (
        jax.ShapeDtypeStruct((H_MLA, rows, QK_DIM), BF16),
        jax.ShapeDtypeStruct((rows, KV_LORA), F32),
        jax.ShapeDtypeStruct((rows, ROPE_DIM), F32),
        jax.ShapeDtypeStruct((H_SB, rows, SB_DIM), BF16),
        jax.ShapeDtypeStruct((rows, SB_W), F32),
        jax.ShapeDtypeStruct((rows, SB_W), F32),
        jax.ShapeDtypeStruct((H_SB, rows, SB_DIM), BF16),
        jax.ShapeDtypeStruct((H_SB, rows, SB_DIM), BF16),
        jax.ShapeDtypeStruct((rows, MEM_W), BF16),
        jax.ShapeDtypeStruct((H_SB, VT_ROWS, rows), BF16),
    )
    return pl.pallas_call(
        _proj_kernel,
        grid=(nt,),
        in_specs=[row(D_MODEL), _full((1, D_MODEL)), _full((D_MODEL, A_COLS)), _full((1, Q_LORA)),
                  _full((Q_LORA, QK_W)), _full((Q_LORA, QK_W)), _full((QK_W, LANES)), _full((LANES, QK_W)),
                  _full((1, QK_W)), _full((1, KV_LORA)),
                  pl.BlockSpec((1, 2, QK_W), lambda i: (i, 0, 0)), _full((tm, QK_W)), _full((tm, QK_W)),
                  _full((1, MEM_W)), _full((SB_W, D_MODEL))],
        out_specs=(hm(QK_DIM), row(KV_LORA), row(ROPE_DIM), hm(SB_DIM), row(SB_W), row(SB_W), hm(SB_DIM),
                   hm(SB_DIM), row(MEM_W), pl.BlockSpec((H_SB, VT_ROWS, tm), lambda i: (0, 0, i))),
        out_shape=out_shape,
        compiler_params=_cparams("parallel"),
        name="proj",
    )(x2d, p["g_attn"], p["w_a"], p["g_q_lat"], p["w_q"], p["w_q_sw"], p["ind96"], p["ind96_t"], p["g_q"],
      p["g_kv_lat"], bcs, ocos, osin, p["g_mem_q"], p["w_sbv_t"])


def _expand_keys(lat, kpe, wk, pk, ind, indt, gk):
    kf = _dot(lat.astype(BF16), wk) + _dot2(kpe, pk)
    inv = lax.rsqrt(_dot2(kf * kf, ind) * (1.0 / QK_DIM) + EPS)
    return kf * _dot2(inv, indt) * gk


def _kv_expand_kernel(lat_ref, kpe_ref, wk_ref, pk_ref, ind_ref, indt_ref, gk_ref, wv_ref, wvt_ref,
                      k_ref, v_ref, vt_ref):
    lat = lat_ref[...]
    latb = lat.astype(BF16)
    ko = _expand_keys(lat, kpe_ref[...], wk_ref[...], pk_ref[...], ind_ref[...], indt_ref[...], gk_ref[...])
    v = _dot(latb, wv_ref[...])
    vt = _dot_nt(wvt_ref[...], latb)
    ones_row = (lax.broadcasted_iota(jnp.int32, (VT_ROWS - V_DIM, vt.shape[1]), 0) == 0).astype(F32)
    for hd in range(H_MLA):
        k_ref[hd] = ko[:, hd * QK_DIM:(hd + 1) * QK_DIM].astype(BF16)
        v_ref[hd] = v[:, hd * V_DIM:(hd + 1) * V_DIM].astype(BF16)
        vt_ref[hd] = jnp.concatenate([vt[hd * V_DIM:(hd + 1) * V_DIM, :], ones_row], axis=0).astype(BF16)


def _kv_expand(lat, kpe, p):
    rows = lat.shape[0]
    tm = min(TM_KV, rows)
    row = lambda w: pl.BlockSpec((tm, w), lambda i: (i, 0))
    hm = lambda d: pl.BlockSpec((H_MLA, tm, d), lambda i: (0, i, 0))
    return pl.pallas_call(
        _kv_expand_kernel,
        grid=(rows // tm,),
        in_specs=[row(KV_LORA), row(ROPE_DIM), _full((KV_LORA, QK_W)), _full((ROPE_DIM, QK_W)),
                  _full((QK_W, LANES)), _full((LANES, QK_W)), _full((1, QK_W)), _full((KV_LORA, MLA_W)),
                  _full((MLA_W, KV_LORA))],
        out_specs=(hm(QK_DIM), hm(V_DIM), pl.BlockSpec((H_MLA, VT_ROWS, tm), lambda i: (0, 0, i))),
        out_shape=(jax.ShapeDtypeStruct((H_MLA, rows, QK_DIM), BF16),
                   jax.ShapeDtypeStruct((H_MLA, rows, V_DIM), BF16),
                   jax.ShapeDtypeStruct((H_MLA, VT_ROWS, rows), BF16)),
        compiler_params=_cparams("parallel"),
        name="kv_expand",
    )(lat, kpe, p["w_k"], p["p_kpe"], p["ind96"], p["ind96_t"], p["g_k"], p["w_v"], p["w_v_t"])


def _mla_update(st, vt, m, acc):
    m_new = jnp.maximum(m, jnp.max(st, axis=0, keepdims=True))
    pr = jnp.exp2(st - m_new).astype(BF16)
    return m_new, acc * jnp.exp2(m - m_new) + _dot(vt, pr)


def _mla_step(qc, k, vt, m, acc, mask):
    st = _dot_nt(k, qc)
    if mask is not None:
        st = jnp.where(mask, st, NEG_BIG)
    return _mla_update(st, vt, m, acc)


def _mla_attn_kernel(q_ref, k_ref, vt_ref, o_ref, s_sc, p_sc):
    i = pl.program_id(1)
    tq = q_ref.shape[1]
    nc = MLA_CHAINS
    tc = tq // nc
    tk = tq // MLA_KSPLIT
    qs = [q_ref[0, c * tc:(c + 1) * tc, :] for c in range(nc)]
    key_chunk = lax.broadcasted_iota(jnp.int32, (tk, tc), 0) // CHUNK
    qry_chunk = lax.broadcasted_iota(jnp.int32, (tk, tc), 1) // CHUNK

    def scores(t, slot):
        k = k_ref[0, pl.ds(pl.multiple_of(t * tk, tk), tk), :]
        out = []
        for c in range(nc):
            st = _dot_nt(k, qs[c])
            s_sc[slot, c] = st
            out.append(jnp.max(st, axis=0, keepdims=True))
        return out

    def accumulate(t, slot, alphas, accs):
        vt = vt_ref[0, :, pl.ds(pl.multiple_of(t * tk, tk), tk)]
        return [accs[c] * alphas[c] + _dot(vt, p_sc[slot, c]) for c in range(nc)]

    def step(t, slot, cmax, alphas, ms, accs, last, mask_u):
        nxt = None if last else scores(t + 1, 1 - slot)
        accs = accumulate(jnp.maximum(t - 1, 0), 1 - slot, alphas, accs)
        m_new, alphas = [], []
        for c in range(nc):
            st = s_sc[slot, c]
            if mask_u is None:
                cm = cmax[c]
            else:
                st = jnp.where(key_chunk + mask_u * (tk // CHUNK) <= qry_chunk + c * (tc // CHUNK), st, NEG_BIG)
                cm = jnp.max(st, axis=0, keepdims=True)
            mn = jnp.maximum(ms[c], cm)
            p_sc[slot, c] = jnp.exp2(st - mn).astype(BF16)
            alphas.append(jnp.exp2(ms[c] - mn))
            m_new.append(mn)
        return nxt, alphas, m_new, accs

    p_sc[1] = jnp.zeros(p_sc.shape[1:], BF16)
    init = (tuple(scores(0, 0)) + (jnp.ones((1, tc), F32),) * nc + (jnp.full((1, tc), NEG_BIG, F32),) * nc
            + (jnp.zeros((VT_ROWS, tc), F32),) * nc)
    unpack = lambda carry: [list(carry[g * nc:(g + 1) * nc]) for g in range(4)]

    def body(j, carry):
        cmax, alphas, ms, accs = unpack(carry)
        for u in range(MLA_KSPLIT):
            cmax, alphas, ms, accs = step(j * MLA_KSPLIT + u, u % 2, cmax, alphas, ms, accs, False, None)
        return tuple(cmax) + tuple(alphas) + tuple(ms) + tuple(accs)

    cmax, alphas, ms, accs = unpack(lax.fori_loop(0, i, body, init))
    for u in range(MLA_KSPLIT):
        cmax, alphas, ms, accs = step(i * MLA_KSPLIT + u, u % 2, cmax, alphas, ms, accs, u + 1 == MLA_KSPLIT, u)
    accs = accumulate((i + 1) * MLA_KSPLIT - 1, (MLA_KSPLIT - 1) % 2, alphas, accs)
    for c in range(nc):
        ot = accs[c].T
        o_ref[0, c * tc:(c + 1) * tc, :] = (ot[:, :V_DIM] / ot[:, V_DIM:V_DIM + 1]).astype(BF16)


def _mla_attn(q, k, vt):
    rows = q.shape[1]
    tq = min(TQ_MLA, rows)
    return pl.pallas_call(
        _mla_attn_kernel,
        grid=(H_MLA, rows // tq),
        in_specs=[pl.BlockSpec((1, tq, QK_DIM), lambda h, i: (h, i, 0)),
                  pl.BlockSpec((1, rows, QK_DIM), lambda h, i: (h, 0, 0)),
                  pl.BlockSpec((1, VT_ROWS, rows), lambda h, i: (h, 0, 0))],
        out_specs=pl.BlockSpec((1, tq, V_DIM), lambda h, i: (h, i, 0)),
        out_shape=jax.ShapeDtypeStruct((H_MLA, rows, V_DIM), BF16),
        scratch_shapes=[pltpu.VMEM((2, MLA_CHAINS, tq // MLA_KSPLIT, tq // MLA_CHAINS), F32),
                        pltpu.VMEM((2, MLA_CHAINS, tq // MLA_KSPLIT, tq // MLA_CHAINS), BF16)],
        compiler_params=_cparams("parallel", "arbitrary"),
        name="mla_attn",
    )(q, k, vt)


def _sb_attn_kernel(q_ref, k_ref, vt_ref, triu_ref, o_ref):
    i = pl.program_id(1)
    nh, tq = q_ref.shape[0], q_ref.shape[1]
    triu = triu_ref[...]
    qs = [q_ref[c] for c in range(nh)]
    before = lax.broadcasted_iota(jnp.int32, (tq, tq), 0) < lax.broadcasted_iota(jnp.int32, (tq, tq), 1)

    def process(tiles, cs, accs):
        starts = [pl.multiple_of(j * tq, tq) for j, _, _ in tiles]
        zs, lks, bts = {}, {}, {}
        for t in range(len(tiles)):
            for c in range(nh):
                zs[t, c] = _dot_nt(k_ref[c, pl.ds(starts[t], tq), :], qs[c])
        for t, (_, diag, live) in enumerate(tiles):
            for c in range(nh):
                lk = -_softplus(zs[t, c])
                if diag:
                    lk = jnp.where(before, lk, 0.0)
                if live is not None:
                    lk = jnp.where(live, lk, 0.0)
                lks[t, c] = lk
        for key, lk in lks.items():
            hi, lo = _split(lk)
            bts[key] = _dot(triu, hi) + _dot(triu, lo)
        cs, accs = list(cs), list(accs)
        for t, (_, diag, live) in enumerate(tiles):
            for c in range(nh):
                a = jnp.exp(zs[t, c] + lks[t, c] + bts[t, c] + cs[c])
                if diag:
                    a = jnp.where(before, a, 0.0)
                if live is not None:
                    a = jnp.where(live, a, 0.0)
                accs[c] = accs[c] + _dot(vt_ref[c, :, pl.ds(starts[t], tq)], a.astype(BF16))
                cs[c] = cs[c] + jnp.sum(lks[t, c], axis=0, keepdims=True)
        return cs, accs

    def cmax(cs):
        out = jnp.max(cs[0])
        for c in cs[1:]:
            out = jnp.maximum(out, jnp.max(c))
        return out

    cs = [jnp.zeros((1, tq), F32)] * nh
    accs = [jnp.zeros((VT_ROWS, tq), F32)] * nh
    cs, accs = process([(i, True, None), (jnp.maximum(i - 1, 0), False, i > 0)], cs, accs)

    def cond(carry):
        return jnp.logical_and(carry[0] >= 0, carry[1] > SB_UNDERFLOW)

    def body(carry):
        j = carry[0]
        cs, accs = process([(j, False, None)], carry[2:2 + nh], carry[2 + nh:])
        return (j - 1, cmax(cs)) + tuple(cs) + tuple(accs)

    carry = lax.while_loop(cond, body, (i - 2, cmax(cs)) + tuple(cs) + tuple(accs))
    for c in range(nh):
        o_ref[c] = carry[2 + nh + c].T[:, :SB_DIM].astype(BF16)


def _tri(n):
    return jnp.asarray(np.tril(np.ones((n, n), np.float32), -1), BF16)


def _sb_attn(q, k, vt):
    rows = q.shape[1]
    tq = min(TQ_SB, rows)
    nh = SB_HEADS
    return pl.pallas_call(
        _sb_attn_kernel,
        grid=(H_SB // nh, rows // tq),
        in_specs=[pl.BlockSpec((nh, tq, SB_DIM), lambda h, i: (h, i, 0)),
                  pl.BlockSpec((nh, rows, SB_DIM), lambda h, i: (h, 0, 0)),
                  pl.BlockSpec((nh, VT_ROWS, rows), lambda h, i: (h, 0, 0)),
                  _full((tq, tq))],
        out_specs=pl.BlockSpec((nh, tq, SB_DIM), lambda h, i: (h, i, 0)),
        out_shape=jax.ShapeDtypeStruct((H_SB, rows, SB_DIM), BF16),
        compiler_params=_cparams("parallel", "arbitrary"),
        name="sb_attn",
    )(q, k, vt, _tri(tq).T)


def _mem_kv_kernel(mem_ref, gmem_ref, w_ref, gk_ref, mk_ref, mv_ref):
    mn = (_rms(mem_ref[...]) * gmem_ref[...]).astype(BF16)
    kv = _dot(mn, w_ref[...])
    ks = [_rms(kv[:, hd * MEM_DIM:(hd + 1) * MEM_DIM]) for hd in range(H_MEM)]
    mk_ref[...] = jnp.concatenate(ks, axis=-1) * gk_ref[...]
    mv_ref[...] = kv[:, MEM_W:]


def _mem_kv(mem2d, p):
    n = mem2d.shape[0]
    return pl.pallas_call(
        _mem_kv_kernel,
        grid=(1,),
        in_specs=[_full((n, D_MODEL)), _full((1, D_MODEL)), _full((D_MODEL, 2 * MEM_W)), _full((1, MEM_W))],
        out_specs=(_full((n, MEM_W)), _full((n, MEM_W))),
        out_shape=(jax.ShapeDtypeStruct((n, MEM_W), F32), jax.ShapeDtypeStruct((n, MEM_W), F32)),
        compiler_params=_cparams("arbitrary"),
        name="mem_kv",
    )(mem2d, p["g_mem"], p["w_mem"], p["g_mem_k"])


def _mem_heads(mq, mk, mv):
    outs = []
    for hd in range(H_MEM):
        sl = slice(hd * MEM_DIM, (hd + 1) * MEM_DIM)
        s = _dot_nt(mq[:, sl], mk[:, sl])
        pr = jnp.exp(s - jnp.max(s, axis=-1, keepdims=True))
        o = _dot(pr.astype(BF16), mv[:, sl])
        outs.append(o / jnp.sum(pr, axis=-1, keepdims=True))
    return jnp.concatenate(outs, axis=-1)


def _mem_attn_kernel(mq_ref, mk_ref, mv_ref, o_ref):
    o_ref[...] = _mem_heads(mq_ref[...], mk_ref[...].astype(BF16), mv_ref[...].astype(BF16)).astype(BF16)


def _mem_attn(mq, mk, mv):
    rows = mq.shape[0]
    tm = min(TM_MEM, rows)
    n = mk.shape[0]
    return pl.pallas_call(
        _mem_attn_kernel,
        grid=(rows // tm,),
        in_specs=[pl.BlockSpec((tm, MEM_W), lambda i: (i, 0)), _full((n, MEM_W)), _full((n, MEM_W))],
        out_specs=pl.BlockSpec((tm, MEM_W), lambda i: (i, 0)),
        out_shape=jax.ShapeDtypeStruct((rows, MEM_W), BF16),
        compiler_params=_cparams("parallel"),
        name="mem_attn",
    )(mq, mk, mv)


def _sample_attn_kernel(past, q_ref, kn_ref, vn_ref, sq_ref, skn_ref, svn_ref, mq_ref,
                        clat_ref, crope_ref, csk_ref, csv_ref, cmk_ref, cmv_ref,
                        wkt_ref, wkn_ref, ind_ref, ones_ref, erow_ref, gkn_ref, gkr_ref, wvh_ref, rep_ref,
                        tri_ref, tris_ref, omla_ref, osb_ref, omem_ref):
    ds = q_ref.shape[1]
    hs = H_MLA * ds
    ds_shift = ds.bit_length() - 1
    tc = tri_ref.shape[0]
    row = lax.broadcasted_iota(jnp.int32, (hs, 1), 0)
    qpos = past + (row & (ds - 1))
    rows_of = lambda x, hd: x[hd * ds:(hd + 1) * ds]

    latb = clat_ref[0, 0].astype(BF16)
    rope = crope_ref[0, 0]
    ropeb = rope.astype(BF16)
    kn = _dot(latb, wkn_ref[...])
    ssq = _dot((kn * kn).astype(BF16), ind_ref[...]) + _dot((rope * rope).astype(BF16), ones_ref[...])
    inv_hi, inv_lo = _split(lax.rsqrt(ssq * (1.0 / QK_DIM) + EPS))
    inv_rows = _dot_nt(erow_ref[...], inv_hi) + _dot_nt(erow_ref[...], inv_lo)
    qabs, qrope, s2 = [], [], []
    for hd in range(H_MLA):
        qh = q_ref[hd].astype(F32)
        qabs.append(_dot((qh[:, :NOPE_DIM] * gkn_ref[...]).astype(BF16), wkt_ref[hd]))
        qrope.append((qh[:, NOPE_DIM:] * gkr_ref[...]).astype(BF16))
        s2.append(_dot_nt(q_ref[hd], kn_ref[hd]))
    s1 = _dot_nt(jnp.concatenate(qabs, axis=0).astype(BF16), latb) + _dot_nt(jnp.concatenate(qrope, axis=0), ropeb)
    s1 = s1 * inv_rows
    s2 = jnp.concatenate(s2, axis=0)
    q_chunk = qpos // CHUNK
    s1 = jnp.where((lax.broadcasted_iota(jnp.int32, s1.shape, 1) // CHUNK) <= q_chunk, s1, NEG_BIG)
    s2 = jnp.where(((past + lax.broadcasted_iota(jnp.int32, s2.shape, 1)) // CHUNK) <= q_chunk, s2, NEG_BIG)
    m = jnp.maximum(jnp.max(s1, axis=-1, keepdims=True), jnp.max(s2, axis=-1, keepdims=True))
    p1 = jnp.exp2(s1 - m)
    p2 = jnp.exp2(s2 - m)
    den = jnp.sum(p1, axis=-1, keepdims=True) + jnp.sum(p2, axis=-1, keepdims=True)
    olat = _dot(p1.astype(BF16), latb)
    for hd in range(H_MLA):
        o = _dot(rows_of(olat, hd).astype(BF16), wvh_ref[hd]) + _dot(rows_of(p2, hd).astype(BF16), vn_ref[hd])
        omla_ref[hd] = (o / rows_of(den, hd)).astype(BF16)

    sq_all = jnp.concatenate([sq_ref[hd] for hd in range(H_SB)], axis=0)
    sqbd = _dot(sq_all, rep_ref[...])
    lane_head = lax.broadcasted_iota(jnp.int32, sqbd.shape, 1) // SB_DIM
    sqbd = jnp.where(lane_head == lax.shift_right_logical(row, ds_shift), sqbd, 0.0).astype(BF16)
    z1 = _dot_nt(sqbd, csk_ref[0, 0].astype(BF16))
    z2 = jnp.concatenate([_dot_nt(sq_ref[hd], skn_ref[hd]) for hd in range(H_SB)], axis=0)
    before_n = lax.broadcasted_iota(jnp.int32, z2.shape, 1) < (row & (ds - 1))
    l2 = jnp.where(before_n, -_softplus(z2), 0.0)
    a2 = jnp.where(before_n, jnp.exp(z2 + l2 + _dot2(l2, tris_ref[...])), 0.0)
    c = jnp.sum(l2, axis=-1, keepdims=True)
    l1 = -_softplus(z1)
    tri = tri_ref[...]
    a1 = [None] * (past // tc)
    for cb in reversed(range(past // tc)):
        cs = slice(cb * tc, (cb + 1) * tc)
        lc = l1[:, cs]
        a1[cb] = jnp.exp(z1[:, cs] + lc + _dot2(lc, tri) + c).astype(BF16)
        c = c + jnp.sum(lc, axis=-1, keepdims=True)
    osb = _dot(jnp.concatenate(a1, axis=1), csv_ref[0, 0].astype(BF16))
    for hd in range(H_SB):
        o = rows_of(osb, hd)[:, hd * SB_DIM:(hd + 1) * SB_DIM] + _dot(rows_of(a2, hd).astype(BF16), svn_ref[hd])
        osb_ref[hd] = o.astype(BF16)

    omem_ref[...] = _mem_heads(mq_ref[...], cmk_ref[0, 0].astype(BF16), cmv_ref[0, 0].astype(BF16)).astype(BF16)


def _sample_attn(q, kn, vn, sq, skn, svn, mq, c_lat, c_rope, c_sbk, c_sbv, c_mk, c_mv, layer, p):
    nb, past = c_lat.shape[1], c_lat.shape[2]
    ds = q.shape[1] // nb
    assert ds & (ds - 1) == 0, "row -> query index uses a bit mask"
    n_mem = c_mk.shape[2]
    tc = min(256, past)
    hs = H_MLA * ds
    erow = np.zeros((hs, LANES), np.float32)
    erow[np.arange(hs), np.arange(hs) // ds] = 1.0
    rep = np.tile(np.eye(SB_DIM, dtype=np.float32), (1, H_SB))
    hm = lambda d: pl.BlockSpec((H_MLA, ds, d), lambda b: (0, b, 0))
    cache = lambda n, w: pl.BlockSpec((1, 1, n, w), lambda b: (layer, b, 0, 0))
    rows = q.shape[1]
    return pl.pallas_call(
        functools.partial(_sample_attn_kernel, past),
        grid=(nb,),
        in_specs=[hm(QK_DIM), hm(QK_DIM), hm(V_DIM), hm(SB_DIM), hm(SB_DIM), hm(SB_DIM),
                  pl.BlockSpec((ds, MEM_W), lambda b: (b, 0)),
                  cache(past, KV_LORA), cache(past, ROPE_DIM), cache(past, SB_W), cache(past, SB_W),
                  cache(n_mem, MEM_W), cache(n_mem, MEM_W),
                  _full((H_MLA, NOPE_DIM, KV_LORA)), _full((KV_LORA, H_MLA * NOPE_DIM)),
                  _full((H_MLA * NOPE_DIM, LANES)), _full((ROPE_DIM, LANES)), _full((hs, LANES)),
                  _full((1, NOPE_DIM)), _full((1, ROPE_DIM)), _full((H_MLA, KV_LORA, V_DIM)),
                  _full((SB_DIM, SB_W)), _full((tc, tc)), _full((ds, ds))],
        out_specs=(hm(V_DIM), hm(SB_DIM), pl.BlockSpec((ds, MEM_W), lambda b: (b, 0))),
        out_shape=(jax.ShapeDtypeStruct((H_MLA, rows, V_DIM), BF16),
                   jax.ShapeDtypeStruct((H_SB, rows, SB_DIM), BF16),
                   jax.ShapeDtypeStruct((rows, MEM_W), BF16)),
        compiler_params=_cparams("parallel"),
        name="sample_attn",
    )(q, kn, vn, sq, skn, svn, mq, c_lat, c_rope, c_sbk, c_sbv, c_mk, c_mv,
      p["w_k_t"], p["w_k_nope"], p["ind64"], p["ones_rope"], jnp.asarray(erow, BF16), p["g_k_nope"], p["g_k_rope"],
      p["w_v_h"], jnp.asarray(rep, BF16), _tri(tc), _tri(ds))


def _merge_kernel(x_ref, gattn_ref, wg_ref, omla_ref, osb_ref, omem_ref, woa_ref, wob_ref, wom_ref, wout_ref,
                  gffn_ref, wrh_ref, wrl_ref, br_ref, x1_ref, idx_ref, gate_ref):
    x = x_ref[...]
    h = (_rms(x) * gattn_ref[...]).astype(BF16)
    g = 1.0 / (1.0 + jnp.exp(-_dot(h, wg_ref[...])))
    ua = _dot(jnp.concatenate([omla_ref[hd] for hd in range(H_MLA)], axis=-1), woa_ref[...])
    ub = _dot(jnp.concatenate([osb_ref[hd] for hd in range(H_SB)], axis=-1), wob_ref[...])
    um = _dot(omem_ref[...], wom_ref[...])
    u = g[:, :D_MODEL] * ua + g[:, D_MODEL:2 * D_MODEL] * ub + g[:, 2 * D_MODEL:] * um
    x1 = x + _dot(u.astype(BF16), wout_ref[...])
    x1_ref[...] = x1
    xh, xl = _split(_rms(x1) * gffn_ref[...])
    lg = _dot(xh, wrh_ref[...]) + _dot(xh, wrl_ref[...]) + _dot(xl, wrh_ref[...]) + br_ref[...]
    lane = lax.broadcasted_iota(jnp.int32, lg.shape, 1).astype(F32)
    vals, ids = [], []
    for _ in range(TOP_K):
        m = jnp.max(lg, axis=-1, keepdims=True)
        sel = jnp.min(jnp.where(lg == m, lane, float(LANES)), axis=-1, keepdims=True)
        vals.append(m)
        ids.append(sel)
        lg = jnp.where(lane == sel, NEG_BIG, lg)
    es = [jnp.exp(v - vals[0]) for v in vals]
    den = es[0] + es[1] + es[2] + es[3]
    idx_o = jnp.zeros(lg.shape, F32)
    gate_o = jnp.zeros(lg.shape, F32)
    for k in range(TOP_K):
        idx_o = jnp.where(lane == float(k), ids[k], idx_o)
        gate_o = jnp.where(lane == float(k), es[k] / den, gate_o)
    idx_ref[...] = idx_o.astype(jnp.int32)
    gate_ref[...] = gate_o


def _merge(x2d, omla, osb, omem, p):
    rows = x2d.shape[0]
    tm = TM_MERGE
    row = lambda w: pl.BlockSpec((tm, w), lambda i: (i, 0))
    hm = lambda d: pl.BlockSpec((H_MLA, tm, d), lambda i: (0, i, 0))
    return pl.pallas_call(
        _merge_kernel,
        grid=(rows // tm,),
        in_specs=[row(D_MODEL), _full((1, D_MODEL)), _full((D_MODEL, 3 * D_MODEL)), hm(V_DIM), hm(SB_DIM),
                  row(MEM_W), _full((MLA_W, D_MODEL)), _full((SB_W, D_MODEL)),
                  _full((MEM_W, D_MODEL)), _full((D_MODEL, D_MODEL)), _full((1, D_MODEL)),
                  _full((D_MODEL, LANES)), _full((D_MODEL, LANES)), _full((1, LANES))],
        out_specs=(row(D_MODEL), row(LANES), row(LANES)),
        out_shape=(jax.ShapeDtypeStruct((rows, D_MODEL), F32), jax.ShapeDtypeStruct((rows, LANES), jnp.int32),
                   jax.ShapeDtypeStruct((rows, LANES), F32)),
        compiler_params=_cparams("parallel"),
        name="merge",
    )(x2d, p["g_attn"], p["w_g"], omla, osb, omem, p["w_o_mla"], p["w_o_sb"], p["w_o_mem"], p["w_out"],
      p["g_ffn"], p["w_r_hi"], p["w_r_lo"], p["b_r"])


def _row_dma(src, src_row, dst, dst_row, sem):
    return pltpu.make_async_copy(src.at[pl.ds(src_row, 1), :], dst.at[pl.ds(dst_row, 1), :], sem)


def _row_dma_wait(src, dst, sem, n):
    def body(r, carry):
        _row_dma(src, 0, dst, 0, sem).wait()
        return carry

    lax.fori_loop(0, n, body, 0, unroll=DMA_UNROLL)


def _dispatch_kernel(dest_ref, x_ref, xs_in, xs_out, sem):
    del xs_in
    n = dest_ref.shape[2]
    for r in range(n):
        _row_dma(x_ref, r // TOP_K, xs_out, dest_ref[0, 0, r], sem.at[0]).start()
    _row_dma_wait(x_ref, xs_out, sem.at[0], n)


def _moe_dispatch(x1, dest, xs):
    rows = x1.shape[0]
    tt = TT_COMBINE
    nt = rows // tt
    return pl.pallas_call(
        _dispatch_kernel,
        grid=(nt,),
        in_specs=[pl.BlockSpec((1, 1, TOP_K * tt), lambda t: (t, 0, 0), memory_space=pltpu.SMEM),
                  pl.BlockSpec((tt, D_MODEL), lambda t: (t, 0)),
                  pl.BlockSpec(memory_space=pl.ANY)],
        out_specs=pl.BlockSpec(memory_space=pl.ANY),
        out_shape=jax.ShapeDtypeStruct(xs.shape, xs.dtype),
        input_output_aliases={2: 0},
        scratch_shapes=[pltpu.SemaphoreType.DMA((1,))],
        compiler_params=_cparams("arbitrary"),
        name="moe_dispatch",
    )(dest.reshape(nt, 1, TOP_K * tt), x1, xs)


def _moe_kernel(be_ref, nact_ref, xs_ref, gffn_ref, wgu_ref, bgu_ref, wd_ref, bd_ref, o_ref, wgu_sc, wd_sc):
    b = pl.program_id(0)

    @pl.when(b >= nact_ref[0])
    def _():
        o_ref[...] = jnp.zeros(o_ref.shape, o_ref.dtype)

    @pl.when(b < nact_ref[0])
    def _():
        changed = jnp.logical_or(b == 0, be_ref[b] != be_ref[jnp.maximum(b - 1, 0)])

        @pl.when(changed)
        def _():
            wgu_sc[...] = wgu_ref[0].astype(BF16)
            wd_sc[...] = wd_ref[0].astype(BF16)

        xb = (_rms(xs_ref[...]) * gffn_ref[...]).astype(BF16)
        gu = _dot(xb, wgu_sc[...]) + bgu_ref[0]
        g = jnp.minimum(gu[:, :D_FF], SWIGLU_LIMIT)
        u = jnp.clip(gu[:, D_FF:], -SWIGLU_LIMIT, SWIGLU_LIMIT)
        hid = (u + 1.0) * (g / (1.0 + jnp.exp(-SWIGLU_ALPHA * g)))
        o_ref[...] = _dot(hid.astype(BF16), wd_sc[...]) + bd_ref[0]


def _moe_experts(xs, block_expert, n_active, p):
    n_blocks = block_expert.shape[0]
    bm = BM_MOE
    blk = lambda b, be, na: (jnp.minimum(b, na[0] - 1), 0)
    grid_spec = pltpu.PrefetchScalarGridSpec(
        num_scalar_prefetch=2,
        grid=(n_blocks,),
        in_specs=[
            pl.BlockSpec((bm, D_MODEL), blk),
            pl.BlockSpec((1, D_MODEL), lambda b, be, na: (0, 0)),
            pl.BlockSpec((1, D_MODEL, 2 * D_FF), lambda b, be, na: (be[b], 0, 0)),
            pl.BlockSpec((1, 1, 2 * D_FF), lambda b, be, na: (be[b], 0, 0)),
            pl.BlockSpec((1, D_FF, D_MODEL), lambda b, be, na: (be[b], 0, 0)),
            pl.BlockSpec((1, 1, D_MODEL), lambda b, be, na: (be[b], 0, 0)),
        ],
        out_specs=pl.BlockSpec((bm, D_MODEL), lambda b, be, na: (b, 0)),
        scratch_shapes=[pltpu.VMEM((D_MODEL, 2 * D_FF), BF16), pltpu.VMEM((D_FF, D_MODEL), BF16)],
    )
    return pl.pallas_call(
        _moe_kernel,
        grid_spec=grid_spec,
        out_shape=jax.ShapeDtypeStruct((n_blocks * bm, D_MODEL), F32),
        compiler_params=_cparams("arbitrary"),
        name="moe_experts",
    )(block_expert, n_active, xs, p["g_ffn"], p["w_gate_up"], p["b_gate_up"], p["w_down"], p["b_down"])


def _combine_gather(y_hbm, pos_ref, dst, sem, n):
    for r in range(n):
        _row_dma(y_hbm, pos_ref[0, 0, r], dst, r, sem).start()


def _combine_kernel(nta, pos_ref, posn_ref, y_hbm, x1a_ref, ga_ref, x1b_ref, gb_ref, oa_ref, ob_ref, buf, sem):
    t = pl.program_id(0)
    nt = pl.num_programs(0)
    tt = oa_ref.shape[0]
    n = TOP_K * tt
    slot = t % 2

    @pl.when(t == 0)
    def _():
        _combine_gather(y_hbm, pos_ref, buf.at[0], sem.at[0], n)

    @pl.when(t + 1 < nt)
    def _():
        _combine_gather(y_hbm, posn_ref, buf.at[1 - slot], sem.at[1 - slot], n)

    _row_dma_wait(y_hbm, buf.at[slot], sem.at[slot], n)

    def combine(x1_ref, gate_ref, o_ref):
        acc = x1_ref[...]
        gate = gate_ref[...]
        for k in range(TOP_K):
            acc = acc + gate[:, k:k + 1] * buf[slot, k * tt:(k + 1) * tt, :]
        o_ref[...] = acc

    @pl.when(t < nta)
    def _():
        combine(x1a_ref, ga_ref, oa_ref)

    @pl.when(t >= nta)
    def _():
        combine(x1b_ref, gb_ref, ob_ref)


def _moe_combine(x1a, gate_a, x1b, gate_b, yb, pos):
    tt = TT_COMBINE
    nta, ntb = x1a.shape[0] // tt, x1b.shape[0] // tt
    nt = nta + ntb
    pos3 = pos.reshape(nt, tt, TOP_K).transpose(0, 2, 1).reshape(nt, 1, TOP_K * tt)
    ia = lambda t: (jnp.minimum(t, nta - 1), 0)
    ib = lambda t: (jnp.maximum(t - nta, 0), 0)
    return pl.pallas_call(
        functools.partial(_combine_kernel, nta),
        grid=(nt,),
        in_specs=[pl.BlockSpec((1, 1, TOP_K * tt), lambda t: (t, 0, 0), memory_space=pltpu.SMEM),
                  pl.BlockSpec((1, 1, TOP_K * tt), lambda t: (jnp.minimum(t + 1, nt - 1), 0, 0),
                               memory_space=pltpu.SMEM),
                  pl.BlockSpec(memory_space=pl.ANY),
                  pl.BlockSpec((tt, D_MODEL), ia), pl.BlockSpec((tt, LANES), ia),
                  pl.BlockSpec((tt, D_MODEL), ib), pl.BlockSpec((tt, LANES), ib)],
        out_specs=(pl.BlockSpec((tt, D_MODEL), ia), pl.BlockSpec((tt, D_MODEL), ib)),
        out_shape=(jax.ShapeDtypeStruct(x1a.shape, F32), jax.ShapeDtypeStruct(x1b.shape, F32)),
        scratch_shapes=[pltpu.VMEM((2, TOP_K * tt, D_MODEL), F32), pltpu.SemaphoreType.DMA((2,))],
        compiler_params=_cparams("arbitrary"),
        name="moe_combine",
    )(pos3, pos3, yb, x1a, gate_a, x1b, gate_b)


def _moe_route(idx):
    rows = idx.shape[0]
    n = rows * TOP_K
    bm = BM_MOE
    e = idx.reshape(n)
    onehot = (e[:, None] == jnp.arange(N_EXPERTS, dtype=jnp.int32)[None, :]).astype(jnp.int32)
    csum = jnp.cumsum(onehot, axis=0)
    rank = jnp.sum(csum * onehot, axis=1) - 1
    counts = csum[-1]
    padded = (counts + bm - 1) // bm * bm
    pend = jnp.cumsum(padded)
    dest = ((pend - padded)[e] + rank).astype(jnp.int32).reshape(rows, TOP_K)
    n_blocks = -(-n // bm) + N_EXPERTS
    starts = jnp.arange(n_blocks, dtype=jnp.int32) * bm
    block_expert = jnp.minimum(
        jnp.sum((pend[None, :] <= starts[:, None]).astype(jnp.int32), axis=1), N_EXPERTS - 1).astype(jnp.int32)
    n_active = (pend[-1:] // bm).astype(jnp.int32)
    return dest, block_expert, n_active


def _pack_layer(l, g_attn, w_in, g_q_lat, w_q_b, g_q_nope, g_q_rope, g_kv_lat, w_kv_b, g_k_nope, g_k_rope, g_mem,
                w_mem_kv, g_mem_q, g_mem_k, w_o_mla, w_o_sb, w_o_mem, w_out, g_ffn, w_router, b_router,
                w_gate_up, b_gate_up, w_down, b_down):
    w = w_in[l]
    off = np.cumsum((Q_LORA, KV_LORA, ROPE_DIM, SB_W, SB_W, SB_W, MEM_W))
    o_kpe, o_sbq = int(off[1]), int(off[2])
    o_gate = int(off[6])
    kpe1 = w[:, o_kpe:o_kpe + ROPE_HALF]
    kpe2 = w[:, o_kpe + ROPE_HALF:o_kpe + ROPE_DIM]
    zl = jnp.zeros((D_MODEL, KPE_LANE), F32)
    zr = jnp.zeros((D_MODEL, LANES - KPE_LANE - ROPE_DIM), F32)
    w_a = jnp.concatenate([w[:, :o_kpe], zl, kpe1, kpe2, zr, zl, -kpe2, kpe1, zr, w[:, o_sbq:o_gate]], axis=1)
    wq = w_q_b[l]
    z_n = jnp.zeros((Q_LORA, H_MLA, NOPE_DIM), F32)
    wq_sw = jnp.concatenate([z_n, -wq[..., NOPE_DIM + ROPE_HALF:], wq[..., NOPE_DIM:NOPE_DIM + ROPE_HALF]], axis=-1)
    lane = np.arange(QK_W)
    ind = np.zeros((QK_W, LANES), np.float32)
    ind[lane, lane // QK_DIM] = 1.0
    ind64 = np.zeros((H_MLA * NOPE_DIM, LANES), np.float32)
    ind64[np.arange(H_MLA * NOPE_DIM), np.arange(H_MLA * NOPE_DIM) // NOPE_DIM] = 1.0
    pk = np.zeros((ROPE_DIM, QK_W), np.float32)
    for hd in range(H_MLA):
        pk[np.arange(ROPE_DIM), hd * QK_DIM + NOPE_DIM + np.arange(ROPE_DIM)] = 1.0
    wkv = w_kv_b[l]
    w_k = jnp.concatenate([wkv[..., :NOPE_DIM], jnp.zeros((KV_LORA, H_MLA, ROPE_DIM), F32)], axis=-1)
    g_q = jnp.tile(jnp.concatenate([g_q_nope[l], g_q_rope[l], g_q_rope[l]]), H_MLA) * (QK_DIM ** -0.5 * LOG2E)
    g_k = jnp.tile(jnp.concatenate([g_k_nope[l], g_k_rope[l], g_k_rope[l]]), H_MLA)
    wm = w_mem_kv[l]
    w_mem = jnp.concatenate([wm[..., :MEM_DIM].reshape(D_MODEL, MEM_W), wm[..., MEM_DIM:].reshape(D_MODEL, MEM_W)], 1)
    w_r = jnp.concatenate([w_router[l], jnp.zeros((D_MODEL, LANES - N_EXPERTS), F32)], axis=1)
    w_r_hi = w_r.astype(BF16)
    b_r = jnp.concatenate([b_router[l].astype(F32), jnp.full((LANES - N_EXPERTS,), NEG_BIG, F32)])
    return {
        "g_attn": g_attn[l][None], "w_a": w_a.astype(BF16), "g_q_lat": g_q_lat[l][None],
        "w_q": wq.reshape(Q_LORA, QK_W).astype(BF16), "w_q_sw": wq_sw.reshape(Q_LORA, QK_W).astype(BF16),
        "ind96": jnp.asarray(ind, BF16), "ind96_t": jnp.asarray(ind.T, BF16), "g_q": g_q[None],
        "g_kv_lat": g_kv_lat[l][None], "g_mem_q": jnp.tile(g_mem_q[l], H_MEM)[None] * (MEM_DIM ** -0.5),
        "w_k": w_k.reshape(KV_LORA, QK_W).astype(BF16), "p_kpe": jnp.asarray(pk, BF16), "g_k": g_k[None],
        "w_v": wkv[..., NOPE_DIM:].reshape(KV_LORA, MLA_W).astype(BF16),
        "w_v_t": wkv[..., NOPE_DIM:].reshape(KV_LORA, MLA_W).T.astype(BF16),
        "w_v_h": wkv[..., NOPE_DIM:].transpose(1, 0, 2).astype(BF16),
        "w_k_t": wkv[..., :NOPE_DIM].transpose(1, 2, 0).astype(BF16),
        "w_k_nope": wkv[..., :NOPE_DIM].reshape(KV_LORA, H_MLA * NOPE_DIM).astype(BF16),
        "ind64": jnp.asarray(ind64, BF16), "ones_rope": jnp.ones((ROPE_DIM, LANES), BF16),
        "g_k_nope": g_k_nope[l][None], "g_k_rope": jnp.concatenate([g_k_rope[l], g_k_rope[l]])[None],
        "g_mem": g_mem[l][None], "w_mem": w_mem.astype(BF16), "g_mem_k": jnp.tile(g_mem_k[l], H_MEM)[None],
        "w_g": w[:, o_gate:].astype(BF16), "w_sbv_t": w[:, o_gate - MEM_W - SB_W:o_gate - MEM_W].T.astype(BF16),
        "w_o_mla": w_o_mla[l].astype(BF16), "w_o_sb": w_o_sb[l].astype(BF16),
        "w_o_mem": w_o_mem[l].astype(BF16), "w_out": w_out[l].astype(BF16), "g_ffn": g_ffn[l][None],
        "w_r_hi": w_r_hi, "w_r_lo": (w_r - w_r_hi.astype(F32)).astype(BF16), "b_r": b_r[None],
        "w_gate_up": w_gate_up[l], "b_gate_up": b_gate_up[l][:, None, :], "w_down": w_down[l],
        "b_down": b_down[l][:, None, :],
    }


def kernel(x_prompt, x_sample, mem_prompt, cache_mla_latent, cache_mla_rope, cache_sb_k, cache_sb_v, cache_mem_k, cache_mem_v, g_attn, w_in, g_q_lat, w_q_b, g_q_nope, g_q_rope, g_kv_lat, w_kv_b, g_k_nope, g_k_rope, g_mem, w_mem_kv, g_mem_q, g_mem_k, w_o_mla, w_o_sb, w_o_mem, w_out, g_ffn, w_router, b_router, w_gate_up, b_gate_up, w_down, b_down):
    depth = g_attn.shape[0]
    bp, sp, _ = x_prompt.shape
    bs, ss, _ = x_sample.shape
    past = cache_mla_latent.shape[2]
    n_mem = mem_prompt.shape[1]
    assert bp == 1 and sp % TQ_MLA == 0 and sp % TM_KV == 0 and (bs * ss) % TM_PROJ == 0 and TM_PROJ % ss == 0
    rows_s = bs * ss
    c_sbk = cache_sb_k.reshape(depth, bs, past, SB_W)
    c_sbv = cache_sb_v.reshape(depth, bs, past, SB_W)
    c_mk = cache_mem_k.reshape(depth, bs, n_mem, MEM_W)
    c_mv = cache_mem_v.reshape(depth, bs, n_mem, MEM_W)
    xp = x_prompt.reshape(sp, D_MODEL)
    xs = x_sample.reshape(rows_s, D_MODEL)
    mem2d = mem_prompt.reshape(n_mem, D_MODEL)
    base_p, off_p = np.arange(sp // TM_PROJ) * TM_PROJ, np.arange(TM_PROJ)
    base_s, off_s = np.full((rows_s // TM_PROJ,), past), np.arange(TM_PROJ) % ss
    outs = [[] for _ in range(10)]
    for l in range(depth):
        p = _pack_layer(l, g_attn, w_in, g_q_lat, w_q_b, g_q_nope, g_q_rope, g_kv_lat, w_kv_b, g_k_nope, g_k_rope,
                        g_mem, w_mem_kv, g_mem_q, g_mem_k, w_o_mla, w_o_sb, w_o_mem, w_out, g_ffn, w_router,
                        b_router, w_gate_up, b_gate_up, w_down, b_down)
        q, lat, kpe, sq, sk, sv, skh, _, mq, svt = _proj(xp, base_p, off_p, p)
        kh, _, vth = _kv_expand(lat, kpe, p)
        o_mla = _mla_attn(q, kh, vth)
        o_sb = _sb_attn(sq, skh, svt)
        mk, mv = _mem_kv(mem2d, p)
        o_mem = _mem_attn(mq, mk, mv)
        x1p, idx_p, gate_p = _merge(xp, o_mla, o_sb, o_mem, p)
        q_s, lat_s, kpe_s, sq_s, sk_s, sv_s, skh_s, svh_s, mq_s, _ = _proj(xs, base_s, off_s, p)
        kn, vn, _ = _kv_expand(lat_s, kpe_s, p)
        o_mla_s, o_sb_s, o_mem_s = _sample_attn(q_s, kn, vn, sq_s, skh_s, svh_s, mq_s, cache_mla_latent,
                                                cache_mla_rope, c_sbk, c_sbv, c_mk, c_mv, l, p)
        x1s, idx_s, gate_s = _merge(xs, o_mla_s, o_sb_s, o_mem_s, p)
        dest, block_expert, n_active = _moe_route(jnp.concatenate([idx_p[:, :TOP_K], idx_s[:, :TOP_K]], axis=0))
        slots = jnp.zeros((block_expert.shape[0] * BM_MOE, D_MODEL), F32)
        slots = _moe_dispatch(x1s, dest[sp:], _moe_dispatch(x1p, dest[:sp], slots))
        yb = _moe_experts(slots, block_expert, n_active, p)
        xp, xs = _moe_combine(x1p, gate_p, x1s, gate_s, yb, dest)
        for lst, val in zip(outs, (lat.reshape(bp, sp, KV_LORA), kpe.reshape(bp, sp, ROPE_DIM),
                                   sk.reshape(bp, sp, H_SB, SB_DIM), sv.reshape(bp, sp, H_SB, SB_DIM),
                                   mk.reshape(bp, n_mem, H_MEM, MEM_DIM), mv.reshape(bp, n_mem, H_MEM, MEM_DIM),
                                   lat_s.reshape(bs, ss, KV_LORA), kpe_s.reshape(bs, ss, ROPE_DIM),
                                   sk_s.reshape(bs, ss, H_SB, SB_DIM), sv_s.reshape(bs, ss, H_SB, SB_DIM))):
            lst.append(val)
    return (xp.reshape(bp, sp, D_MODEL), xs.reshape(bs, ss, D_MODEL)) + tuple(jnp.stack(o) for o in outs)
```

```python
import functools

import numpy as np
import jax
import jax.numpy as jnp
from jax import lax
from jax.experimental import pallas as pl
from jax.experimental.pallas import tpu as pltpu

F32 = jnp.float32
BF16 = jnp.bfloat16

D_MODEL = 1024
CHUNK = 64
EPS = 1e-6
H_MLA = 8
NOPE_DIM = 64
ROPE_DIM = 32
ROPE_HALF = ROPE_DIM // 2
V_DIM = 64
Q_LORA = 384
KV_LORA = 256
ROPE_BASE = 10000.0
QK_DIM = NOPE_DIM + ROPE_DIM
QK_W = H_MLA * QK_DIM
H_SB = 8
SB_DIM = 64
H_MEM = 4
MEM_DIM = 128
N_EXPERTS = 32
TOP_K = 4
D_FF = 1024
SWIGLU_LIMIT = 7.0
SWIGLU_ALPHA = 1.702
MLA_W = H_MLA * V_DIM
SB_W = H_SB * SB_DIM
MEM_W = H_MEM * MEM_DIM

LANES = 128
VT_ROWS = LANES
MLA_CHAINS = 4
MLA_KSPLIT = 2
assert MLA_KSPLIT % 2 == 0
LOG2E = 1.4426950408889634
A_QLAT = 0
A_KVLAT = A_QLAT + Q_LORA
A_KPE = A_KVLAT + KV_LORA
A_KPE_SW = A_KPE + LANES
A_SBQ = A_KPE_SW + LANES
A_SBK = A_SBQ + SB_W
A_SBV = A_SBK + SB_W
A_MEMQ = A_SBV + SB_W
A_COLS = A_MEMQ + MEM_W
KPE_LANE = NOPE_DIM

TM_PROJ = 256
TM_KV = 512
TQ_MLA = 1024
TQ_SB = 256
SB_HEADS = 2
TM_MEM = 512
TM_MERGE = 256
BM_MOE = 512
TT_COMBINE = 256
DMA_UNROLL = 8
TOP_K_SHIFT = TOP_K.bit_length() - 1
assert 1 << TOP_K_SHIFT == TOP_K
VMEM_LIMIT = 56 * 1024 * 1024

SB_UNDERFLOW = -120.0
NEG_BIG = -3.0e38


def _cparams(*sem):
    return pltpu.CompilerParams(dimension_semantics=sem, vmem_limit_bytes=VMEM_LIMIT)


def _split(x):
    hi = x.astype(BF16)
    lo = (x - hi.astype(F32)).astype(BF16)
    return hi, lo


def _dot(a, b):
    return jnp.dot(a, b, preferred_element_type=F32)


def _dot2(x, m):
    hi, lo = _split(x)
    return _dot(hi, m) + _dot(lo, m)


def _dot_nt(a, b):
    return lax.dot_general(a, b, (((1,), (1,)), ((), ())), preferred_element_type=F32)


def _rms(x):
    return x * lax.rsqrt(jnp.mean(x * x, axis=-1, keepdims=True) + EPS)


def _softplus(z):
    return jnp.maximum(z, 0.0) + jnp.log(1.0 + jnp.exp(-jnp.abs(z)))


def _full(shape):
    n = len(shape)
    return pl.BlockSpec(shape, lambda *_: (0,) * n)


def _proj_kernel(x_ref, gattn_ref, wa_ref, gqlat_ref, wq_ref, wqs_ref, ind_ref, indt_ref, gq_ref, gkv_ref,
                 bcs_ref, ocos_ref, osin_ref, gmq_ref, wsvt_ref,
                 q_ref, lat_ref, kpe_ref, sq_ref, sk_ref, sv_ref, skh_ref, svh_ref, mq_ref, svt_ref):
    h = (_rms(x_ref[...]) * gattn_ref[...]).astype(BF16)
    z = _dot(h, wa_ref[...])
    svt = _dot_nt(wsvt_ref[...], h)
    pad = jnp.zeros((VT_ROWS - SB_DIM, svt.shape[1]), F32)
    for hd in range(H_SB):
        svt_ref[hd] = jnp.concatenate([svt[hd * SB_DIM:(hd + 1) * SB_DIM, :], pad], axis=0).astype(BF16)
    bc = bcs_ref[0, 0:1, :]
    bs = bcs_ref[0, 1:2, :]
    oc = ocos_ref[...]
    osn = osin_ref[...]
    cos_f = bc * oc - bs * osn
    sin_f = bs * oc + bc * osn
    qn = (_rms(z[:, A_QLAT:A_QLAT + Q_LORA]) * gqlat_ref[...]).astype(BF16)
    qr = _dot(qn, wq_ref[...]) * cos_f + _dot(qn, wqs_ref[...]) * sin_f
    ssq = _dot2(qr * qr, ind_ref[...])
    inv = lax.rsqrt(ssq * (1.0 / QK_DIM) + EPS)
    qo = qr * _dot2(inv, indt_ref[...]) * gq_ref[...]
    for hd in range(H_MLA):
        q_ref[hd] = qo[:, hd * QK_DIM:(hd + 1) * QK_DIM].astype(BF16)
    lat_ref[...] = _rms(z[:, A_KVLAT:A_KVLAT + KV_LORA]) * gkv_ref[...]
    kr = z[:, A_KPE:A_KPE + LANES] * cos_f[:, :LANES] + z[:, A_KPE_SW:A_KPE_SW + LANES] * sin_f[:, :LANES]
    kpe_ref[...] = kr[:, KPE_LANE:KPE_LANE + ROPE_DIM]
    sbq = z[:, A_SBQ:A_SBQ + SB_W] * (SB_DIM ** -0.5)
    sbk = z[:, A_SBK:A_SBK + SB_W]
    sbv = z[:, A_SBV:A_SBV + SB_W]
    sk_ref[...] = sbk
    sv_ref[...] = sbv
    for hd in range(H_SB):
        sl = slice(hd * SB_DIM, (hd + 1) * SB_DIM)
        sq_ref[hd] = sbq[:, sl].astype(BF16)
        skh_ref[hd] = sbk[:, sl].astype(BF16)
        svh_ref[hd] = sbv[:, sl].astype(BF16)
    mqs = []
    for hd in range(H_MEM):
        mqs.append(_rms(z[:, A_MEMQ + hd * MEM_DIM:A_MEMQ + (hd + 1) * MEM_DIM]))
    mq_ref[...] = (jnp.concatenate(mqs, axis=-1) * gmq_ref[...]).astype(BF16)


def _rope_tables(base_pos, off_pos):
    lane = np.arange(QK_W) % QK_DIM
    inv_freq = ROPE_BASE ** (-np.arange(ROPE_HALF, dtype=np.float64) / ROPE_HALF)
    freq = np.where(lane >= NOPE_DIM, inv_freq[(lane - NOPE_DIM) % ROPE_HALF], 0.0)
    ab = np.asarray(base_pos, np.float64)[:, None] * freq
    ao = np.asarray(off_pos, np.float64)[:, None] * freq
    bcs = np.stack([np.cos(ab), np.sin(ab)], axis=1).astype(np.float32)
    return jnp.asarray(bcs), jnp.asarray(np.cos(ao), F32), jnp.asarray(np.sin(ao), F32)


def _proj(x2d, base_pos, off_pos, p):
    rows = x2d.shape[0]
    tm = TM_PROJ
    nt = rows // tm
    bcs, ocos, osin = _rope_tables(base_pos, off_pos)
    row = lambda w: pl.BlockSpec((tm, w), lambda i: (i, 0))
    hm = lambda d: pl.BlockSpec((H_MLA, tm, d), lambda i: (0, i, 0))
    out_shape = (
        jax.ShapeDtypeStruct((H_MLA, rows, QK_DIM), BF16),
        jax.ShapeDtypeStruct((rows, KV_LORA), F32),
        jax.ShapeDtypeStruct((rows, ROPE_DIM), F32),
        jax.ShapeDtypeStruct((H_SB, rows, SB_DIM), BF16),
        jax.ShapeDtypeStruct((rows, SB_W), F32),
        jax.ShapeDtypeStruct((rows, SB_W), F32),
        jax.ShapeDtypeStruct((H_SB, rows, SB_DIM), BF16),
        jax.ShapeDtypeStruct((H_SB, rows, SB_DIM), BF16),
        jax.ShapeDtypeStruct((rows, MEM_W), BF16),
        jax.ShapeDtypeStruct((H_SB, VT_ROWS, rows), BF16),
    )
    return pl.pallas_call(
        _proj_kernel,
        grid=(nt,),
        in_specs=[row(D_MODEL), _full((1, D_MODEL)), _full((D_MODEL, A_COLS)), _full((1, Q_LORA)),
                  _full((Q_LORA, QK_W)), _full((Q_LORA, QK_W)), _full((QK_W, LANES)), _full((LANES, QK_W)),
                  _full((1, QK_W)), _full((1, KV_LORA)),
                  pl.BlockSpec((1, 2, QK_W), lambda i: (i, 0, 0)), _full((tm, QK_W)), _full((tm, QK_W)),
                  _full((1, MEM_W)), _full((SB_W, D_MODEL))],
        out_specs=(hm(QK_DIM), row(KV_LORA), row(ROPE_DIM), hm(SB_DIM), row(SB_W), row(SB_W), hm(SB_DIM),
                   hm(SB_DIM), row(MEM_W), pl.BlockSpec((H_SB, VT_ROWS, tm), lambda i: (0, 0, i))),
        out_shape=out_shape,
        compiler_params=_cparams("parallel"),
        name="proj",
    )(x2d, p["g_attn"], p["w_a"], p["g_q_lat"], p["w_q"], p["w_q_sw"], p["ind96"], p["ind96_t"], p["g_q"],
      p["g_kv_lat"], bcs, ocos, osin, p["g_mem_q"], p["w_sbv_t"])


def _expand_keys(lat, kpe, wk, pk, ind, indt, gk):
    kf = _dot(lat.astype(BF16), wk) + _dot2(kpe, pk)
    inv = lax.rsqrt(_dot2(kf * kf, ind) * (1.0 / QK_DIM) + EPS)
    return kf * _dot2(inv, indt) * gk


def _kv_expand_kernel(lat_ref, kpe_ref, wk_ref, pk_ref, ind_ref, indt_ref, gk_ref, wv_ref, wvt_ref,
                      k_ref, v_ref, vt_ref):
    lat = lat_ref[...]
    latb = lat.astype(BF16)
    ko = _expand_keys(lat, kpe_ref[...], wk_ref[...], pk_ref[...], ind_ref[...], indt_ref[...], gk_ref[...])
    v = _dot(latb, wv_ref[...])
    vt = _dot_nt(wvt_ref[...], latb)
    ones_row = (lax.broadcasted_iota(jnp.int32, (VT_ROWS - V_DIM, vt.shape[1]), 0) == 0).astype(F32)
    for hd in range(H_MLA):
        k_ref[hd] = ko[:, hd * QK_DIM:(hd + 1) * QK_DIM].astype(BF16)
        v_ref[hd] = v[:, hd * V_DIM:(hd + 1) * V_DIM].astype(BF16)
        vt_ref[hd] = jnp.concatenate([vt[hd * V_DIM:(hd + 1) * V_DIM, :], ones_row], axis=0).astype(BF16)


def _kv_expand(lat, kpe, p):
    rows = lat.shape[0]
    tm = min(TM_KV, rows)
    row = lambda w: pl.BlockSpec((tm, w), lambda i: (i, 0))
    hm = lambda d: pl.BlockSpec((H_MLA, tm, d), lambda i: (0, i, 0))
    return pl.pallas_call(
        _kv_expand_kernel,
        grid=(rows // tm,),
        in_specs=[row(KV_LORA), row(ROPE_DIM), _full((KV_LORA, QK_W)), _full((ROPE_DIM, QK_W)),
                  _full((QK_W, LANES)), _full((LANES, QK_W)), _full((1, QK_W)), _full((KV_LORA, MLA_W)),
                  _full((MLA_W, KV_LORA))],
        out_specs=(hm(QK_DIM), hm(V_DIM), pl.BlockSpec((H_MLA, VT_ROWS, tm), lambda i: (0, 0, i))),
        out_shape=(jax.ShapeDtypeStruct((H_MLA, rows, QK_DIM), BF16),
                   jax.ShapeDtypeStruct((H_MLA, rows, V_DIM), BF16),
                   jax.ShapeDtypeStruct((H_MLA, VT_ROWS, rows), BF16)),
        compiler_params=_cparams("parallel"),
        name="kv_expand",
    )(lat, kpe, p["w_k"], p["p_kpe"], p["ind96"], p["ind96_t"], p["g_k"], p["w_v"], p["w_v_t"])


def _mla_update(st, vt, m, acc):
    m_new = jnp.maximum(m, jnp.max(st, axis=0, keepdims=True))
    pr = jnp.exp2(st - m_new).astype(BF16)
    return m_new, acc * jnp.exp2(m - m_new) + _dot(vt, pr)


def _mla_step(qc, k, vt, m, acc, mask):
    st = _dot_nt(k, qc)
    if mask is not None:
        st = jnp.where(mask, st, NEG_BIG)
    return _mla_update(st, vt, m, acc)


def _mla_attn_kernel(q_ref, k_ref, vt_ref, o_ref, s_sc, p_sc):
    i = pl.program_id(1)
    tq = q_ref.shape[1]
    nc = MLA_CHAINS
    tc = tq // nc
    tk = tq // MLA_KSPLIT
    qs = [q_ref[0, c * tc:(c + 1) * tc, :] for c in range(nc)]
    key_chunk = lax.broadcasted_iota(jnp.int32, (tk, tc), 0) // CHUNK
    qry_chunk = lax.broadcasted_iota(jnp.int32, (tk, tc), 1) // CHUNK

    def scores(t, slot):
        k = k_ref[0, pl.ds(pl.multiple_of(t * tk, tk), tk), :]
        out = []
        for c in range(nc):
            st = _dot_nt(k, qs[c])
            s_sc[slot, c] = st
            out.append(jnp.max(st, axis=0, keepdims=True))
        return out

    def accumulate(t, slot, alphas, accs):
        vt = vt_ref[0, :, pl.ds(pl.multiple_of(t * tk, tk), tk)]
        return [accs[c] * alphas[c] + _dot(vt, p_sc[slot, c]) for c in range(nc)]

    def step(t, slot, cmax, alphas, ms, accs, last, mask_u):
        nxt = None if last else scores(t + 1, 1 - slot)
        accs = accumulate(jnp.maximum(t - 1, 0), 1 - slot, alphas, accs)
        m_new, alphas = [], []
        for c in range(nc):
            st = s_sc[slot, c]
            if mask_u is None:
                cm = cmax[c]
            else:
                st = jnp.where(key_chunk + mask_u * (tk // CHUNK) <= qry_chunk + c * (tc // CHUNK), st, NEG_BIG)
                cm = jnp.max(st, axis=0, keepdims=True)
            mn = jnp.maximum(ms[c], cm)
            p_sc[slot, c] = jnp.exp2(st - mn).astype(BF16)
            alphas.append(jnp.exp2(ms[c] - mn))
            m_new.append(mn)
        return nxt, alphas, m_new, accs

    p_sc[1] = jnp.zeros(p_sc.shape[1:], BF16)
    init = (tuple(scores(0, 0)) + (jnp.ones((1, tc), F32),) * nc + (jnp.full((1, tc), NEG_BIG, F32),) * nc
            + (jnp.zeros((VT_ROWS, tc), F32),) * nc)
    unpack = lambda carry: [list(carry[g * nc:(g + 1) * nc]) for g in range(4)]

    def body(j, carry):
        cmax, alphas, ms, accs = unpack(carry)
        for u in range(MLA_KSPLIT):
            cmax, alphas, ms, accs = step(j * MLA_KSPLIT + u, u % 2, cmax, alphas, ms, accs, False, None)
        return tuple(cmax) + tuple(alphas) + tuple(ms) + tuple(accs)

    cmax, alphas, ms, accs = unpack(lax.fori_loop(0, i, body, init))
    for u in range(MLA_KSPLIT):
        cmax, alphas, ms, accs = step(i * MLA_KSPLIT + u, u % 2, cmax, alphas, ms, accs, u + 1 == MLA_KSPLIT, u)
    accs = accumulate((i + 1) * MLA_KSPLIT - 1, (MLA_KSPLIT - 1) % 2, alphas, accs)
    for c in range(nc):
        ot = accs[c].T
        o_ref[0, c * tc:(c + 1) * tc, :] = (ot[:, :V_DIM] / ot[:, V_DIM:V_DIM + 1]).astype(BF16)


def _mla_attn(q, k, vt):
    rows = q.shape[1]
    tq = min(TQ_MLA, rows)
    return pl.pallas_call(
        _mla_attn_kernel,
        grid=(H_MLA, rows // tq),
        in_specs=[pl.BlockSpec((1, tq, QK_DIM), lambda h, i: (h, i, 0)),
                  pl.BlockSpec((1, rows, QK_DIM), lambda h, i: (h, 0, 0)),
                  pl.BlockSpec((1, VT_ROWS, rows), lambda h, i: (h, 0, 0))],
        out_specs=pl.BlockSpec((1, tq, V_DIM), lambda h, i: (h, i, 0)),
        out_shape=jax.ShapeDtypeStruct((H_MLA, rows, V_DIM), BF16),
        scratch_shapes=[pltpu.VMEM((2, MLA_CHAINS, tq // MLA_KSPLIT, tq // MLA_CHAINS), F32),
                        pltpu.VMEM((2, MLA_CHAINS, tq // MLA_KSPLIT, tq // MLA_CHAINS), BF16)],
        compiler_params=_cparams("parallel", "arbitrary"),
        name="mla_attn",
    )(q, k, vt)


def _sb_attn_kernel(q_ref, k_ref, vt_ref, triu_ref, o_ref):
    i = pl.program_id(1)
    nh, tq = q_ref.shape[0], q_ref.shape[1]
    triu = triu_ref[...]
    qs = [q_ref[c] for c in range(nh)]
    before = lax.broadcasted_iota(jnp.int32, (tq, tq), 0) < lax.broadcasted_iota(jnp.int32, (tq, tq), 1)

    def process(tiles, cs, accs):
        starts = [pl.multiple_of(j * tq, tq) for j, _, _ in tiles]
        zs, lks, bts = {}, {}, {}
        for t in range(len(tiles)):
            for c in range(nh):
                zs[t, c] = _dot_nt(k_ref[c, pl.ds(starts[t], tq), :], qs[c])
        for t, (_, diag, live) in enumerate(tiles):
            for c in range(nh):
                lk = -_softplus(zs[t, c])
                if diag:
                    lk = jnp.where(before, lk, 0.0)
                if live is not None:
                    lk = jnp.where(live, lk, 0.0)
                lks[t, c] = lk
        for key, lk in lks.items():
            hi, lo = _split(lk)
            bts[key] = _dot(triu, hi) + _dot(triu, lo)
        cs, accs = list(cs), list(accs)
        for t, (_, diag, live) in enumerate(tiles):
            for c in range(nh):
                a = jnp.exp(zs[t, c] + lks[t, c] + bts[t, c] + cs[c])
                if diag:
                    a = jnp.where(before, a, 0.0)
                if live is not None:
                    a = jnp.where(live, a, 0.0)
                accs[c] = accs[c] + _dot(vt_ref[c, :, pl.ds(starts[t], tq)], a.astype(BF16))
                cs[c] = cs[c] + jnp.sum(lks[t, c], axis=0, keepdims=True)
        return cs, accs

    def cmax(cs):
        out = jnp.max(cs[0])
        for c in cs[1:]:
            out = jnp.maximum(out, jnp.max(c))
        return out

    cs = [jnp.zeros((1, tq), F32)] * nh
    accs = [jnp.zeros((VT_ROWS, tq), F32)] * nh
    cs, accs = process([(i, True, None), (jnp.maximum(i - 1, 0), False, i > 0)], cs, accs)

    def cond(carry):
        return jnp.logical_and(carry[0] >= 0, carry[1] > SB_UNDERFLOW)

    def body(carry):
        j = carry[0]
        cs, accs = process([(j, False, None)], carry[2:2 + nh], carry[2 + nh:])
        return (j - 1, cmax(cs)) + tuple(cs) + tuple(accs)

    carry = lax.while_loop(cond, body, (i - 2, cmax(cs)) + tuple(cs) + tuple(accs))
    for c in range(nh):
        o_ref[c] = carry[2 + nh + c].T[:, :SB_DIM].astype(BF16)


def _tri(n):
    return jnp.asarray(np.tril(np.ones((n, n), np.float32), -1), BF16)


def _sb_attn(q, k, vt):
    rows = q.shape[1]
    tq = min(TQ_SB, rows)
    nh = SB_HEADS
    return pl.pallas_call(
        _sb_attn_kernel,
        grid=(H_SB // nh, rows // tq),
        in_specs=[pl.BlockSpec((nh, tq, SB_DIM), lambda h, i: (h, i, 0)),
                  pl.BlockSpec((nh, rows, SB_DIM), lambda h, i: (h, 0, 0)),
                  pl.BlockSpec((nh, VT_ROWS, rows), lambda h, i: (h, 0, 0)),
                  _full((tq, tq))],
        out_specs=pl.BlockSpec((nh, tq, SB_DIM), lambda h, i: (h, i, 0)),
        out_shape=jax.ShapeDtypeStruct((H_SB, rows, SB_DIM), BF16),
        compiler_params=_cparams("parallel", "arbitrary"),
        name="sb_attn",
    )(q, k, vt, _tri(tq).T)


def _mem_kv_kernel(mem_ref, gmem_ref, w_ref, gk_ref, mk_ref, mv_ref):
    mn = (_rms(mem_ref[...]) * gmem_ref[...]).astype(BF16)
    kv = _dot(mn, w_ref[...])
    ks = [_rms(kv[:, hd * MEM_DIM:(hd + 1) * MEM_DIM]) for hd in range(H_MEM)]
    mk_ref[...] = jnp.concatenate(ks, axis=-1) * gk_ref[...]
    mv_ref[...] = kv[:, MEM_W:]


def _mem_kv(mem2d, p):
    n = mem2d.shape[0]
    return pl.pallas_call(
        _mem_kv_kernel,
        grid=(1,),
        in_specs=[_full((n, D_MODEL)), _full((1, D_MODEL)), _full((D_MODEL, 2 * MEM_W)), _full((1, MEM_W))],
        out_specs=(_full((n, MEM_W)), _full((n, MEM_W))),
        out_shape=(jax.ShapeDtypeStruct((n, MEM_W), F32), jax.ShapeDtypeStruct((n, MEM_W), F32)),
        compiler_params=_cparams("arbitrary"),
        name="mem_kv",
    )(mem2d, p["g_mem"], p["w_mem"], p["g_mem_k"])


def _mem_heads(mq, mk, mv):
    outs = []
    for hd in range(H_MEM):
        sl = slice(hd * MEM_DIM, (hd + 1) * MEM_DIM)
        s = _dot_nt(mq[:, sl], mk[:, sl])
        pr = jnp.exp(s - jnp.max(s, axis=-1, keepdims=True))
        o = _dot(pr.astype(BF16), mv[:, sl])
        outs.append(o / jnp.sum(pr, axis=-1, keepdims=True))
    return jnp.concatenate(outs, axis=-1)


def _mem_attn_kernel(mq_ref, mk_ref, mv_ref, o_ref):
    o_ref[...] = _mem_heads(mq_ref[...], mk_ref[...].astype(BF16), mv_ref[...].astype(BF16)).astype(BF16)


def _mem_attn(mq, mk, mv):
    rows = mq.shape[0]
    tm = min(TM_MEM, rows)
    n = mk.shape[0]
    return pl.pallas_call(
        _mem_attn_kernel,
        grid=(rows // tm,),
        in_specs=[pl.BlockSpec((tm, MEM_W), lambda i: (i, 0)), _full((n, MEM_W)), _full((n, MEM_W))],
        out_specs=pl.BlockSpec((tm, MEM_W), lambda i: (i, 0)),
        out_shape=jax.ShapeDtypeStruct((rows, MEM_W), BF16),
        compiler_params=_cparams("parallel"),
        name="mem_attn",
    )(mq, mk, mv)


def _sample_attn_kernel(past, layer, q_ref, kn_ref, vn_ref, sq_ref, skn_ref, svn_ref, mq_ref,
                        clat_ref, crope_ref, csk_hbm, csv_hbm, cmk_ref, cmv_ref,
                        wkt_ref, wkn_ref, ind_ref, ones_ref, erow_ref, gkn_ref, gkr_ref, wvh_ref,
                        tri_ref, tris_ref, omla_ref, osb_ref, omem_ref, kbuf, vbuf, sem):
    b = pl.program_id(0)
    ds = q_ref.shape[1]
    hs = H_MLA * ds
    tc = tri_ref.shape[0]
    row = lax.broadcasted_iota(jnp.int32, (hs, 1), 0)
    qpos = past + (row & (ds - 1))
    rows_of = lambda x, hd: x[hd * ds:(hd + 1) * ds]

    def sb_chunk_copies(j):
        rows_j = pl.ds(pl.multiple_of(j * tc, tc), tc)
        cps = []
        for hd in range(H_SB):
            cps.append(pltpu.make_async_copy(csk_hbm.at[layer, b, rows_j, hd, :], kbuf.at[hd], sem.at[0]))
            cps.append(pltpu.make_async_copy(csv_hbm.at[layer, b, rows_j, hd, :], vbuf.at[hd], sem.at[1]))
        return cps

    newest = sb_chunk_copies(past // tc - 1)
    for cp in newest:
        cp.start()

    latb = clat_ref[0, 0].astype(BF16)
    rope = crope_ref[0, 0]
    ropeb = rope.astype(BF16)
    kn = _dot(latb, wkn_ref[...])
    ssq = _dot((kn * kn).astype(BF16), ind_ref[...]) + _dot((rope * rope).astype(BF16), ones_ref[...])
    inv_hi, inv_lo = _split(lax.rsqrt(ssq * (1.0 / QK_DIM) + EPS))
    inv_rows = _dot_nt(erow_ref[...], inv_hi) + _dot_nt(erow_ref[...], inv_lo)
    qabs, qrope, s2 = [], [], []
    for hd in range(H_MLA):
        qh = q_ref[hd].astype(F32)
        qabs.append(_dot((qh[:, :NOPE_DIM] * gkn_ref[...]).astype(BF16), wkt_ref[hd]))
        qrope.append((qh[:, NOPE_DIM:] * gkr_ref[...]).astype(BF16))
        s2.append(_dot_nt(q_ref[hd], kn_ref[hd]))
    s1 = _dot_nt(jnp.concatenate(qabs, axis=0).astype(BF16), latb) + _dot_nt(jnp.concatenate(qrope, axis=0), ropeb)
    s1 = s1 * inv_rows
    s2 = jnp.concatenate(s2, axis=0)
    q_chunk = qpos // CHUNK
    s1 = jnp.where((lax.broadcasted_iota(jnp.int32, s1.shape, 1) // CHUNK) <= q_chunk, s1, NEG_BIG)
    s2 = jnp.where(((past + lax.broadcasted_iota(jnp.int32, s2.shape, 1)) // CHUNK) <= q_chunk, s2, NEG_BIG)
    m = jnp.maximum(jnp.max(s1, axis=-1, keepdims=True), jnp.max(s2, axis=-1, keepdims=True))
    p1 = jnp.exp2(s1 - m)
    p2 = jnp.exp2(s2 - m)
    den = jnp.sum(p1, axis=-1, keepdims=True) + jnp.sum(p2, axis=-1, keepdims=True)
    olat = _dot(p1.astype(BF16), latb)
    for hd in range(H_MLA):
        o = _dot(rows_of(olat, hd).astype(BF16), wvh_ref[hd]) + _dot(rows_of(p2, hd).astype(BF16), vn_ref[hd])
        omla_ref[hd] = (o / rows_of(den, hd)).astype(BF16)

    z2 = jnp.concatenate([_dot_nt(sq_ref[hd], skn_ref[hd]) for hd in range(H_SB)], axis=0)
    before_n = lax.broadcasted_iota(jnp.int32, z2.shape, 1) < (row & (ds - 1))
    l2 = jnp.where(before_n, -_softplus(z2), 0.0)
    a2 = jnp.where(before_n, jnp.exp(z2 + l2 + _dot2(l2, tris_ref[...])), 0.0)
    c = jnp.sum(l2, axis=-1, keepdims=True)
    outs = [_dot(rows_of(a2, hd).astype(BF16), svn_ref[hd]) for hd in range(H_SB)]
    tri = tri_ref[...]

    def cached_chunk(c, outs):
        z = jnp.concatenate([_dot_nt(sq_ref[hd], kbuf[hd].astype(BF16)) for hd in range(H_SB)], axis=0)
        lc = -_softplus(z)
        a = jnp.exp(z + lc + _dot2(lc, tri) + c).astype(BF16)
        outs = [outs[hd] + _dot(rows_of(a, hd), vbuf[hd].astype(BF16)) for hd in range(H_SB)]
        return c + jnp.sum(lc, axis=-1, keepdims=True), outs

    for cp in newest:
        cp.wait()
    c, outs = cached_chunk(c, outs)

    def cond(carry):
        return jnp.logical_and(carry[0] >= 0, carry[1] > SB_UNDERFLOW)

    def body(carry):
        cps = sb_chunk_copies(carry[0])
        for cp in cps:
            cp.start()
        for cp in cps:
            cp.wait()
        c, outs = cached_chunk(carry[2], list(carry[3:]))
        return (carry[0] - 1, jnp.max(c), c) + tuple(outs)

    carry = lax.while_loop(cond, body, (past // tc - 2, jnp.max(c), c) + tuple(outs))
    for hd in range(H_SB):
        osb_ref[hd] = carry[3 + hd].astype(BF16)

    omem_ref[...] = _mem_heads(mq_ref[...], cmk_ref[0, 0].astype(BF16), cmv_ref[0, 0].astype(BF16)).astype(BF16)


def _sample_attn(q, kn, vn, sq, skn, svn, mq, c_lat, c_rope, c_sbk, c_sbv, c_mk, c_mv, layer, p):
    nb, past = c_lat.shape[1], c_lat.shape[2]
    ds = q.shape[1] // nb
    assert ds & (ds - 1) == 0, "row -> query index uses a bit mask"
    n_mem = c_mk.shape[2]
    tc = min(256, past)
    assert past % tc == 0
    hs = H_MLA * ds
    erow = np.zeros((hs, LANES), np.float32)
    erow[np.arange(hs), np.arange(hs) // ds] = 1.0
    hm = lambda d: pl.BlockSpec((H_MLA, ds, d), lambda b: (0, b, 0))
    cache = lambda n, w: pl.BlockSpec((1, 1, n, w), lambda b: (layer, b, 0, 0))
    rows = q.shape[1]
    return pl.pallas_call(
        functools.partial(_sample_attn_kernel, past, layer),
        grid=(nb,),
        in_specs=[hm(QK_DIM), hm(QK_DIM), hm(V_DIM), hm(SB_DIM), hm(SB_DIM), hm(SB_DIM),
                  pl.BlockSpec((ds, MEM_W), lambda b: (b, 0)),
                  cache(past, KV_LORA), cache(past, ROPE_DIM),
                  pl.BlockSpec(memory_space=pl.ANY), pl.BlockSpec(memory_space=pl.ANY),
                  cache(n_mem, MEM_W), cache(n_mem, MEM_W),
                  _full((H_MLA, NOPE_DIM, KV_LORA)), _full((KV_LORA, H_MLA * NOPE_DIM)),
                  _full((H_MLA * NOPE_DIM, LANES)), _full((ROPE_DIM, LANES)), _full((hs, LANES)),
                  _full((1, NOPE_DIM)), _full((1, ROPE_DIM)), _full((H_MLA, KV_LORA, V_DIM)),
                  _full((tc, tc)), _full((ds, ds))],
        out_specs=(hm(V_DIM), hm(SB_DIM), pl.BlockSpec((ds, MEM_W), lambda b: (b, 0))),
        out_shape=(jax.ShapeDtypeStruct((H_MLA, rows, V_DIM), BF16),
                   jax.ShapeDtypeStruct((H_SB, rows, SB_DIM), BF16),
                   jax.ShapeDtypeStruct((rows, MEM_W), BF16)),
        scratch_shapes=[pltpu.VMEM((H_SB, tc, SB_DIM), F32), pltpu.VMEM((H_SB, tc, SB_DIM), F32),
                        pltpu.SemaphoreType.DMA((2,))],
        compiler_params=_cparams("arbitrary"),
        name="sample_attn",
    )(q, kn, vn, sq, skn, svn, mq, c_lat, c_rope, c_sbk, c_sbv, c_mk, c_mv,
      p["w_k_t"], p["w_k_nope"], p["ind64"], p["ones_rope"], jnp.asarray(erow, BF16), p["g_k_nope"], p["g_k_rope"],
      p["w_v_h"], _tri(tc), _tri(ds))


def _merge_kernel(x_ref, gattn_ref, wg_ref, omla_ref, osb_ref, omem_ref, woa_ref, wob_ref, wom_ref, wout_ref,
                  gffn_ref, wrh_ref, wrl_ref, br_ref, x1_ref, idx_ref, gate_ref):
    x = x_ref[...]
    h = (_rms(x) * gattn_ref[...]).astype(BF16)
    g = 1.0 / (1.0 + jnp.exp(-_dot(h, wg_ref[...])))
    ua = _dot(jnp.concatenate([omla_ref[hd] for hd in range(H_MLA)], axis=-1), woa_ref[...])
    ub = _dot(jnp.concatenate([osb_ref[hd] for hd in range(H_SB)], axis=-1), wob_ref[...])
    um = _dot(omem_ref[...], wom_ref[...])
    u = g[:, :D_MODEL] * ua + g[:, D_MODEL:2 * D_MODEL] * ub + g[:, 2 * D_MODEL:] * um
    x1 = x + _dot(u.astype(BF16), wout_ref[...])
    x1_ref[...] = x1
    xh, xl = _split(_rms(x1) * gffn_ref[...])
    lg = _dot(xh, wrh_ref[...]) + _dot(xh, wrl_ref[...]) + _dot(xl, wrh_ref[...]) + br_ref[...]
    lane = lax.broadcasted_iota(jnp.int32, lg.shape, 1).astype(F32)
    vals, ids = [], []
    for _ in range(TOP_K):
        m = jnp.max(lg, axis=-1, keepdims=True)
        sel = jnp.min(jnp.where(lg == m, lane, float(LANES)), axis=-1, keepdims=True)
        vals.append(m)
        ids.append(sel)
        lg = jnp.where(lane == sel, NEG_BIG, lg)
    es = [jnp.exp(v - vals[0]) for v in vals]
    den = es[0] + es[1] + es[2] + es[3]
    idx_o = jnp.zeros(lg.shape, F32)
    gate_o = jnp.zeros(lg.shape, F32)
    for k in range(TOP_K):
        idx_o = jnp.where(lane == float(k), ids[k], idx_o)
        gate_o = jnp.where(lane == float(k), es[k] / den, gate_o)
    idx_ref[...] = idx_o.astype(jnp.int32)
    gate_ref[...] = gate_o


def _merge(x2d, omla, osb, omem, p):
    rows = x2d.shape[0]
    tm = TM_MERGE
    row = lambda w: pl.BlockSpec((tm, w), lambda i: (i, 0))
    hm = lambda d: pl.BlockSpec((H_MLA, tm, d), lambda i: (0, i, 0))
    return pl.pallas_call(
        _merge_kernel,
        grid=(rows // tm,),
        in_specs=[row(D_MODEL), _full((1, D_MODEL)), _full((D_MODEL, 3 * D_MODEL)), hm(V_DIM), hm(SB_DIM),
                  row(MEM_W), _full((MLA_W, D_MODEL)), _full((SB_W, D_MODEL)),
                  _full((MEM_W, D_MODEL)), _full((D_MODEL, D_MODEL)), _full((1, D_MODEL)),
                  _full((D_MODEL, LANES)), _full((D_MODEL, LANES)), _full((1, LANES))],
        out_specs=(row(D_MODEL), row(LANES), row(LANES)),
        out_shape=(jax.ShapeDtypeStruct((rows, D_MODEL), F32), jax.ShapeDtypeStruct((rows, LANES), jnp.int32),
                   jax.ShapeDtypeStruct((rows, LANES), F32)),
        compiler_params=_cparams("parallel"),
        name="merge",
    )(x2d, p["g_attn"], p["w_g"], omla, osb, omem, p["w_o_mla"], p["w_o_sb"], p["w_o_mem"], p["w_out"],
      p["g_ffn"], p["w_r_hi"], p["w_r_lo"], p["b_r"])


def _row_dma(src, src_row, dst, dst_row, sem):
    return pltpu.make_async_copy(src.at[pl.ds(src_row, 1), :], dst.at[pl.ds(dst_row, 1), :], sem)


def _row_dma_wait(src, dst, sem, n):
    def body(r, carry):
        _row_dma(src, 0, dst, 0, sem).wait()
        return carry

    lax.fori_loop(0, n, body, 0, unroll=DMA_UNROLL)


def _dispatch_kernel(dest_ref, x_ref, xs_in, xs_out, sem):
    del xs_in
    n = dest_ref.shape[2]
    for r in range(n):
        _row_dma(x_ref, r // TOP_K, xs_out, dest_ref[0, 0, r], sem.at[0]).start()
    _row_dma_wait(x_ref, xs_out, sem.at[0], n)


def _moe_dispatch(x1, dest, xs):
    rows = x1.shape[0]
    tt = TT_COMBINE
    nt = rows // tt
    return pl.pallas_call(
        _dispatch_kernel,
        grid=(nt,),
        in_specs=[pl.BlockSpec((1, 1, TOP_K * tt), lambda t: (t, 0, 0), memory_space=pltpu.SMEM),
                  pl.BlockSpec((tt, D_MODEL), lambda t: (t, 0)),
                  pl.BlockSpec(memory_space=pl.ANY)],
        out_specs=pl.BlockSpec(memory_space=pl.ANY),
        out_shape=jax.ShapeDtypeStruct(xs.shape, xs.dtype),
        input_output_aliases={2: 0},
        scratch_shapes=[pltpu.SemaphoreType.DMA((1,))],
        compiler_params=_cparams("arbitrary"),
        name="moe_dispatch",
    )(dest.reshape(nt, 1, TOP_K * tt), x1, xs)


def _moe_kernel(be_ref, nact_ref, xs_ref, gffn_ref, wgu_ref, bgu_ref, wd_ref, bd_ref, o_ref, wgu_sc, wd_sc):
    b = pl.program_id(0)

    @pl.when(b >= nact_ref[0])
    def _():
        o_ref[...] = jnp.zeros(o_ref.shape, o_ref.dtype)

    @pl.when(b < nact_ref[0])
    def _():
        changed = jnp.logical_or(b == 0, be_ref[b] != be_ref[jnp.maximum(b - 1, 0)])

        @pl.when(changed)
        def _():
            wgu_sc[...] = wgu_ref[0].astype(BF16)
            wd_sc[...] = wd_ref[0].astype(BF16)

        xb = (_rms(xs_ref[...]) * gffn_ref[...]).astype(BF16)
        gu = _dot(xb, wgu_sc[...]) + bgu_ref[0]
        g = jnp.minimum(gu[:, :D_FF], SWIGLU_LIMIT)
        u = jnp.clip(gu[:, D_FF:], -SWIGLU_LIMIT, SWIGLU_LIMIT)
        hid = (u + 1.0) * (g / (1.0 + jnp.exp(-SWIGLU_ALPHA * g)))
        o_ref[...] = _dot(hid.astype(BF16), wd_sc[...]) + bd_ref[0]


def _moe_experts(xs, block_expert, n_active, p):
    n_blocks = block_expert.shape[0]
    bm = BM_MOE
    blk = lambda b, be, na: (jnp.minimum(b, na[0] - 1), 0)
    grid_spec = pltpu.PrefetchScalarGridSpec(
        num_scalar_prefetch=2,
        grid=(n_blocks,),
        in_specs=[
            pl.BlockSpec((bm, D_MODEL), blk),
            pl.BlockSpec((1, D_MODEL), lambda b, be, na: (0, 0)),
            pl.BlockSpec((1, D_MODEL, 2 * D_FF), lambda b, be, na: (be[b], 0, 0)),
            pl.BlockSpec((1, 1, 2 * D_FF), lambda b, be, na: (be[b], 0, 0)),
            pl.BlockSpec((1, D_FF, D_MODEL), lambda b, be, na: (be[b], 0, 0)),
            pl.BlockSpec((1, 1, D_MODEL), lambda b, be, na: (be[b], 0, 0)),
        ],
        out_specs=pl.BlockSpec((bm, D_MODEL), lambda b, be, na: (b, 0)),
        scratch_shapes=[pltpu.VMEM((D_MODEL, 2 * D_FF), BF16), pltpu.VMEM((D_FF, D_MODEL), BF16)],
    )
    return pl.pallas_call(
        _moe_kernel,
        grid_spec=grid_spec,
        out_shape=jax.ShapeDtypeStruct((n_blocks * bm, D_MODEL), F32),
        compiler_params=_cparams("arbitrary"),
        name="moe_experts",
    )(block_expert, n_active, xs, p["g_ffn"], p["w_gate_up"], p["b_gate_up"], p["w_down"], p["b_down"])


def _combine_gather(y_hbm, pos_ref, dst, sem, n):
    for r in range(n):
        _row_dma(y_hbm, pos_ref[0, 0, r], dst, r, sem).start()


def _combine_kernel(nta, pos_ref, posn_ref, y_hbm, x1a_ref, ga_ref, x1b_ref, gb_ref, oa_ref, ob_ref, buf, sem):
    t = pl.program_id(0)
    nt = pl.num_programs(0)
    tt = oa_ref.shape[0]
    n = TOP_K * tt
    slot = t % 2

    @pl.when(t == 0)
    def _():
        _combine_gather(y_hbm, pos_ref, buf.at[0], sem.at[0], n)

    @pl.when(t + 1 < nt)
    def _():
        _combine_gather(y_hbm, posn_ref, buf.at[1 - slot], sem.at[1 - slot], n)

    _row_dma_wait(y_hbm, buf.at[slot], sem.at[slot], n)

    def combine(x1_ref, gate_ref, o_ref):
        acc = x1_ref[...]
        gate = gate_ref[...]
        for k in range(TOP_K):
            acc = acc + gate[:, k:k + 1] * buf[slot, k * tt:(k + 1) * tt, :]
        o_ref[...] = acc

    @pl.when(t < nta)
    def _():
        combine(x1a_ref, ga_ref, oa_ref)

    @pl.when(t >= nta)
    def _():
        combine(x1b_ref, gb_ref, ob_ref)


def _moe_combine(x1a, gate_a, x1b, gate_b, yb, pos):
    tt = TT_COMBINE
    nta, ntb = x1a.shape[0] // tt, x1b.shape[0] // tt
    nt = nta + ntb
    pos3 = pos.reshape(nt, tt, TOP_K).transpose(0, 2, 1).reshape(nt, 1, TOP_K * tt)
    ia = lambda t: (jnp.minimum(t, nta - 1), 0)
    ib = lambda t: (jnp.maximum(t - nta, 0), 0)
    return pl.pallas_call(
        functools.partial(_combine_kernel, nta),
        grid=(nt,),
        in_specs=[pl.BlockSpec((1, 1, TOP_K * tt), lambda t: (t, 0, 0), memory_space=pltpu.SMEM),
                  pl.BlockSpec((1, 1, TOP_K * tt), lambda t: (jnp.minimum(t + 1, nt - 1), 0, 0),
                               memory_space=pltpu.SMEM),
                  pl.BlockSpec(memory_space=pl.ANY),
                  pl.BlockSpec((tt, D_MODEL), ia), pl.BlockSpec((tt, LANES), ia),
                  pl.BlockSpec((tt, D_MODEL), ib), pl.BlockSpec((tt, LANES), ib)],
        out_specs=(pl.BlockSpec((tt, D_MODEL), ia), pl.BlockSpec((tt, D_MODEL), ib)),
        out_shape=(jax.ShapeDtypeStruct(x1a.shape, F32), jax.ShapeDtypeStruct(x1b.shape, F32)),
        scratch_shapes=[pltpu.VMEM((2, TOP_K * tt, D_MODEL), F32), pltpu.SemaphoreType.DMA((2,))],
        compiler_params=_cparams("arbitrary"),
        name="moe_combine",
    )(pos3, pos3, yb, x1a, gate_a, x1b, gate_b)


def _moe_route(idx):
    rows = idx.shape[0]
    n = rows * TOP_K
    bm = BM_MOE
    e = idx.reshape(n)
    onehot = (e[:, None] == jnp.arange(N_EXPERTS, dtype=jnp.int32)[None, :]).astype(jnp.int32)
    csum = jnp.cumsum(onehot, axis=0)
    rank = jnp.sum(csum * onehot, axis=1) - 1
    counts = csum[-1]
    padded = (counts + bm - 1) // bm * bm
    pend = jnp.cumsum(padded)
    dest = ((pend - padded)[e] + rank).astype(jnp.int32).reshape(rows, TOP_K)
    n_blocks = -(-n // bm) + N_EXPERTS
    starts = jnp.arange(n_blocks, dtype=jnp.int32) * bm
    block_expert = jnp.minimum(
        jnp.sum((pend[None, :] <= starts[:, None]).astype(jnp.int32), axis=1), N_EXPERTS - 1).astype(jnp.int32)
    n_active = (pend[-1:] // bm).astype(jnp.int32)
    return dest, block_expert, n_active


def _pack_layer(l, g_attn, w_in, g_q_lat, w_q_b, g_q_nope, g_q_rope, g_kv_lat, w_kv_b, g_k_nope, g_k_rope, g_mem,
                w_mem_kv, g_mem_q, g_mem_k, w_o_mla, w_o_sb, w_o_mem, w_out, g_ffn, w_router, b_router,
                w_gate_up, b_gate_up, w_down, b_down):
    w = w_in[l]
    off = np.cumsum((Q_LORA, KV_LORA, ROPE_DIM, SB_W, SB_W, SB_W, MEM_W))
    o_kpe, o_sbq = int(off[1]), int(off[2])
    o_gate = int(off[6])
    kpe1 = w[:, o_kpe:o_kpe + ROPE_HALF]
    kpe2 = w[:, o_kpe + ROPE_HALF:o_kpe + ROPE_DIM]
    zl = jnp.zeros((D_MODEL, KPE_LANE), F32)
    zr = jnp.zeros((D_MODEL, LANES - KPE_LANE - ROPE_DIM), F32)
    w_a = jnp.concatenate([w[:, :o_kpe], zl, kpe1, kpe2, zr, zl, -kpe2, kpe1, zr, w[:, o_sbq:o_gate]], axis=1)
    wq = w_q_b[l]
    z_n = jnp.zeros((Q_LORA, H_MLA, NOPE_DIM), F32)
    wq_sw = jnp.concatenate([z_n, -wq[..., NOPE_DIM + ROPE_HALF:], wq[..., NOPE_DIM:NOPE_DIM + ROPE_HALF]], axis=-1)
    lane = np.arange(QK_W)
    ind = np.zeros((QK_W, LANES), np.float32)
    ind[lane, lane // QK_DIM] = 1.0
    ind64 = np.zeros((H_MLA * NOPE_DIM, LANES), np.float32)
    ind64[np.arange(H_MLA * NOPE_DIM), np.arange(H_MLA * NOPE_DIM) // NOPE_DIM] = 1.0
    pk = np.zeros((ROPE_DIM, QK_W), np.float32)
    for hd in range(H_MLA):
        pk[np.arange(ROPE_DIM), hd * QK_DIM + NOPE_DIM + np.arange(ROPE_DIM)] = 1.0
    wkv = w_kv_b[l]
    w_k = jnp.concatenate([wkv[..., :NOPE_DIM], jnp.zeros((KV_LORA, H_MLA, ROPE_DIM), F32)], axis=-1)
    g_q = jnp.tile(jnp.concatenate([g_q_nope[l], g_q_rope[l], g_q_rope[l]]), H_MLA) * (QK_DIM ** -0.5 * LOG2E)
    g_k = jnp.tile(jnp.concatenate([g_k_nope[l], g_k_rope[l], g_k_rope[l]]), H_MLA)
    wm = w_mem_kv[l]
    w_mem = jnp.concatenate([wm[..., :MEM_DIM].reshape(D_MODEL, MEM_W), wm[..., MEM_DIM:].reshape(D_MODEL, MEM_W)], 1)
    w_r = jnp.concatenate([w_router[l], jnp.zeros((D_MODEL, LANES - N_EXPERTS), F32)], axis=1)
    w_r_hi = w_r.astype(BF16)
    b_r = jnp.concatenate([b_router[l].astype(F32), jnp.full((LANES - N_EXPERTS,), NEG_BIG, F32)])
    return {
        "g_attn": g_attn[l][None], "w_a": w_a.astype(BF16), "g_q_lat": g_q_lat[l][None],
        "w_q": wq.reshape(Q_LORA, QK_W).astype(BF16), "w_q_sw": wq_sw.reshape(Q_LORA, QK_W).astype(BF16),
        "ind96": jnp.asarray(ind, BF16), "ind96_t": jnp.asarray(ind.T, BF16), "g_q": g_q[None],
        "g_kv_lat": g_kv_lat[l][None], "g_mem_q": jnp.tile(g_mem_q[l], H_MEM)[None] * (MEM_DIM ** -0.5),
        "w_k": w_k.reshape(KV_LORA, QK_W).astype(BF16), "p_kpe": jnp.asarray(pk, BF16), "g_k": g_k[None],
        "w_v": wkv[..., NOPE_DIM:].reshape(KV_LORA, MLA_W).astype(BF16),
        "w_v_t": wkv[..., NOPE_DIM:].reshape(KV_LORA, MLA_W).T.astype(BF16),
        "w_v_h": wkv[..., NOPE_DIM:].transpose(1, 0, 2).astype(BF16),
        "w_k_t": wkv[..., :NOPE_DIM].transpose(1, 2, 0).astype(BF16),
        "w_k_nope": wkv[..., :NOPE_DIM].reshape(KV_LORA, H_MLA * NOPE_DIM).astype(BF16),
        "ind64": jnp.asarray(ind64, BF16), "ones_rope": jnp.ones((ROPE_DIM, LANES), BF16),
        "g_k_nope": g_k_nope[l][None], "g_k_rope": jnp.concatenate([g_k_rope[l], g_k_rope[l]])[None],
        "g_mem": g_mem[l][None], "w_mem": w_mem.astype(BF16), "g_mem_k": jnp.tile(g_mem_k[l], H_MEM)[None],
        "w_g": w[:, o_gate:].astype(BF16), "w_sbv_t": w[:, o_gate - MEM_W - SB_W:o_gate - MEM_W].T.astype(BF16),
        "w_o_mla": w_o_mla[l].astype(BF16), "w_o_sb": w_o_sb[l].astype(BF16),
        "w_o_mem": w_o_mem[l].astype(BF16), "w_out": w_out[l].astype(BF16), "g_ffn": g_ffn[l][None],
        "w_r_hi": w_r_hi, "w_r_lo": (w_r - w_r_hi.astype(F32)).astype(BF16), "b_r": b_r[None],
        "w_gate_up": w_gate_up[l], "b_gate_up": b_gate_up[l][:, None, :], "w_down": w_down[l],
        "b_down": b_down[l][:, None, :],
    }


def kernel(x_prompt, x_sample, mem_prompt, cache_mla_latent, cache_mla_rope, cache_sb_k, cache_sb_v, cache_mem_k, cache_mem_v, g_attn, w_in, g_q_lat, w_q_b, g_q_nope, g_q_rope, g_kv_lat, w_kv_b, g_k_nope, g_k_rope, g_mem, w_mem_kv, g_mem_q, g_mem_k, w_o_mla, w_o_sb, w_o_mem, w_out, g_ffn, w_router, b_router, w_gate_up, b_gate_up, w_down, b_down):
    depth = g_attn.shape[0]
    bp, sp, _ = x_prompt.shape
    bs, ss, _ = x_sample.shape
    past = cache_mla_latent.shape[2]
    n_mem = mem_prompt.shape[1]
    assert bp == 1 and sp % TQ_MLA == 0 and sp % TM_KV == 0 and (bs * ss) % TM_PROJ == 0 and TM_PROJ % ss == 0
    rows_s = bs * ss
    c_sbk, c_sbv = cache_sb_k, cache_sb_v
    c_mk = cache_mem_k.reshape(depth, bs, n_mem, MEM_W)
    c_mv = cache_mem_v.reshape(depth, bs, n_mem, MEM_W)
    xp = x_prompt.reshape(sp, D_MODEL)
    xs = x_sample.reshape(rows_s, D_MODEL)
    mem2d = mem_prompt.reshape(n_mem, D_MODEL)
    base_p, off_p = np.arange(sp // TM_PROJ) * TM_PROJ, np.arange(TM_PROJ)
    base_s, off_s = np.full((rows_s // TM_PROJ,), past), np.arange(TM_PROJ) % ss
    outs = [[] for _ in range(10)]
    for l in range(depth):
        p = _pack_layer(l, g_attn, w_in, g_q_lat, w_q_b, g_q_nope, g_q_rope, g_kv_lat, w_kv_b, g_k_nope, g_k_rope,
                        g_mem, w_mem_kv, g_mem_q, g_mem_k, w_o_mla, w_o_sb, w_o_mem, w_out, g_ffn, w_router,
                        b_router, w_gate_up, b_gate_up, w_down, b_down)
        q, lat, kpe, sq, sk, sv, skh, _, mq, svt = _proj(xp, base_p, off_p, p)
        kh, _, vth = _kv_expand(lat, kpe, p)
        o_mla = _mla_attn(q, kh, vth)
        o_sb = _sb_attn(sq, skh, svt)
        mk, mv = _mem_kv(mem2d, p)
        o_mem = _mem_attn(mq, mk, mv)
        x1p, idx_p, gate_p = _merge(xp, o_mla, o_sb, o_mem, p)
        q_s, lat_s, kpe_s, sq_s, sk_s, sv_s, skh_s, svh_s, mq_s, _ = _proj(xs, base_s, off_s, p)
        kn, vn, _ = _kv_expand(lat_s, kpe_s, p)
        o_mla_s, o_sb_s, o_mem_s = _sample_attn(q_s, kn, vn, sq_s, skh_s, svh_s, mq_s, cache_mla_latent,
                                                cache_mla_rope, c_sbk, c_sbv, c_mk, c_mv, l, p)
        x1s, idx_s, gate_s = _merge(xs, o_mla_s, o_sb_s, o_mem_s, p)
        dest, block_expert, n_active = _moe_route(jnp.concatenate([idx_p[:, :TOP_K], idx_s[:, :TOP_K]], axis=0))
        slots = jnp.zeros((block_expert.shape[0] * BM_MOE, D_MODEL), F32)
        slots = _moe_dispatch(x1s, dest[sp:], _moe_dispatch(x1p, dest[:sp], slots))
        yb = _moe_experts(slots, block_expert, n_active, p)
        xp, xs = _moe_combine(x1p, gate_p, x1s, gate_s, yb, dest)
        for lst, val in zip(outs, (lat.reshape(bp, sp, KV_LORA), kpe.reshape(bp, sp, ROPE_DIM),
                                   sk.reshape(bp, sp, H_SB, SB_DIM), sv.reshape(bp, sp, H_SB, SB_DIM),
                                   mk.reshape(bp, n_mem, H_MEM, MEM_DIM), mv.reshape(bp, n_mem, H_MEM, MEM_DIM),
                                   lat_s.reshape(bs, ss, KV_LORA), kpe_s.reshape(bs, ss, ROPE_DIM),
                                   sk_s.reshape(bs, ss, H_SB, SB_DIM), sv_s.reshape(bs, ss, H_SB, SB_DIM))):
            lst.append(val)
    return (xp.reshape(bp, sp, D_MODEL), xs.reshape(bs, ss, D_MODEL)) + tuple(jnp.stack(o) for o in outs)
```

```python
import functools

import numpy as np
import jax
import jax.numpy as jnp
from jax import lax
from jax.experimental import pallas as pl
from jax.experimental.pallas import tpu as pltpu

F32 = jnp.float32
BF16 = jnp.bfloat16

D_MODEL = 1024
CHUNK = 64
EPS = 1e-6
H_MLA = 8
NOPE_DIM = 64
ROPE_DIM = 32
ROPE_HALF = ROPE_DIM // 2
V_DIM = 64
Q_LORA = 384
KV_LORA = 256
ROPE_BASE = 10000.0
QK_DIM = NOPE_DIM + ROPE_DIM
QK_W = H_MLA * QK_DIM
H_SB = 8
SB_DIM = 64
H_MEM = 4
MEM_DIM = 128
N_EXPERTS = 32
TOP_K = 4
D_FF = 1024
SWIGLU_LIMIT = 7.0
SWIGLU_ALPHA = 1.702
MLA_W = H_MLA * V_DIM
SB_W = H_SB * SB_DIM
MEM_W = H_MEM * MEM_DIM

LANES = 128
VT_ROWS = LANES
MLA_CHAINS = 4
MLA_KSPLIT = 2
assert MLA_KSPLIT % 2 == 0
LOG2E = 1.4426950408889634
A_QLAT = 0
A_KVLAT = A_QLAT + Q_LORA
A_KPE = A_KVLAT + KV_LORA
A_KPE_SW = A_KPE + LANES
A_SBQ = A_KPE_SW + LANES
A_SBK = A_SBQ + SB_W
A_SBV = A_SBK + SB_W
A_MEMQ = A_SBV + SB_W
A_COLS = A_MEMQ + MEM_W
KPE_LANE = NOPE_DIM

TM_PROJ = 256
TM_KV = 512
TQ_MLA = 1024
TQ_SB = 256
SB_HEADS = 2
TM_MEM = 512
TM_MERGE = 256
BM_MOE = 512
TT_COMBINE = 256
DMA_UNROLL = 8
DMA_THREADS = 2
TOP_K_SHIFT = TOP_K.bit_length() - 1
assert 1 << TOP_K_SHIFT == TOP_K
VMEM_LIMIT = 56 * 1024 * 1024

SB_UNDERFLOW = -120.0
NEG_BIG = -3.0e38


def _cparams(*sem):
    return pltpu.CompilerParams(dimension_semantics=sem, vmem_limit_bytes=VMEM_LIMIT)


def _split(x):
    hi = x.astype(BF16)
    lo = (x - hi.astype(F32)).astype(BF16)
    return hi, lo


def _dot(a, b):
    return jnp.dot(a, b, preferred_element_type=F32)


def _dot2(x, m):
    hi, lo = _split(x)
    return _dot(hi, m) + _dot(lo, m)


def _dot_nt(a, b):
    return lax.dot_general(a, b, (((1,), (1,)), ((), ())), preferred_element_type=F32)


def _rms(x):
    return x * lax.rsqrt(jnp.mean(x * x, axis=-1, keepdims=True) + EPS)


def _softplus(z):
    return jnp.maximum(z, 0.0) + jnp.log(1.0 + jnp.exp(-jnp.abs(z)))


def _full(shape):
    n = len(shape)
    return pl.BlockSpec(shape, lambda *_: (0,) * n)


def _proj_kernel(x_ref, gattn_ref, wa_ref, gqlat_ref, wq_ref, wqs_ref, ind_ref, indt_ref, gq_ref, gkv_ref,
                 bcs_ref, ocos_ref, osin_ref, gmq_ref, wsvt_ref,
                 q_ref, lat_ref, kpe_ref, sq_ref, sk_ref, sv_ref, skh_ref, svh_ref, mq_ref, svt_ref):
    h = (_rms(x_ref[...]) * gattn_ref[...]).astype(BF16)
    z = _dot(h, wa_ref[...])
    svt = _dot_nt(wsvt_ref[...], h)
    pad = jnp.zeros((VT_ROWS - SB_DIM, svt.shape[1]), F32)
    for hd in range(H_SB):
        svt_ref[hd] = jnp.concatenate([svt[hd * SB_DIM:(hd + 1) * SB_DIM, :], pad], axis=0).astype(BF16)
    bc = bcs_ref[0, 0:1, :]
    bs = bcs_ref[0, 1:2, :]
    oc = ocos_ref[...]
    osn = osin_ref[...]
    cos_f = bc * oc - bs * osn
    sin_f = bs * oc + bc * osn
    qn = (_rms(z[:, A_QLAT:A_QLAT + Q_LORA]) * gqlat_ref[...]).astype(BF16)
    qr = _dot(qn, wq_ref[...]) * cos_f + _dot(qn, wqs_ref[...]) * sin_f
    ssq = _dot2(qr * qr, ind_ref[...])
    inv = lax.rsqrt(ssq * (1.0 / QK_DIM) + EPS)
    qo = qr * _dot2(inv, indt_ref[...]) * gq_ref[...]
    for hd in range(H_MLA):
        q_ref[hd] = qo[:, hd * QK_DIM:(hd + 1) * QK_DIM].astype(BF16)
    lat_ref[...] = _rms(z[:, A_KVLAT:A_KVLAT + KV_LORA]) * gkv_ref[...]
    kr = z[:, A_KPE:A_KPE + LANES] * cos_f[:, :LANES] + z[:, A_KPE_SW:A_KPE_SW + LANES] * sin_f[:, :LANES]
    kpe_ref[...] = kr[:, KPE_LANE:KPE_LANE + ROPE_DIM]
    sbq = z[:, A_SBQ:A_SBQ + SB_W] * (SB_DIM ** -0.5)
    sbk = z[:, A_SBK:A_SBK + SB_W]
    sbv = z[:, A_SBV:A_SBV + SB_W]
    sk_ref[...] = sbk
    sv_ref[...] = sbv
    for hd in range(H_SB):
        sl = slice(hd * SB_DIM, (hd + 1) * SB_DIM)
        sq_ref[hd] = sbq[:, sl].astype(BF16)
        skh_ref[hd] = sbk[:, sl].astype(BF16)
        svh_ref[hd] = sbv[:, sl].astype(BF16)
    mqs = []
    for hd in range(H_MEM):
        mqs.append(_rms(z[:, A_MEMQ + hd * MEM_DIM:A_MEMQ + (hd + 1) * MEM_DIM]))
    mq_ref[...] = (jnp.concatenate(mqs, axis=-1) * gmq_ref[...]).astype(BF16)


def _rope_tables(base_pos, off_pos):
    lane = np.arange(QK_W) % QK_DIM
    inv_freq = ROPE_BASE ** (-np.arange(ROPE_HALF, dtype=np.float64) / ROPE_HALF)
    freq = np.where(lane >= NOPE_DIM, inv_freq[(lane - NOPE_DIM) % ROPE_HALF], 0.0)
    ab = np.asarray(base_pos, np.float64)[:, None] * freq
    ao = np.asarray(off_pos, np.float64)[:, None] * freq
    bcs = np.stack([np.cos(ab), np.sin(ab)], axis=1).astype(np.float32)
    return jnp.asarray(bcs), jnp.asarray(np.cos(ao), F32), jnp.asarray(np.sin(ao), F32)


def _proj(x2d, base_pos, off_pos, p):
    rows = x2d.shape[0]
    tm = TM_PROJ
    nt = rows // tm
    bcs, ocos, osin = _rope_tables(base_pos, off_pos)
    row = lambda w: pl.BlockSpec((tm, w), lambda i: (i, 0))
    hm = lambda d: pl.BlockSpec((H_MLA, tm, d), lambda i: (0, i, 0))
    out_shape = (
        jax.ShapeDtypeStruct((H_MLA, rows, QK_DIM), BF16),
        jax.ShapeDtypeStruct((rows, KV_LORA), F32),
        jax.ShapeDtypeStruct((rows, ROPE_DIM), F32),
        jax.ShapeDtypeStruct((H_SB, rows, SB_DIM), BF16),
        jax.ShapeDtypeStruct((rows, SB_W), F32),
        jax.ShapeDtypeStruct((rows, SB_W), F32),
        jax.ShapeDtypeStruct((H_SB, rows, SB_DIM), BF16),
        jax.ShapeDtypeStruct((H_SB, rows, SB_DIM), BF16),
        jax.ShapeDtypeStruct((rows, MEM_W), BF16),
        jax.ShapeDtypeStruct((H_SB, VT_ROWS, rows), BF16),
    )
    return pl.pallas_call(
        _proj_kernel,
        grid=(nt,),
        in_specs=[row(D_MODEL), _full((1, D_MODEL)), _full((D_MODEL, A_COLS)), _full((1, Q_LORA)),
                  _full((Q_LORA, QK_W)), _full((Q_LORA, QK_W)), _full((QK_W, LANES)), _full((LANES, QK_W)),
                  _full((1, QK_W)), _full((1, KV_LORA)),
                  pl.BlockSpec((1, 2, QK_W), lambda i: (i, 0, 0)), _full((tm, QK_W)), _full((tm, QK_W)),
                  _full((1, MEM_W)), _full((SB_W, D_MODEL))],
        out_specs=(hm(QK_DIM), row(KV_LORA), row(ROPE_DIM), hm(SB_DIM), row(SB_W), row(SB_W), hm(SB_DIM),
                   hm(SB_DIM), row(MEM_W), pl.BlockSpec((H_SB, VT_ROWS, tm), lambda i: (0, 0, i))),
        out_shape=out_shape,
        compiler_params=_cparams("parallel"),
        name="proj",
    )(x2d, p["g_attn"], p["w_a"], p["g_q_lat"], p["w_q"], p["w_q_sw"], p["ind96"], p["ind96_t"], p["g_q"],
      p["g_kv_lat"], bcs, ocos, osin, p["g_mem_q"], p["w_sbv_t"])


def _expand_keys(lat, kpe, wk, pk, ind, indt, gk):
    kf = _dot(lat.astype(BF16), wk) + _dot2(kpe, pk)
    inv = lax.rsqrt(_dot2(kf * kf, ind) * (1.0 / QK_DIM) + EPS)
    return kf * _dot2(inv, indt) * gk


def _kv_expand_kernel(lat_ref, kpe_ref, wk_ref, pk_ref, ind_ref, indt_ref, gk_ref, wv_ref, wvt_ref,
                      k_ref, v_ref, vt_ref):
    lat = lat_ref[...]
    latb = lat.astype(BF16)
    ko = _expand_keys(lat, kpe_ref[...], wk_ref[...], pk_ref[...], ind_ref[...], indt_ref[...], gk_ref[...])
    v = _dot(latb, wv_ref[...])
    vt = _dot_nt(wvt_ref[...], latb)
    ones_row = (lax.broadcasted_iota(jnp.int32, (VT_ROWS - V_DIM, vt.shape[1]), 0) == 0).astype(F32)
    for hd in range(H_MLA):
        k_ref[hd] = ko[:, hd * QK_DIM:(hd + 1) * QK_DIM].astype(BF16)
        v_ref[hd] = v[:, hd * V_DIM:(hd + 1) * V_DIM].astype(BF16)
        vt_ref[hd] = jnp.concatenate([vt[hd * V_DIM:(hd + 1) * V_DIM, :], ones_row], axis=0).astype(BF16)


def _kv_expand(lat, kpe, p):
    rows = lat.shape[0]
    tm = min(TM_KV, rows)
    row = lambda w: pl.BlockSpec((tm, w), lambda i: (i, 0))
    hm = lambda d: pl.BlockSpec((H_MLA, tm, d), lambda i: (0, i, 0))
    return pl.pallas_call(
        _kv_expand_kernel,
        grid=(rows // tm,),
        in_specs=[row(KV_LORA), row(ROPE_DIM), _full((KV_LORA, QK_W)), _full((ROPE_DIM, QK_W)),
                  _full((QK_W, LANES)), _full((LANES, QK_W)), _full((1, QK_W)), _full((KV_LORA, MLA_W)),
                  _full((MLA_W, KV_LORA))],
        out_specs=(hm(QK_DIM), hm(V_DIM), pl.BlockSpec((H_MLA, VT_ROWS, tm), lambda i: (0, 0, i))),
        out_shape=(jax.ShapeDtypeStruct((H_MLA, rows, QK_DIM), BF16),
                   jax.ShapeDtypeStruct((H_MLA, rows, V_DIM), BF16),
                   jax.ShapeDtypeStruct((H_MLA, VT_ROWS, rows), BF16)),
        compiler_params=_cparams("parallel"),
        name="kv_expand",
    )(lat, kpe, p["w_k"], p["p_kpe"], p["ind96"], p["ind96_t"], p["g_k"], p["w_v"], p["w_v_t"])


def _mla_update(st, vt, m, acc):
    m_new = jnp.maximum(m, jnp.max(st, axis=0, keepdims=True))
    pr = jnp.exp2(st - m_new).astype(BF16)
    return m_new, acc * jnp.exp2(m - m_new) + _dot(vt, pr)


def _mla_step(qc, k, vt, m, acc, mask):
    st = _dot_nt(k, qc)
    if mask is not None:
        st = jnp.where(mask, st, NEG_BIG)
    return _mla_update(st, vt, m, acc)


def _mla_attn_kernel(q_ref, k_ref, vt_ref, o_ref, s_sc, p_sc):
    i = pl.program_id(1)
    tq = q_ref.shape[1]
    nc = MLA_CHAINS
    tc = tq // nc
    tk = tq // MLA_KSPLIT
    qs = [q_ref[0, c * tc:(c + 1) * tc, :] for c in range(nc)]
    key_chunk = lax.broadcasted_iota(jnp.int32, (tk, tc), 0) // CHUNK
    qry_chunk = lax.broadcasted_iota(jnp.int32, (tk, tc), 1) // CHUNK

    def scores(t, slot):
        k = k_ref[0, pl.ds(pl.multiple_of(t * tk, tk), tk), :]
        out = []
        for c in range(nc):
            st = _dot_nt(k, qs[c])
            s_sc[slot, c] = st
            out.append(jnp.max(st, axis=0, keepdims=True))
        return out

    def accumulate(t, slot, alphas, accs):
        vt = vt_ref[0, :, pl.ds(pl.multiple_of(t * tk, tk), tk)]
        return [accs[c] * alphas[c] + _dot(vt, p_sc[slot, c]) for c in range(nc)]

    def step(t, slot, cmax, alphas, ms, accs, last, mask_u):
        nxt = None if last else scores(t + 1, 1 - slot)
        accs = accumulate(jnp.maximum(t - 1, 0), 1 - slot, alphas, accs)
        m_new, alphas = [], []
        for c in range(nc):
            st = s_sc[slot, c]
            if mask_u is None:
                cm = cmax[c]
            else:
                st = jnp.where(key_chunk + mask_u * (tk // CHUNK) <= qry_chunk + c * (tc // CHUNK), st, NEG_BIG)
                cm = jnp.max(st, axis=0, keepdims=True)
            mn = jnp.maximum(ms[c], cm)
            p_sc[slot, c] = jnp.exp2(st - mn).astype(BF16)
            alphas.append(jnp.exp2(ms[c] - mn))
            m_new.append(mn)
        return nxt, alphas, m_new, accs

    p_sc[1] = jnp.zeros(p_sc.shape[1:], BF16)
    init = (tuple(scores(0, 0)) + (jnp.ones((1, tc), F32),) * nc + (jnp.full((1, tc), NEG_BIG, F32),) * nc
            + (jnp.zeros((VT_ROWS, tc), F32),) * nc)
    unpack = lambda carry: [list(carry[g * nc:(g + 1) * nc]) for g in range(4)]

    def body(j, carry):
        cmax, alphas, ms, accs = unpack(carry)
        for u in range(MLA_KSPLIT):
            cmax, alphas, ms, accs = step(j * MLA_KSPLIT + u, u % 2, cmax, alphas, ms, accs, False, None)
        return tuple(cmax) + tuple(alphas) + tuple(ms) + tuple(accs)

    cmax, alphas, ms, accs = unpack(lax.fori_loop(0, i, body, init))
    for u in range(MLA_KSPLIT):
        cmax, alphas, ms, accs = step(i * MLA_KSPLIT + u, u % 2, cmax, alphas, ms, accs, u + 1 == MLA_KSPLIT, u)
    accs = accumulate((i + 1) * MLA_KSPLIT - 1, (MLA_KSPLIT - 1) % 2, alphas, accs)
    for c in range(nc):
        ot = accs[c].T
        o_ref[0, c * tc:(c + 1) * tc, :] = (ot[:, :V_DIM] / ot[:, V_DIM:V_DIM + 1]).astype(BF16)


def _mla_attn(q, k, vt):
    rows = q.shape[1]
    tq = min(TQ_MLA, rows)
    return pl.pallas_call(
        _mla_attn_kernel,
        grid=(H_MLA, rows // tq),
        in_specs=[pl.BlockSpec((1, tq, QK_DIM), lambda h, i: (h, i, 0)),
                  pl.BlockSpec((1, rows, QK_DIM), lambda h, i: (h, 0, 0)),
                  pl.BlockSpec((1, VT_ROWS, rows), lambda h, i: (h, 0, 0))],
        out_specs=pl.BlockSpec((1, tq, V_DIM), lambda h, i: (h, i, 0)),
        out_shape=jax.ShapeDtypeStruct((H_MLA, rows, V_DIM), BF16),
        scratch_shapes=[pltpu.VMEM((2, MLA_CHAINS, tq // MLA_KSPLIT, tq // MLA_CHAINS), F32),
                        pltpu.VMEM((2, MLA_CHAINS, tq // MLA_KSPLIT, tq // MLA_CHAINS), BF16)],
        compiler_params=_cparams("parallel", "arbitrary"),
        name="mla_attn",
    )(q, k, vt)


def _sb_attn_kernel(q_ref, k_ref, vt_ref, triu_ref, o_ref):
    i = pl.program_id(1)
    nh, tq = q_ref.shape[0], q_ref.shape[1]
    triu = triu_ref[...]
    qs = [q_ref[c] for c in range(nh)]
    before = lax.broadcasted_iota(jnp.int32, (tq, tq), 0) < lax.broadcasted_iota(jnp.int32, (tq, tq), 1)

    def process(tiles, cs, accs):
        starts = [pl.multiple_of(j * tq, tq) for j, _, _ in tiles]
        zs, lks, bts = {}, {}, {}
        for t in range(len(tiles)):
            for c in range(nh):
                zs[t, c] = _dot_nt(k_ref[c, pl.ds(starts[t], tq), :], qs[c])
        for t, (_, diag, live) in enumerate(tiles):
            for c in range(nh):
                lk = -_softplus(zs[t, c])
                if diag:
                    lk = jnp.where(before, lk, 0.0)
                if live is not None:
                    lk = jnp.where(live, lk, 0.0)
                lks[t, c] = lk
        for key, lk in lks.items():
            hi, lo = _split(lk)
            bts[key] = _dot(triu, hi) + _dot(triu, lo)
        cs, accs = list(cs), list(accs)
        for t, (_, diag, live) in enumerate(tiles):
            for c in range(nh):
                a = jnp.exp(zs[t, c] + lks[t, c] + bts[t, c] + cs[c])
                if diag:
                    a = jnp.where(before, a, 0.0)
                if live is not None:
                    a = jnp.where(live, a, 0.0)
                accs[c] = accs[c] + _dot(vt_ref[c, :, pl.ds(starts[t], tq)], a.astype(BF16))
                cs[c] = cs[c] + jnp.sum(lks[t, c], axis=0, keepdims=True)
        return cs, accs

    def cmax(cs):
        out = jnp.max(cs[0])
        for c in cs[1:]:
            out = jnp.maximum(out, jnp.max(c))
        return out

    cs = [jnp.zeros((1, tq), F32)] * nh
    accs = [jnp.zeros((VT_ROWS, tq), F32)] * nh
    cs, accs = process([(i, True, None), (jnp.maximum(i - 1, 0), False, i > 0)], cs, accs)

    def cond(carry):
        return jnp.logical_and(carry[0] >= 0, carry[1] > SB_UNDERFLOW)

    def body(carry):
        j = carry[0]
        cs, accs = process([(j, False, None)], carry[2:2 + nh], carry[2 + nh:])
        return (j - 1, cmax(cs)) + tuple(cs) + tuple(accs)

    carry = lax.while_loop(cond, body, (i - 2, cmax(cs)) + tuple(cs) + tuple(accs))
    for c in range(nh):
        o_ref[c] = carry[2 + nh + c].T[:, :SB_DIM].astype(BF16)


def _tri(n):
    return jnp.asarray(np.tril(np.ones((n, n), np.float32), -1), BF16)


def _sb_attn(q, k, vt):
    rows = q.shape[1]
    tq = min(TQ_SB, rows)
    nh = SB_HEADS
    return pl.pallas_call(
        _sb_attn_kernel,
        grid=(H_SB // nh, rows // tq),
        in_specs=[pl.BlockSpec((nh, tq, SB_DIM), lambda h, i: (h, i, 0)),
                  pl.BlockSpec((nh, rows, SB_DIM), lambda h, i: (h, 0, 0)),
                  pl.BlockSpec((nh, VT_ROWS, rows), lambda h, i: (h, 0, 0)),
                  _full((tq, tq))],
        out_specs=pl.BlockSpec((nh, tq, SB_DIM), lambda h, i: (h, i, 0)),
        out_shape=jax.ShapeDtypeStruct((H_SB, rows, SB_DIM), BF16),
        compiler_params=_cparams("parallel", "arbitrary"),
        name="sb_attn",
    )(q, k, vt, _tri(tq).T)


def _mem_kv_kernel(mem_ref, gmem_ref, w_ref, gk_ref, mk_ref, mv_ref):
    mn = (_rms(mem_ref[...]) * gmem_ref[...]).astype(BF16)
    kv = _dot(mn, w_ref[...])
    ks = [_rms(kv[:, hd * MEM_DIM:(hd + 1) * MEM_DIM]) for hd in range(H_MEM)]
    mk_ref[...] = jnp.concatenate(ks, axis=-1) * gk_ref[...]
    mv_ref[...] = kv[:, MEM_W:]


def _mem_kv(mem2d, p):
    n = mem2d.shape[0]
    return pl.pallas_call(
        _mem_kv_kernel,
        grid=(1,),
        in_specs=[_full((n, D_MODEL)), _full((1, D_MODEL)), _full((D_MODEL, 2 * MEM_W)), _full((1, MEM_W))],
        out_specs=(_full((n, MEM_W)), _full((n, MEM_W))),
        out_shape=(jax.ShapeDtypeStruct((n, MEM_W), F32), jax.ShapeDtypeStruct((n, MEM_W), F32)),
        compiler_params=_cparams("arbitrary"),
        name="mem_kv",
    )(mem2d, p["g_mem"], p["w_mem"], p["g_mem_k"])


def _mem_heads(mq, mk, mv):
    outs = []
    for hd in range(H_MEM):
        sl = slice(hd * MEM_DIM, (hd + 1) * MEM_DIM)
        s = _dot_nt(mq[:, sl], mk[:, sl])
        pr = jnp.exp(s - jnp.max(s, axis=-1, keepdims=True))
        o = _dot(pr.astype(BF16), mv[:, sl])
        outs.append(o / jnp.sum(pr, axis=-1, keepdims=True))
    return jnp.concatenate(outs, axis=-1)


def _mem_attn_kernel(mq_ref, mk_ref, mv_ref, o_ref):
    o_ref[...] = _mem_heads(mq_ref[...], mk_ref[...].astype(BF16), mv_ref[...].astype(BF16)).astype(BF16)


def _mem_attn(mq, mk, mv):
    rows = mq.shape[0]
    tm = min(TM_MEM, rows)
    n = mk.shape[0]
    return pl.pallas_call(
        _mem_attn_kernel,
        grid=(rows // tm,),
        in_specs=[pl.BlockSpec((tm, MEM_W), lambda i: (i, 0)), _full((n, MEM_W)), _full((n, MEM_W))],
        out_specs=pl.BlockSpec((tm, MEM_W), lambda i: (i, 0)),
        out_shape=jax.ShapeDtypeStruct((rows, MEM_W), BF16),
        compiler_params=_cparams("parallel"),
        name="mem_attn",
    )(mq, mk, mv)


def _sample_attn_kernel(past, q_ref, kn_ref, vn_ref, sq_ref, skn_ref, svn_ref, mq_ref,
                        clat_ref, crope_ref, csk_ref, csv_ref, cmk_ref, cmv_ref,
                        wkt_ref, wkn_ref, ind_ref, ones_ref, erow_ref, gkn_ref, gkr_ref, wvh_ref, rep_ref,
                        tri_ref, tris_ref, omla_ref, osb_ref, omem_ref):
    ds = q_ref.shape[1]
    hs = H_MLA * ds
    ds_shift = ds.bit_length() - 1
    tc = tri_ref.shape[0]
    row = lax.broadcasted_iota(jnp.int32, (hs, 1), 0)
    qpos = past + (row & (ds - 1))
    rows_of = lambda x, hd: x[hd * ds:(hd + 1) * ds]

    latb = clat_ref[0, 0].astype(BF16)
    rope = crope_ref[0, 0]
    ropeb = rope.astype(BF16)
    kn = _dot(latb, wkn_ref[...])
    ssq = _dot((kn * kn).astype(BF16), ind_ref[...]) + _dot((rope * rope).astype(BF16), ones_ref[...])
    inv_hi, inv_lo = _split(lax.rsqrt(ssq * (1.0 / QK_DIM) + EPS))
    inv_rows = _dot_nt(erow_ref[...], inv_hi) + _dot_nt(erow_ref[...], inv_lo)
    qabs, qrope, s2 = [], [], []
    for hd in range(H_MLA):
        qh = q_ref[hd].astype(F32)
        qabs.append(_dot((qh[:, :NOPE_DIM] * gkn_ref[...]).astype(BF16), wkt_ref[hd]))
        qrope.append((qh[:, NOPE_DIM:] * gkr_ref[...]).astype(BF16))
        s2.append(_dot_nt(q_ref[hd], kn_ref[hd]))
    s1 = _dot_nt(jnp.concatenate(qabs, axis=0).astype(BF16), latb) + _dot_nt(jnp.concatenate(qrope, axis=0), ropeb)
    s1 = s1 * inv_rows
    s2 = jnp.concatenate(s2, axis=0)
    q_chunk = qpos // CHUNK
    s1 = jnp.where((lax.broadcasted_iota(jnp.int32, s1.shape, 1) // CHUNK) <= q_chunk, s1, NEG_BIG)
    s2 = jnp.where(((past + lax.broadcasted_iota(jnp.int32, s2.shape, 1)) // CHUNK) <= q_chunk, s2, NEG_BIG)
    m = jnp.maximum(jnp.max(s1, axis=-1, keepdims=True), jnp.max(s2, axis=-1, keepdims=True))
    p1 = jnp.exp2(s1 - m)
    p2 = jnp.exp2(s2 - m)
    den = jnp.sum(p1, axis=-1, keepdims=True) + jnp.sum(p2, axis=-1, keepdims=True)
    olat = _dot(p1.astype(BF16), latb)
    for hd in range(H_MLA):
        o = _dot(rows_of(olat, hd).astype(BF16), wvh_ref[hd]) + _dot(rows_of(p2, hd).astype(BF16), vn_ref[hd])
        omla_ref[hd] = (o / rows_of(den, hd)).astype(BF16)

    sq_all = jnp.concatenate([sq_ref[hd] for hd in range(H_SB)], axis=0)
    sqbd = _dot(sq_all, rep_ref[...])
    lane_head = lax.broadcasted_iota(jnp.int32, sqbd.shape, 1) // SB_DIM
    sqbd = jnp.where(lane_head == lax.shift_right_logical(row, ds_shift), sqbd, 0.0).astype(BF16)
    z1 = _dot_nt(sqbd, csk_ref[0, 0].astype(BF16))
    z2 = jnp.concatenate([_dot_nt(sq_ref[hd], skn_ref[hd]) for hd in range(H_SB)], axis=0)
    before_n = lax.broadcasted_iota(jnp.int32, z2.shape, 1) < (row & (ds - 1))
    l2 = jnp.where(before_n, -_softplus(z2), 0.0)
    a2 = jnp.where(before_n, jnp.exp(z2 + l2 + _dot2(l2, tris_ref[...])), 0.0)
    c = jnp.sum(l2, axis=-1, keepdims=True)
    l1 = -_softplus(z1)
    tri = tri_ref[...]
    a1 = [None] * (past // tc)
    for cb in reversed(range(past // tc)):
        cs = slice(cb * tc, (cb + 1) * tc)
        lc = l1[:, cs]
        a1[cb] = jnp.exp(z1[:, cs] + lc + _dot2(lc, tri) + c).astype(BF16)
        c = c + jnp.sum(lc, axis=-1, keepdims=True)
    osb = _dot(jnp.concatenate(a1, axis=1), csv_ref[0, 0].astype(BF16))
    for hd in range(H_SB):
        o = rows_of(osb, hd)[:, hd * SB_DIM:(hd + 1) * SB_DIM] + _dot(rows_of(a2, hd).astype(BF16), svn_ref[hd])
        osb_ref[hd] = o.astype(BF16)

    omem_ref[...] = _mem_heads(mq_ref[...], cmk_ref[0, 0].astype(BF16), cmv_ref[0, 0].astype(BF16)).astype(BF16)


def _sample_attn(q, kn, vn, sq, skn, svn, mq, c_lat, c_rope, c_sbk, c_sbv, c_mk, c_mv, layer, p):
    nb, past = c_lat.shape[1], c_lat.shape[2]
    ds = q.shape[1] // nb
    assert ds & (ds - 1) == 0, "row -> query index uses a bit mask"
    n_mem = c_mk.shape[2]
    tc = min(256, past)
    assert past % tc == 0
    hs = H_MLA * ds
    erow = np.zeros((hs, LANES), np.float32)
    erow[np.arange(hs), np.arange(hs) // ds] = 1.0
    rep = np.tile(np.eye(SB_DIM, dtype=np.float32), (1, H_SB))
    hm = lambda d: pl.BlockSpec((H_MLA, ds, d), lambda b: (0, b, 0))
    cache = lambda n, w: pl.BlockSpec((1, 1, n, w), lambda b: (layer, b, 0, 0))
    rows = q.shape[1]
    return pl.pallas_call(
        functools.partial(_sample_attn_kernel, past),
        grid=(nb,),
        in_specs=[hm(QK_DIM), hm(QK_DIM), hm(V_DIM), hm(SB_DIM), hm(SB_DIM), hm(SB_DIM),
                  pl.BlockSpec((ds, MEM_W), lambda b: (b, 0)),
                  cache(past, KV_LORA), cache(past, ROPE_DIM), cache(past, SB_W), cache(past, SB_W),
                  cache(n_mem, MEM_W), cache(n_mem, MEM_W),
                  _full((H_MLA, NOPE_DIM, KV_LORA)), _full((KV_LORA, H_MLA * NOPE_DIM)),
                  _full((H_MLA * NOPE_DIM, LANES)), _full((ROPE_DIM, LANES)), _full((hs, LANES)),
                  _full((1, NOPE_DIM)), _full((1, ROPE_DIM)), _full((H_MLA, KV_LORA, V_DIM)),
                  _full((SB_DIM, SB_W)), _full((tc, tc)), _full((ds, ds))],
        out_specs=(hm(V_DIM), hm(SB_DIM), pl.BlockSpec((ds, MEM_W), lambda b: (b, 0))),
        out_shape=(jax.ShapeDtypeStruct((H_MLA, rows, V_DIM), BF16),
                   jax.ShapeDtypeStruct((H_SB, rows, SB_DIM), BF16),
                   jax.ShapeDtypeStruct((rows, MEM_W), BF16)),
        compiler_params=_cparams("parallel"),
        name="sample_attn",
    )(q, kn, vn, sq, skn, svn, mq, c_lat, c_rope, c_sbk, c_sbv, c_mk, c_mv,
      p["w_k_t"], p["w_k_nope"], p["ind64"], p["ones_rope"], jnp.asarray(erow, BF16), p["g_k_nope"], p["g_k_rope"],
      p["w_v_h"], jnp.asarray(rep, BF16), _tri(tc), _tri(ds))


def _merge_kernel(x_ref, gattn_ref, wg_ref, omla_ref, osb_ref, omem_ref, woa_ref, wob_ref, wom_ref, wout_ref,
                  gffn_ref, wrh_ref, wrl_ref, br_ref, x1_ref, idx_ref, gate_ref):
    x = x_ref[...]
    h = (_rms(x) * gattn_ref[...]).astype(BF16)
    g = 1.0 / (1.0 + jnp.exp(-_dot(h, wg_ref[...])))
    ua = _dot(jnp.concatenate([omla_ref[hd] for hd in range(H_MLA)], axis=-1), woa_ref[...])
    ub = _dot(jnp.concatenate([osb_ref[hd] for hd in range(H_SB)], axis=-1), wob_ref[...])
    um = _dot(omem_ref[...], wom_ref[...])
    u = g[:, :D_MODEL] * ua + g[:, D_MODEL:2 * D_MODEL] * ub + g[:, 2 * D_MODEL:] * um
    x1 = x + _dot(u.astype(BF16), wout_ref[...])
    x1_ref[...] = x1
    xh, xl = _split(_rms(x1) * gffn_ref[...])
    lg = _dot(xh, wrh_ref[...]) + _dot(xh, wrl_ref[...]) + _dot(xl, wrh_ref[...]) + br_ref[...]
    lane = lax.broadcasted_iota(jnp.int32, lg.shape, 1).astype(F32)
    vals, ids = [], []
    for _ in range(TOP_K):
        m = jnp.max(lg, axis=-1, keepdims=True)
        sel = jnp.min(jnp.where(lg == m, lane, float(LANES)), axis=-1, keepdims=True)
        vals.append(m)
        ids.append(sel)
        lg = jnp.where(lane == sel, NEG_BIG, lg)
    es = [jnp.exp(v - vals[0]) for v in vals]
    den = es[0] + es[1] + es[2] + es[3]
    idx_o = jnp.zeros(lg.shape, F32)
    gate_o = jnp.zeros(lg.shape, F32)
    for k in range(TOP_K):
        idx_o = jnp.where(lane == float(k), ids[k], idx_o)
        gate_o = jnp.where(lane == float(k), es[k] / den, gate_o)
    idx_ref[...] = idx_o.astype(jnp.int32)
    gate_ref[...] = gate_o


def _merge(x2d, omla, osb, omem, p):
    rows = x2d.shape[0]
    tm = TM_MERGE
    row = lambda w: pl.BlockSpec((tm, w), lambda i: (i, 0))
    hm = lambda d: pl.BlockSpec((H_MLA, tm, d), lambda i: (0, i, 0))
    return pl.pallas_call(
        _merge_kernel,
        grid=(rows // tm,),
        in_specs=[row(D_MODEL), _full((1, D_MODEL)), _full((D_MODEL, 3 * D_MODEL)), hm(V_DIM), hm(SB_DIM),
                  row(MEM_W), _full((MLA_W, D_MODEL)), _full((SB_W, D_MODEL)),
                  _full((MEM_W, D_MODEL)), _full((D_MODEL, D_MODEL)), _full((1, D_MODEL)),
                  _full((D_MODEL, LANES)), _full((D_MODEL, LANES)), _full((1, LANES))],
        out_specs=(row(D_MODEL), row(LANES), row(LANES)),
        out_shape=(jax.ShapeDtypeStruct((rows, D_MODEL), F32), jax.ShapeDtypeStruct((rows, LANES), jnp.int32),
                   jax.ShapeDtypeStruct((rows, LANES), F32)),
        compiler_params=_cparams("parallel"),
        name="merge",
    )(x2d, p["g_attn"], p["w_g"], omla, osb, omem, p["w_o_mla"], p["w_o_sb"], p["w_o_mem"], p["w_out"],
      p["g_ffn"], p["w_r_hi"], p["w_r_lo"], p["b_r"])


def _row_dma(src, src_row, dst, dst_row, sem):
    return pltpu.make_async_copy(src.at[pl.ds(src_row, 1), :], dst.at[pl.ds(dst_row, 1), :], sem)


def _row_dma_wait(src, dst, sem, n):
    def body(r, carry):
        _row_dma(src, 0, dst, 0, sem).wait()
        return carry

    lax.fori_loop(0, n, body, 0, unroll=DMA_UNROLL)


def _dispatch_kernel(dest_ref, x_ref, xs_in, xs_out, sem):
    del xs_in
    n = dest_ref.shape[2]
    for r in range(n):
        _row_dma(x_ref, r // TOP_K, xs_out, dest_ref[0, 0, r], sem.at[0]).start(priority=r % DMA_THREADS)
    _row_dma_wait(x_ref, xs_out, sem.at[0], n)


def _moe_dispatch(x1, dest, xs):
    rows = x1.shape[0]
    tt = TT_COMBINE
    nt = rows // tt
    return pl.pallas_call(
        _dispatch_kernel,
        grid=(nt,),
        in_specs=[pl.BlockSpec((1, 1, TOP_K * tt), lambda t: (t, 0, 0), memory_space=pltpu.SMEM),
                  pl.BlockSpec((tt, D_MODEL), lambda t: (t, 0)),
                  pl.BlockSpec(memory_space=pl.ANY)],
        out_specs=pl.BlockSpec(memory_space=pl.ANY),
        out_shape=jax.ShapeDtypeStruct(xs.shape, xs.dtype),
        input_output_aliases={2: 0},
        scratch_shapes=[pltpu.SemaphoreType.DMA((1,))],
        compiler_params=_cparams("arbitrary"),
        name="moe_dispatch",
    )(dest.reshape(nt, 1, TOP_K * tt), x1, xs)


def _moe_kernel(be_ref, nact_ref, xs_ref, gffn_ref, wgu_ref, bgu_ref, wd_ref, bd_ref, o_ref, wgu_sc, wd_sc):
    b = pl.program_id(0)

    @pl.when(b >= nact_ref[0])
    def _():
        o_ref[...] = jnp.zeros(o_ref.shape, o_ref.dtype)

    @pl.when(b < nact_ref[0])
    def _():
        changed = jnp.logical_or(b == 0, be_ref[b] != be_ref[jnp.maximum(b - 1, 0)])

        @pl.when(changed)
        def _():
            wgu_sc[...] = wgu_ref[0].astype(BF16)
            wd_sc[...] = wd_ref[0].astype(BF16)

        xb = (_rms(xs_ref[...]) * gffn_ref[...]).astype(BF16)
        gu = _dot(xb, wgu_sc[...]) + bgu_ref[0]
        g = jnp.minimum(gu[:, :D_FF], SWIGLU_LIMIT)
        u = jnp.clip(gu[:, D_FF:], -SWIGLU_LIMIT, SWIGLU_LIMIT)
        hid = (u + 1.0) * (g / (1.0 + jnp.exp(-SWIGLU_ALPHA * g)))
        o_ref[...] = _dot(hid.astype(BF16), wd_sc[...]) + bd_ref[0]


def _moe_experts(xs, block_expert, n_active, p):
    n_blocks = block_expert.shape[0]
    bm = BM_MOE
    blk = lambda b, be, na: (jnp.minimum(b, na[0] - 1), 0)
    grid_spec = pltpu.PrefetchScalarGridSpec(
        num_scalar_prefetch=2,
        grid=(n_blocks,),
        in_specs=[
            pl.BlockSpec((bm, D_MODEL), blk),
            pl.BlockSpec((1, D_MODEL), lambda b, be, na: (0, 0)),
            pl.BlockSpec((1, D_MODEL, 2 * D_FF), lambda b, be, na: (be[b], 0, 0)),
            pl.BlockSpec((1, 1, 2 * D_FF), lambda b, be, na: (be[b], 0, 0)),
            pl.BlockSpec((1, D_FF, D_MODEL), lambda b, be, na: (be[b], 0, 0)),
            pl.BlockSpec((1, 1, D_MODEL), lambda b, be, na: (be[b], 0, 0)),
        ],
        out_specs=pl.BlockSpec((bm, D_MODEL), lambda b, be, na: (b, 0)),
        scratch_shapes=[pltpu.VMEM((D_MODEL, 2 * D_FF), BF16), pltpu.VMEM((D_FF, D_MODEL), BF16)],
    )
    return pl.pallas_call(
        _moe_kernel,
        grid_spec=grid_spec,
        out_shape=jax.ShapeDtypeStruct((n_blocks * bm, D_MODEL), F32),
        compiler_params=_cparams("arbitrary"),
        name="moe_experts",
    )(block_expert, n_active, xs, p["g_ffn"], p["w_gate_up"], p["b_gate_up"], p["w_down"], p["b_down"])


def _combine_gather(y_hbm, pos_ref, dst, sem, n):
    for r in range(n):
        _row_dma(y_hbm, pos_ref[0, 0, r], dst, r, sem).start(priority=r % DMA_THREADS)


def _combine_kernel(nta, pos_ref, posn_ref, y_hbm, x1a_ref, ga_ref, x1b_ref, gb_ref, oa_ref, ob_ref, buf, sem):
    t = pl.program_id(0)
    nt = pl.num_programs(0)
    tt = oa_ref.shape[0]
    n = TOP_K * tt
    slot = t % 2

    @pl.when(t == 0)
    def _():
        _combine_gather(y_hbm, pos_ref, buf.at[0], sem.at[0], n)

    @pl.when(t + 1 < nt)
    def _():
        _combine_gather(y_hbm, posn_ref, buf.at[1 - slot], sem.at[1 - slot], n)

    _row_dma_wait(y_hbm, buf.at[slot], sem.at[slot], n)

    def combine(x1_ref, gate_ref, o_ref):
        acc = x1_ref[...]
        gate = gate_ref[...]
        for k in range(TOP_K):
            acc = acc + gate[:, k:k + 1] * buf[slot, k * tt:(k + 1) * tt, :]
        o_ref[...] = acc

    @pl.when(t < nta)
    def _():
        combine(x1a_ref, ga_ref, oa_ref)

    @pl.when(t >= nta)
    def _():
        combine(x1b_ref, gb_ref, ob_ref)


def _moe_combine(x1a, gate_a, x1b, gate_b, yb, pos):
    tt = TT_COMBINE
    nta, ntb = x1a.shape[0] // tt, x1b.shape[0] // tt
    nt = nta + ntb
    pos3 = pos.reshape(nt, tt, TOP_K).transpose(0, 2, 1).reshape(nt, 1, TOP_K * tt)
    ia = lambda t: (jnp.minimum(t, nta - 1), 0)
    ib = lambda t: (jnp.maximum(t - nta, 0), 0)
    return pl.pallas_call(
        functools.partial(_combine_kernel, nta),
        grid=(nt,),
        in_specs=[pl.BlockSpec((1, 1, TOP_K * tt), lambda t: (t, 0, 0), memory_space=pltpu.SMEM),
                  pl.BlockSpec((1, 1, TOP_K * tt), lambda t: (jnp.minimum(t + 1, nt - 1), 0, 0),
                               memory_space=pltpu.SMEM),
                  pl.BlockSpec(memory_space=pl.ANY),
                  pl.BlockSpec((tt, D_MODEL), ia), pl.BlockSpec((tt, LANES), ia),
                  pl.BlockSpec((tt, D_MODEL), ib), pl.BlockSpec((tt, LANES), ib)],
        out_specs=(pl.BlockSpec((tt, D_MODEL), ia), pl.BlockSpec((tt, D_MODEL), ib)),
        out_shape=(jax.ShapeDtypeStruct(x1a.shape, F32), jax.ShapeDtypeStruct(x1b.shape, F32)),
        scratch_shapes=[pltpu.VMEM((2, TOP_K * tt, D_MODEL), F32), pltpu.SemaphoreType.DMA((2,))],
        compiler_params=_cparams("arbitrary"),
        name="moe_combine",
    )(pos3, pos3, yb, x1a, gate_a, x1b, gate_b)


def _moe_route(idx):
    rows = idx.shape[0]
    n = rows * TOP_K
    bm = BM_MOE
    e = idx.reshape(n)
    onehot = (e[:, None] == jnp.arange(N_EXPERTS, dtype=jnp.int32)[None, :]).astype(jnp.int32)
    csum = jnp.cumsum(onehot, axis=0)
    rank = jnp.sum(csum * onehot, axis=1) - 1
    counts = csum[-1]
    padded = (counts + bm - 1) // bm * bm
    pend = jnp.cumsum(padded)
    dest = ((pend - padded)[e] + rank).astype(jnp.int32).reshape(rows, TOP_K)
    n_blocks = -(-n // bm) + N_EXPERTS
    starts = jnp.arange(n_blocks, dtype=jnp.int32) * bm
    block_expert = jnp.minimum(
        jnp.sum((pend[None, :] <= starts[:, None]).astype(jnp.int32), axis=1), N_EXPERTS - 1).astype(jnp.int32)
    n_active = (pend[-1:] // bm).astype(jnp.int32)
    return dest, block_expert, n_active


def _pack_layer(l, g_attn, w_in, g_q_lat, w_q_b, g_q_nope, g_q_rope, g_kv_lat, w_kv_b, g_k_nope, g_k_rope, g_mem,
                w_mem_kv, g_mem_q, g_mem_k, w_o_mla, w_o_sb, w_o_mem, w_out, g_ffn, w_router, b_router,
                w_gate_up, b_gate_up, w_down, b_down):
    w = w_in[l]
    off = np.cumsum((Q_LORA, KV_LORA, ROPE_DIM, SB_W, SB_W, SB_W, MEM_W))
    o_kpe, o_sbq = int(off[1]), int(off[2])
    o_gate = int(off[6])
    kpe1 = w[:, o_kpe:o_kpe + ROPE_HALF]
    kpe2 = w[:, o_kpe + ROPE_HALF:o_kpe + ROPE_DIM]
    zl = jnp.zeros((D_MODEL, KPE_LANE), F32)
    zr = jnp.zeros((D_MODEL, LANES - KPE_LANE - ROPE_DIM), F32)
    w_a = jnp.concatenate([w[:, :o_kpe], zl, kpe1, kpe2, zr, zl, -kpe2, kpe1, zr, w[:, o_sbq:o_gate]], axis=1)
    wq = w_q_b[l]
    z_n = jnp.zeros((Q_LORA, H_MLA, NOPE_DIM), F32)
    wq_sw = jnp.concatenate([z_n, -wq[..., NOPE_DIM + ROPE_HALF:], wq[..., NOPE_DIM:NOPE_DIM + ROPE_HALF]], axis=-1)
    lane = np.arange(QK_W)
    ind = np.zeros((QK_W, LANES), np.float32)
    ind[lane, lane // QK_DIM] = 1.0
    ind64 = np.zeros((H_MLA * NOPE_DIM, LANES), np.float32)
    ind64[np.arange(H_MLA * NOPE_DIM), np.arange(H_MLA * NOPE_DIM) // NOPE_DIM] = 1.0
    pk = np.zeros((ROPE_DIM, QK_W), np.float32)
    for hd in range(H_MLA):
        pk[np.arange(ROPE_DIM), hd * QK_DIM + NOPE_DIM + np.arange(ROPE_DIM)] = 1.0
    wkv = w_kv_b[l]
    w_k = jnp.concatenate([wkv[..., :NOPE_DIM], jnp.zeros((KV_LORA, H_MLA, ROPE_DIM), F32)], axis=-1)
    g_q = jnp.tile(jnp.concatenate([g_q_nope[l], g_q_rope[l], g_q_rope[l]]), H_MLA) * (QK_DIM ** -0.5 * LOG2E)
    g_k = jnp.tile(jnp.concatenate([g_k_nope[l], g_k_rope[l], g_k_rope[l]]), H_MLA)
    wm = w_mem_kv[l]
    w_mem = jnp.concatenate([wm[..., :MEM_DIM].reshape(D_MODEL, MEM_W), wm[..., MEM_DIM:].reshape(D_MODEL, MEM_W)], 1)
    w_r = jnp.concatenate([w_router[l], jnp.zeros((D_MODEL, LANES - N_EXPERTS), F32)], axis=1)
    w_r_hi = w_r.astype(BF16)
    b_r = jnp.concatenate([b_router[l].astype(F32), jnp.full((LANES - N_EXPERTS,), NEG_BIG, F32)])
    return {
        "g_attn": g_attn[l][None], "w_a": w_a.astype(BF16), "g_q_lat": g_q_lat[l][None],
        "w_q": wq.reshape(Q_LORA, QK_W).astype(BF16), "w_q_sw": wq_sw.reshape(Q_LORA, QK_W).astype(BF16),
        "ind96": jnp.asarray(ind, BF16), "ind96_t": jnp.asarray(ind.T, BF16), "g_q": g_q[None],
        "g_kv_lat": g_kv_lat[l][None], "g_mem_q": jnp.tile(g_mem_q[l], H_MEM)[None] * (MEM_DIM ** -0.5),
        "w_k": w_k.reshape(KV_LORA, QK_W).astype(BF16), "p_kpe": jnp.asarray(pk, BF16), "g_k": g_k[None],
        "w_v": wkv[..., NOPE_DIM:].reshape(KV_LORA, MLA_W).astype(BF16),
        "w_v_t": wkv[..., NOPE_DIM:].reshape(KV_LORA, MLA_W).T.astype(BF16),
        "w_v_h": wkv[..., NOPE_DIM:].transpose(1, 0, 2).astype(BF16),
        "w_k_t": wkv[..., :NOPE_DIM].transpose(1, 2, 0).astype(BF16),
        "w_k_nope": wkv[..., :NOPE_DIM].reshape(KV_LORA, H_MLA * NOPE_DIM).astype(BF16),
        "ind64": jnp.asarray(ind64, BF16), "ones_rope": jnp.ones((ROPE_DIM, LANES), BF16),
        "g_k_nope": g_k_nope[l][None], "g_k_rope": jnp.concatenate([g_k_rope[l], g_k_rope[l]])[None],
        "g_mem": g_mem[l][None], "w_mem": w_mem.astype(BF16), "g_mem_k": jnp.tile(g_mem_k[l], H_MEM)[None],
        "w_g": w[:, o_gate:].astype(BF16), "w_sbv_t": w[:, o_gate - MEM_W - SB_W:o_gate - MEM_W].T.astype(BF16),
        "w_o_mla": w_o_mla[l].astype(BF16), "w_o_sb": w_o_sb[l].astype(BF16),
        "w_o_mem": w_o_mem[l].astype(BF16), "w_out": w_out[l].astype(BF16), "g_ffn": g_ffn[l][None],
        "w_r_hi": w_r_hi, "w_r_lo": (w_r - w_r_hi.astype(F32)).astype(BF16), "b_r": b_r[None],
        "w_gate_up": w_gate_up[l], "b_gate_up": b_gate_up[l][:, None, :], "w_down": w_down[l],
        "b_down": b_down[l][:, None, :],
    }


def kernel(x_prompt, x_sample, mem_prompt, cache_mla_latent, cache_mla_rope, cache_sb_k, cache_sb_v, cache_mem_k, cache_mem_v, g_attn, w_in, g_q_lat, w_q_b, g_q_nope, g_q_rope, g_kv_lat, w_kv_b, g_k_nope, g_k_rope, g_mem, w_mem_kv, g_mem_q, g_mem_k, w_o_mla, w_o_sb, w_o_mem, w_out, g_ffn, w_router, b_router, w_gate_up, b_gate_up, w_down, b_down):
    depth = g_attn.shape[0]
    bp, sp, _ = x_prompt.shape
    bs, ss, _ = x_sample.shape
    past = cache_mla_latent.shape[2]
    n_mem = mem_prompt.shape[1]
    assert bp == 1 and sp % TQ_MLA == 0 and sp % TM_KV == 0 and (bs * ss) % TM_PROJ == 0 and TM_PROJ % ss == 0
    rows_s = bs * ss
    c_sbk = cache_sb_k.reshape(depth, bs, past, SB_W)
    c_sbv = cache_sb_v.reshape(depth, bs, past, SB_W)
    c_mk = cache_mem_k.reshape(depth, bs, n_mem, MEM_W)
    c_mv = cache_mem_v.reshape(depth, bs, n_mem, MEM_W)
    xp = x_prompt.reshape(sp, D_MODEL)
    xs = x_sample.reshape(rows_s, D_MODEL)
    mem2d = mem_prompt.reshape(n_mem, D_MODEL)
    base_p, off_p = np.arange(sp // TM_PROJ) * TM_PROJ, np.arange(TM_PROJ)
    base_s, off_s = np.full((rows_s // TM_PROJ,), past), np.arange(TM_PROJ) % ss
    outs = [[] for _ in range(10)]
    for l in range(depth):
        p = _pack_layer(l, g_attn, w_in, g_q_lat, w_q_b, g_q_nope, g_q_rope, g_kv_lat, w_kv_b, g_k_nope, g_k_rope,
                        g_mem, w_mem_kv, g_mem_q, g_mem_k, w_o_mla, w_o_sb, w_o_mem, w_out, g_ffn, w_router,
                        b_router, w_gate_up, b_gate_up, w_down, b_down)
        q, lat, kpe, sq, sk, sv, skh, _, mq, svt = _proj(xp, base_p, off_p, p)
        kh, _, vth = _kv_expand(lat, kpe, p)
        o_mla = _mla_attn(q, kh, vth)
        o_sb = _sb_attn(sq, skh, svt)
        mk, mv = _mem_kv(mem2d, p)
        o_mem = _mem_attn(mq, mk, mv)
        x1p, idx_p, gate_p = _merge(xp, o_mla, o_sb, o_mem, p)
        q_s, lat_s, kpe_s, sq_s, sk_s, sv_s, skh_s, svh_s, mq_s, _ = _proj(xs, base_s, off_s, p)
        kn, vn, _ = _kv_expand(lat_s, kpe_s, p)
        o_mla_s, o_sb_s, o_mem_s = _sample_attn(q_s, kn, vn, sq_s, skh_s, svh_s, mq_s, cache_mla_latent,
                                                cache_mla_rope, c_sbk, c_sbv, c_mk, c_mv, l, p)
        x1s, idx_s, gate_s = _merge(xs, o_mla_s, o_sb_s, o_mem_s, p)
        dest, block_expert, n_active = _moe_route(jnp.concatenate([idx_p[:, :TOP_K], idx_s[:, :TOP_K]], axis=0))
        slots = jnp.zeros((block_expert.shape[0] * BM_MOE, D_MODEL), F32)
        slots = _moe_dispatch(x1s, dest[sp:], _moe_dispatch(x1p, dest[:sp], slots))
        yb = _moe_experts(slots, block_expert, n_active, p)
        xp, xs = _moe_combine(x1p, gate_p, x1s, gate_s, yb, dest)
        for lst, val in zip(outs, (lat.reshape(bp, sp, KV_LORA), kpe.reshape(bp, sp, ROPE_DIM),
                                   sk.reshape(bp, sp, H_SB, SB_DIM), sv.reshape(bp, sp, H_SB, SB_DIM),
                                   mk.reshape(bp, n_mem, H_MEM, MEM_DIM), mv.reshape(bp, n_mem, H_MEM, MEM_DIM),
                                   lat_s.reshape(bs, ss, KV_LORA), kpe_s.reshape(bs, ss, ROPE_DIM),
                                   sk_s.reshape(bs, ss, H_SB, SB_DIM), sv_s.reshape(bs, ss, H_SB, SB_DIM))):
            lst.append(val)
    return (xp.reshape(bp, sp, D_MODEL), xs.reshape(bs, ss, D_MODEL)) + tuple(jnp.stack(o) for o in outs)
```

```python
import functools

import numpy as np
import jax
import jax.numpy as jnp
from jax import lax
from jax.experimental import pallas as pl
from jax.experimental.pallas import tpu as pltpu

F32 = jnp.float32
BF16 = jnp.bfloat16

D_MODEL = 1024
CHUNK = 64
EPS = 1e-6
H_MLA = 8
NOPE_DIM = 64
ROPE_DIM = 32
ROPE_HALF = ROPE_DIM // 2
V_DIM = 64
Q_LORA = 384
KV_LORA = 256
ROPE_BASE = 10000.0
QK_DIM = NOPE_DIM + ROPE_DIM
QK_W = H_MLA * QK_DIM
H_SB = 8
SB_DIM = 64
H_MEM = 4
MEM_DIM = 128
N_EXPERTS = 32
TOP_K = 4
D_FF = 1024
SWIGLU_LIMIT = 7.0
SWIGLU_ALPHA = 1.702
MLA_W = H_MLA * V_DIM
SB_W = H_SB * SB_DIM
MEM_W = H_MEM * MEM_DIM

LANES = 128
VT_ROWS = LANES
MLA_CHAINS = 4
MLA_KSPLIT = 2
assert MLA_KSPLIT % 2 == 0
LOG2E = 1.4426950408889634
A_QLAT = 0
A_KVLAT = A_QLAT + Q_LORA
A_KPE = A_KVLAT + KV_LORA
A_KPE_SW = A_KPE + LANES
A_SBQ = A_KPE_SW + LANES
A_SBK = A_SBQ + SB_W
A_SBV = A_SBK + SB_W
A_MEMQ = A_SBV + SB_W
A_COLS = A_MEMQ + MEM_W
KPE_LANE = NOPE_DIM

TM_PROJ = 256
TM_KV = 512
TQ_MLA = 1024
TQ_SB = 256
SB_HEADS = 2
TM_MEM = 512
TM_MERGE = 256
BM_MOE = 512
TT_COMBINE = 256
DMA_UNROLL = 8
DMA_THREADS = 2
TOP_K_SHIFT = TOP_K.bit_length() - 1
assert 1 << TOP_K_SHIFT == TOP_K
VMEM_LIMIT = 56 * 1024 * 1024

SB_UNDERFLOW = -120.0
NEG_BIG = -3.0e38


def _cparams(*sem):
    return pltpu.CompilerParams(dimension_semantics=sem, vmem_limit_bytes=VMEM_LIMIT)


def _split(x):
    hi = x.astype(BF16)
    lo = (x - hi.astype(F32)).astype(BF16)
    return hi, lo


def _dot(a, b):
    return jnp.dot(a, b, preferred_element_type=F32)


def _dot2(x, m):
    hi, lo = _split(x)
    return _dot(hi, m) + _dot(lo, m)


def _dot_nt(a, b):
    return lax.dot_general(a, b, (((1,), (1,)), ((), ())), preferred_element_type=F32)


def _rms(x):
    return x * lax.rsqrt(jnp.mean(x * x, axis=-1, keepdims=True) + EPS)


def _softplus(z):
    return jnp.maximum(z, 0.0) + jnp.log(1.0 + jnp.exp(-jnp.abs(z)))


def _full(shape):
    n = len(shape)
    return pl.BlockSpec(shape, lambda *_: (0,) * n)


def _proj_kernel(x_ref, gattn_ref, wa_ref, gqlat_ref, wq_ref, wqs_ref, ind_ref, indt_ref, gq_ref, gkv_ref,
                 bcs_ref, ocos_ref, osin_ref, gmq_ref, wsvt_ref,
                 q_ref, lat_ref, kpe_ref, sq_ref, sk_ref, sv_ref, skh_ref, svh_ref, mq_ref, svt_ref):
    h = (_rms(x_ref[...]) * gattn_ref[...]).astype(BF16)
    z = _dot(h, wa_ref[...])
    svt = _dot_nt(wsvt_ref[...], h)
    pad = jnp.zeros((VT_ROWS - SB_DIM, svt.shape[1]), F32)
    for hd in range(H_SB):
        svt_ref[hd] = jnp.concatenate([svt[hd * SB_DIM:(hd + 1) * SB_DIM, :], pad], axis=0).astype(BF16)
    bc = bcs_ref[0, 0:1, :]
    bs = bcs_ref[0, 1:2, :]
    oc = ocos_ref[...]
    osn = osin_ref[...]
    cos_f = bc * oc - bs * osn
    sin_f = bs * oc + bc * osn
    qn = (_rms(z[:, A_QLAT:A_QLAT + Q_LORA]) * gqlat_ref[...]).astype(BF16)
    qr = _dot(qn, wq_ref[...]) * cos_f + _dot(qn, wqs_ref[...]) * sin_f
    ssq = _dot2(qr * qr, ind_ref[...])
    inv = lax.rsqrt(ssq * (1.0 / QK_DIM) + EPS)
    qo = qr * _dot2(inv, indt_ref[...]) * gq_ref[...]
    for hd in range(H_MLA):
        q_ref[hd] = qo[:, hd * QK_DIM:(hd + 1) * QK_DIM].astype(BF16)
    lat_ref[...] = _rms(z[:, A_KVLAT:A_KVLAT + KV_LORA]) * gkv_ref[...]
    kr = z[:, A_KPE:A_KPE + LANES] * cos_f[:, :LANES] + z[:, A_KPE_SW:A_KPE_SW + LANES] * sin_f[:, :LANES]
    kpe_ref[...] = kr[:, KPE_LANE:KPE_LANE + ROPE_DIM]
    sbq = z[:, A_SBQ:A_SBQ + SB_W] * (SB_DIM ** -0.5)
    sbk = z[:, A_SBK:A_SBK + SB_W]
    sbv = z[:, A_SBV:A_SBV + SB_W]
    sk_ref[...] = sbk
    sv_ref[...] = sbv
    for hd in range(H_SB):
        sl = slice(hd * SB_DIM, (hd + 1) * SB_DIM)
        sq_ref[hd] = sbq[:, sl].astype(BF16)
        skh_ref[hd] = sbk[:, sl].astype(BF16)
        svh_ref[hd] = sbv[:, sl].astype(BF16)
    mqs = []
    for hd in range(H_MEM):
        mqs.append(_rms(z[:, A_MEMQ + hd * MEM_DIM:A_MEMQ + (hd + 1) * MEM_DIM]))
    mq_ref[...] = (jnp.concatenate(mqs, axis=-1) * gmq_ref[...]).astype(BF16)


def _rope_tables(base_pos, off_pos):
    lane = np.arange(QK_W) % QK_DIM
    inv_freq = ROPE_BASE ** (-np.arange(ROPE_HALF, dtype=np.float64) / ROPE_HALF)
    freq = np.where(lane >= NOPE_DIM, inv_freq[(lane - NOPE_DIM) % ROPE_HALF], 0.0)
    ab = np.asarray(base_pos, np.float64)[:, None] * freq
    ao = np.asarray(off_pos, np.float64)[:, None] * freq
    bcs = np.stack([np.cos(ab), np.sin(ab)], axis=1).astype(np.float32)
    return jnp.asarray(bcs), jnp.asarray(np.cos(ao), F32), jnp.asarray(np.sin(ao), F32)


def _proj(x2d, base_pos, off_pos, p):
    rows = x2d.shape[0]
    tm = TM_PROJ
    nt = rows // tm
    bcs, ocos, osin = _rope_tables(base_pos, off_pos)
    row = lambda w: pl.BlockSpec((tm, w), lambda i: (i, 0))
    hm = lambda d: pl.BlockSpec((H_MLA, tm, d), lambda i: (0, i, 0))
    out_shape = (
        jax.ShapeDtypeStruct((H_MLA, rows, QK_DIM), BF16),
        jax.ShapeDtypeStruct((rows, KV_LORA), F32),
        jax.ShapeDtypeStruct((rows, ROPE_DIM), F32),
        jax.ShapeDtypeStruct((H_SB, rows, SB_DIM), BF16),
        jax.ShapeDtypeStruct((rows, SB_W), F32),
        jax.ShapeDtypeStruct((rows, SB_W), F32),
        jax.ShapeDtypeStruct((H_SB, rows, SB_DIM), BF16),
        jax.ShapeDtypeStruct((H_SB, rows, SB_DIM), BF16),
        jax.ShapeDtypeStruct((rows, MEM_W), BF16),
        jax.ShapeDtypeStruct((H_SB, VT_ROWS, rows), BF16),
    )
    return pl.pallas_call(
        _proj_kernel,
        grid=(nt,),
        in_specs=[row(D_MODEL), _full((1, D_MODEL)), _full((D_MODEL, A_COLS)), _full((1, Q_LORA)),
                  _full((Q_LORA, QK_W)), _full((Q_LORA, QK_W)), _full((QK_W, LANES)), _full((LANES, QK_W)),
                  _full((1, QK_W)), _full((1, KV_LORA)),
                  pl.BlockSpec((1, 2, QK_W), lambda i: (i, 0, 0)), _full((tm, QK_W)), _full((tm, QK_W)),
                  _full((1, MEM_W)), _full((SB_W, D_MODEL))],
        out_specs=(hm(QK_DIM), row(KV_LORA), row(ROPE_DIM), hm(SB_DIM), row(SB_W), row(SB_W), hm(SB_DIM),
                   hm(SB_DIM), row(MEM_W), pl.BlockSpec((H_SB, VT_ROWS, tm), lambda i: (0, 0, i))),
        out_shape=out_shape,
        compiler_params=_cparams("parallel"),
        name="proj",
    )(x2d, p["g_attn"], p["w_a"], p["g_q_lat"], p["w_q"], p["w_q_sw"], p["ind96"], p["ind96_t"], p["g_q"],
      p["g_kv_lat"], bcs, ocos, osin, p["g_mem_q"], p["w_sbv_t"])


def _expand_keys(lat, kpe, wk, pk, ind, indt, gk):
    kf = _dot(lat.astype(BF16), wk) + _dot2(kpe, pk)
    inv = lax.rsqrt(_dot2(kf * kf, ind) * (1.0 / QK_DIM) + EPS)
    return kf * _dot2(inv, indt) * gk


def _kv_expand_kernel(lat_ref, kpe_ref, wk_ref, pk_ref, ind_ref, indt_ref, gk_ref, wv_ref, wvt_ref,
                      k_ref, v_ref, vt_ref):
    lat = lat_ref[...]
    latb = lat.astype(BF16)
    ko = _expand_keys(lat, kpe_ref[...], wk_ref[...], pk_ref[...], ind_ref[...], indt_ref[...], gk_ref[...])
    v = _dot(latb, wv_ref[...])
    vt = _dot_nt(wvt_ref[...], latb)
    ones_row = (lax.broadcasted_iota(jnp.int32, (VT_ROWS - V_DIM, vt.shape[1]), 0) == 0).astype(F32)
    for hd in range(H_MLA):
        k_ref[hd] = ko[:, hd * QK_DIM:(hd + 1) * QK_DIM].astype(BF16)
        v_ref[hd] = v[:, hd * V_DIM:(hd + 1) * V_DIM].astype(BF16)
        vt_ref[hd] = jnp.concatenate([vt[hd * V_DIM:(hd + 1) * V_DIM, :], ones_row], axis=0).astype(BF16)


def _kv_expand(lat, kpe, p):
    rows = lat.shape[0]
    tm = min(TM_KV, rows)
    row = lambda w: pl.BlockSpec((tm, w), lambda i: (i, 0))
    hm = lambda d: pl.BlockSpec((H_MLA, tm, d), lambda i: (0, i, 0))
    return pl.pallas_call(
        _kv_expand_kernel,
        grid=(rows // tm,),
        in_specs=[row(KV_LORA), row(ROPE_DIM), _full((KV_LORA, QK_W)), _full((ROPE_DIM, QK_W)),
                  _full((QK_W, LANES)), _full((LANES, QK_W)), _full((1, QK_W)), _full((KV_LORA, MLA_W)),
                  _full((MLA_W, KV_LORA))],
        out_specs=(hm(QK_DIM), hm(V_DIM), pl.BlockSpec((H_MLA, VT_ROWS, tm), lambda i: (0, 0, i))),
        out_shape=(jax.ShapeDtypeStruct((H_MLA, rows, QK_DIM), BF16),
                   jax.ShapeDtypeStruct((H_MLA, rows, V_DIM), BF16),
                   jax.ShapeDtypeStruct((H_MLA, VT_ROWS, rows), BF16)),
        compiler_params=_cparams("parallel"),
        name="kv_expand",
    )(lat, kpe, p["w_k"], p["p_kpe"], p["ind96"], p["ind96_t"], p["g_k"], p["w_v"], p["w_v_t"])


def _mla_update(st, vt, m, acc):
    m_new = jnp.maximum(m, jnp.max(st, axis=0, keepdims=True))
    pr = jnp.exp2(st - m_new).astype(BF16)
    return m_new, acc * jnp.exp2(m - m_new) + _dot(vt, pr)


def _mla_step(qc, k, vt, m, acc, mask):
    st = _dot_nt(k, qc)
    if mask is not None:
        st = jnp.where(mask, st, NEG_BIG)
    return _mla_update(st, vt, m, acc)


def _mla_attn_kernel(q_ref, k_ref, vt_ref, o_ref, s_sc, p_sc):
    i = pl.program_id(1)
    tq = q_ref.shape[1]
    nc = MLA_CHAINS
    tc = tq // nc
    tk = tq // MLA_KSPLIT
    qs = [q_ref[0, c * tc:(c + 1) * tc, :] for c in range(nc)]
    key_chunk = lax.broadcasted_iota(jnp.int32, (tk, tc), 0) // CHUNK
    qry_chunk = lax.broadcasted_iota(jnp.int32, (tk, tc), 1) // CHUNK

    def scores(t, slot):
        k = k_ref[0, pl.ds(pl.multiple_of(t * tk, tk), tk), :]
        out = []
        for c in range(nc):
            st = _dot_nt(k, qs[c])
            s_sc[slot, c] = st
            out.append(jnp.max(st, axis=0, keepdims=True))
        return out

    def accumulate(t, slot, alphas, accs):
        vt = vt_ref[0, :, pl.ds(pl.multiple_of(t * tk, tk), tk)]
        return [accs[c] * alphas[c] + _dot(vt, p_sc[slot, c]) for c in range(nc)]

    def step(t, slot, cmax, alphas, ms, accs, last, mask_u):
        nxt = None if last else scores(t + 1, 1 - slot)
        accs = accumulate(jnp.maximum(t - 1, 0), 1 - slot, alphas, accs)
        m_new, alphas = [], []
        for c in range(nc):
            st = s_sc[slot, c]
            if mask_u is None:
                cm = cmax[c]
            else:
                st = jnp.where(key_chunk + mask_u * (tk // CHUNK) <= qry_chunk + c * (tc // CHUNK), st, NEG_BIG)
                cm = jnp.max(st, axis=0, keepdims=True)
            mn = jnp.maximum(ms[c], cm)
            p_sc[slot, c] = jnp.exp2(st - mn).astype(BF16)
            alphas.append(jnp.exp2(ms[c] - mn))
            m_new.append(mn)
        return nxt, alphas, m_new, accs

    p_sc[1] = jnp.zeros(p_sc.shape[1:], BF16)
    init = (tuple(scores(0, 0)) + (jnp.ones((1, tc), F32),) * nc + (jnp.full((1, tc), NEG_BIG, F32),) * nc
            + (jnp.zeros((VT_ROWS, tc), F32),) * nc)
    unpack = lambda carry: [list(carry[g * nc:(g + 1) * nc]) for g in range(4)]

    def body(j, carry):
        cmax, alphas, ms, accs = unpack(carry)
        for u in range(MLA_KSPLIT):
            cmax, alphas, ms, accs = step(j * MLA_KSPLIT + u, u % 2, cmax, alphas, ms, accs, False, None)
        return tuple(cmax) + tuple(alphas) + tuple(ms) + tuple(accs)

    cmax, alphas, ms, accs = unpack(lax.fori_loop(0, i, body, init))
    for u in range(MLA_KSPLIT):
        cmax, alphas, ms, accs = step(i * MLA_KSPLIT + u, u % 2, cmax, alphas, ms, accs, u + 1 == MLA_KSPLIT, u)
    accs = accumulate((i + 1) * MLA_KSPLIT - 1, (MLA_KSPLIT - 1) % 2, alphas, accs)
    for c in range(nc):
        ot = accs[c].T
        o_ref[0, c * tc:(c + 1) * tc, :] = (ot[:, :V_DIM] / ot[:, V_DIM:V_DIM + 1]).astype(BF16)


def _mla_attn(q, k, vt):
    rows = q.shape[1]
    tq = min(TQ_MLA, rows)
    return pl.pallas_call(
        _mla_attn_kernel,
        grid=(H_MLA, rows // tq),
        in_specs=[pl.BlockSpec((1, tq, QK_DIM), lambda h, i: (h, i, 0)),
                  pl.BlockSpec((1, rows, QK_DIM), lambda h, i: (h, 0, 0)),
                  pl.BlockSpec((1, VT_ROWS, rows), lambda h, i: (h, 0, 0))],
        out_specs=pl.BlockSpec((1, tq, V_DIM), lambda h, i: (h, i, 0)),
        out_shape=jax.ShapeDtypeStruct((H_MLA, rows, V_DIM), BF16),
        scratch_shapes=[pltpu.VMEM((2, MLA_CHAINS, tq // MLA_KSPLIT, tq // MLA_CHAINS), F32),
                        pltpu.VMEM((2, MLA_CHAINS, tq // MLA_KSPLIT, tq // MLA_CHAINS), BF16)],
        compiler_params=_cparams("parallel", "arbitrary"),
        name="mla_attn",
    )(q, k, vt)


def _sb_attn_kernel(q_ref, k_ref, vt_ref, triu_ref, o_ref):
    i = pl.program_id(1)
    nh, tq = q_ref.shape[0], q_ref.shape[1]
    triu = triu_ref[...]
    qs = [q_ref[c] for c in range(nh)]
    before = lax.broadcasted_iota(jnp.int32, (tq, tq), 0) < lax.broadcasted_iota(jnp.int32, (tq, tq), 1)

    def process(tiles, cs, accs):
        starts = [pl.multiple_of(j * tq, tq) for j, _, _ in tiles]
        zs, lks, bts = {}, {}, {}
        for t in range(len(tiles)):
            for c in range(nh):
                zs[t, c] = _dot_nt(k_ref[c, pl.ds(starts[t], tq), :], qs[c])
        for t, (_, diag, live) in enumerate(tiles):
            for c in range(nh):
                lk = -_softplus(zs[t, c])
                if diag:
                    lk = jnp.where(before, lk, 0.0)
                if live is not None:
                    lk = jnp.where(live, lk, 0.0)
                lks[t, c] = lk
        for key, lk in lks.items():
            hi, lo = _split(lk)
            bts[key] = _dot(triu, hi) + _dot(triu, lo)
        cs, accs = list(cs), list(accs)
        for t, (_, diag, live) in enumerate(tiles):
            for c in range(nh):
                a = jnp.exp(zs[t, c] + lks[t, c] + bts[t, c] + cs[c])
                if diag:
                    a = jnp.where(before, a, 0.0)
                if live is not None:
                    a = jnp.where(live, a, 0.0)
                accs[c] = accs[c] + _dot(vt_ref[c, :, pl.ds(starts[t], tq)], a.astype(BF16))
                cs[c] = cs[c] + jnp.sum(lks[t, c], axis=0, keepdims=True)
        return cs, accs

    def cmax(cs):
        out = jnp.max(cs[0])
        for c in cs[1:]:
            out = jnp.maximum(out, jnp.max(c))
        return out

    cs = [jnp.zeros((1, tq), F32)] * nh
    accs = [jnp.zeros((VT_ROWS, tq), F32)] * nh
    cs, accs = process([(i, True, None), (jnp.maximum(i - 1, 0), False, i > 0)], cs, accs)

    def cond(carry):
        return jnp.logical_and(carry[0] >= 0, carry[1] > SB_UNDERFLOW)

    def body(carry):
        j = carry[0]
        cs, accs = process([(j, False, None)], carry[2:2 + nh], carry[2 + nh:])
        return (j - 1, cmax(cs)) + tuple(cs) + tuple(accs)

    carry = lax.while_loop(cond, body, (i - 2, cmax(cs)) + tuple(cs) + tuple(accs))
    for c in range(nh):
        o_ref[c] = carry[2 + nh + c].T[:, :SB_DIM].astype(BF16)


def _tri(n):
    return jnp.asarray(np.tril(np.ones((n, n), np.float32), -1), BF16)


def _sb_attn(q, k, vt):
    rows = q.shape[1]
    tq = min(TQ_SB, rows)
    nh = SB_HEADS
    return pl.pallas_call(
        _sb_attn_kernel,
        grid=(H_SB // nh, rows // tq),
        in_specs=[pl.BlockSpec((nh, tq, SB_DIM), lambda h, i: (h, i, 0)),
                  pl.BlockSpec((nh, rows, SB_DIM), lambda h, i: (h, 0, 0)),
                  pl.BlockSpec((nh, VT_ROWS, rows), lambda h, i: (h, 0, 0)),
                  _full((tq, tq))],
        out_specs=pl.BlockSpec((nh, tq, SB_DIM), lambda h, i: (h, i, 0)),
        out_shape=jax.ShapeDtypeStruct((H_SB, rows, SB_DIM), BF16),
        compiler_params=_cparams("parallel", "arbitrary"),
        name="sb_attn",
    )(q, k, vt, _tri(tq).T)


def _mem_kv_kernel(mem_ref, gmem_ref, w_ref, gk_ref, mk_ref, mv_ref):
    mn = (_rms(mem_ref[...]) * gmem_ref[...]).astype(BF16)
    kv = _dot(mn, w_ref[...])
    ks = [_rms(kv[:, hd * MEM_DIM:(hd + 1) * MEM_DIM]) for hd in range(H_MEM)]
    mk_ref[...] = jnp.concatenate(ks, axis=-1) * gk_ref[...]
    mv_ref[...] = kv[:, MEM_W:]


def _mem_kv(mem2d, p):
    n = mem2d.shape[0]
    return pl.pallas_call(
        _mem_kv_kernel,
        grid=(1,),
        in_specs=[_full((n, D_MODEL)), _full((1, D_MODEL)), _full((D_MODEL, 2 * MEM_W)), _full((1, MEM_W))],
        out_specs=(_full((n, MEM_W)), _full((n, MEM_W))),
        out_shape=(jax.ShapeDtypeStruct((n, MEM_W), F32), jax.ShapeDtypeStruct((n, MEM_W), F32)),
        compiler_params=_cparams("arbitrary"),
        name="mem_kv",
    )(mem2d, p["g_mem"], p["w_mem"], p["g_mem_k"])


def _mem_heads(mq, mk, mv):
    outs = []
    for hd in range(H_MEM):
        sl = slice(hd * MEM_DIM, (hd + 1) * MEM_DIM)
        s = _dot_nt(mq[:, sl], mk[:, sl])
        pr = jnp.exp(s - jnp.max(s, axis=-1, keepdims=True))
        o = _dot(pr.astype(BF16), mv[:, sl])
        outs.append(o / jnp.sum(pr, axis=-1, keepdims=True))
    return jnp.concatenate(outs, axis=-1)


def _mem_attn_kernel(mq_ref, mk_ref, mv_ref, o_ref):
    o_ref[...] = _mem_heads(mq_ref[...], mk_ref[...].astype(BF16), mv_ref[...].astype(BF16)).astype(BF16)


def _mem_attn(mq, mk, mv):
    rows = mq.shape[0]
    tm = min(TM_MEM, rows)
    n = mk.shape[0]
    return pl.pallas_call(
        _mem_attn_kernel,
        grid=(rows // tm,),
        in_specs=[pl.BlockSpec((tm, MEM_W), lambda i: (i, 0)), _full((n, MEM_W)), _full((n, MEM_W))],
        out_specs=pl.BlockSpec((tm, MEM_W), lambda i: (i, 0)),
        out_shape=jax.ShapeDtypeStruct((rows, MEM_W), BF16),
        compiler_params=_cparams("parallel"),
        name="mem_attn",
    )(mq, mk, mv)


def _sample_attn_kernel(past, q_ref, kn_ref, vn_ref, sq_ref, skn_ref, svn_ref, mq_ref,
                        clat_ref, crope_ref, csk_ref, csv_ref, cmk_ref, cmv_ref,
                        wkt_ref, wkn_ref, ind_ref, ones_ref, erow_ref, gkn_ref, gkr_ref, wvh_ref,
                        tri_ref, tris_ref, omla_ref, osb_ref, omem_ref):
    ds = q_ref.shape[1]
    hs = H_MLA * ds
    tc = tri_ref.shape[0]
    row = lax.broadcasted_iota(jnp.int32, (hs, 1), 0)
    qpos = past + (row & (ds - 1))
    rows_of = lambda x, hd: x[hd * ds:(hd + 1) * ds]

    latb = clat_ref[0, 0].astype(BF16)
    rope = crope_ref[0, 0]
    ropeb = rope.astype(BF16)
    kn = _dot(latb, wkn_ref[...])
    ssq = _dot((kn * kn).astype(BF16), ind_ref[...]) + _dot((rope * rope).astype(BF16), ones_ref[...])
    inv_hi, inv_lo = _split(lax.rsqrt(ssq * (1.0 / QK_DIM) + EPS))
    inv_rows = _dot_nt(erow_ref[...], inv_hi) + _dot_nt(erow_ref[...], inv_lo)
    qabs, qrope, s2 = [], [], []
    for hd in range(H_MLA):
        qh = q_ref[hd].astype(F32)
        qabs.append(_dot((qh[:, :NOPE_DIM] * gkn_ref[...]).astype(BF16), wkt_ref[hd]))
        qrope.append((qh[:, NOPE_DIM:] * gkr_ref[...]).astype(BF16))
        s2.append(_dot_nt(q_ref[hd], kn_ref[hd]))
    s1 = _dot_nt(jnp.concatenate(qabs, axis=0).astype(BF16), latb) + _dot_nt(jnp.concatenate(qrope, axis=0), ropeb)
    s1 = s1 * inv_rows
    s2 = jnp.concatenate(s2, axis=0)
    q_chunk = qpos // CHUNK
    s1 = jnp.where((lax.broadcasted_iota(jnp.int32, s1.shape, 1) // CHUNK) <= q_chunk, s1, NEG_BIG)
    s2 = jnp.where(((past + lax.broadcasted_iota(jnp.int32, s2.shape, 1)) // CHUNK) <= q_chunk, s2, NEG_BIG)
    m = jnp.maximum(jnp.max(s1, axis=-1, keepdims=True), jnp.max(s2, axis=-1, keepdims=True))
    p1 = jnp.exp2(s1 - m)
    p2 = jnp.exp2(s2 - m)
    den = jnp.sum(p1, axis=-1, keepdims=True) + jnp.sum(p2, axis=-1, keepdims=True)
    olat = _dot(p1.astype(BF16), latb)
    for hd in range(H_MLA):
        o = _dot(rows_of(olat, hd).astype(BF16), wvh_ref[hd]) + _dot(rows_of(p2, hd).astype(BF16), vn_ref[hd])
        omla_ref[hd] = (o / rows_of(den, hd)).astype(BF16)

    z2 = jnp.concatenate([_dot_nt(sq_ref[hd], skn_ref[hd]) for hd in range(H_SB)], axis=0)
    before_n = lax.broadcasted_iota(jnp.int32, z2.shape, 1) < (row & (ds - 1))
    l2 = jnp.where(before_n, -_softplus(z2), 0.0)
    a2 = jnp.where(before_n, jnp.exp(z2 + l2 + _dot2(l2, tris_ref[...])), 0.0)
    c = jnp.sum(l2, axis=-1, keepdims=True)
    outs = [_dot(rows_of(a2, hd).astype(BF16), svn_ref[hd]) for hd in range(H_SB)]
    tri = tri_ref[...]

    def cached_chunk(start, c, outs):
        rows_j = pl.ds(start, tc)
        z = jnp.concatenate(
            [_dot_nt(sq_ref[hd], csk_ref[0, 0, rows_j, hd, :].astype(BF16)) for hd in range(H_SB)], axis=0)
        lc = -_softplus(z)
        a = jnp.exp(z + lc + _dot2(lc, tri) + c).astype(BF16)
        outs = [outs[hd] + _dot(rows_of(a, hd), csv_ref[0, 0, rows_j, hd, :].astype(BF16)) for hd in range(H_SB)]
        return c + jnp.sum(lc, axis=-1, keepdims=True), outs

    c, outs = cached_chunk(past - tc, c, outs)

    def cond(carry):
        return jnp.logical_and(carry[0] >= 0, carry[1] > SB_UNDERFLOW)

    def body(carry):
        c, outs = cached_chunk(pl.multiple_of(carry[0] * tc, tc), carry[2], list(carry[3:]))
        return (carry[0] - 1, jnp.max(c), c) + tuple(outs)

    carry = lax.while_loop(cond, body, (past // tc - 2, jnp.max(c), c) + tuple(outs))
    for hd in range(H_SB):
        osb_ref[hd] = carry[3 + hd].astype(BF16)

    omem_ref[...] = _mem_heads(mq_ref[...], cmk_ref[0, 0].astype(BF16), cmv_ref[0, 0].astype(BF16)).astype(BF16)


def _sample_attn(q, kn, vn, sq, skn, svn, mq, c_lat, c_rope, c_sbk, c_sbv, c_mk, c_mv, layer, p):
    nb, past = c_lat.shape[1], c_lat.shape[2]
    ds = q.shape[1] // nb
    assert ds & (ds - 1) == 0, "row -> query index uses a bit mask"
    n_mem = c_mk.shape[2]
    tc = min(256, past)
    assert past % tc == 0
    hs = H_MLA * ds
    erow = np.zeros((hs, LANES), np.float32)
    erow[np.arange(hs), np.arange(hs) // ds] = 1.0
    hm = lambda d: pl.BlockSpec((H_MLA, ds, d), lambda b: (0, b, 0))
    sb_cache = pl.BlockSpec((1, 1, past, H_SB, SB_DIM), lambda b: (layer, b, 0, 0, 0))
    cache = lambda n, w: pl.BlockSpec((1, 1, n, w), lambda b: (layer, b, 0, 0))
    rows = q.shape[1]
    return pl.pallas_call(
        functools.partial(_sample_attn_kernel, past),
        grid=(nb,),
        in_specs=[hm(QK_DIM), hm(QK_DIM), hm(V_DIM), hm(SB_DIM), hm(SB_DIM), hm(SB_DIM),
                  pl.BlockSpec((ds, MEM_W), lambda b: (b, 0)),
                  cache(past, KV_LORA), cache(past, ROPE_DIM), sb_cache, sb_cache,
                  cache(n_mem, MEM_W), cache(n_mem, MEM_W),
                  _full((H_MLA, NOPE_DIM, KV_LORA)), _full((KV_LORA, H_MLA * NOPE_DIM)),
                  _full((H_MLA * NOPE_DIM, LANES)), _full((ROPE_DIM, LANES)), _full((hs, LANES)),
                  _full((1, NOPE_DIM)), _full((1, ROPE_DIM)), _full((H_MLA, KV_LORA, V_DIM)),
                  _full((tc, tc)), _full((ds, ds))],
        out_specs=(hm(V_DIM), hm(SB_DIM), pl.BlockSpec((ds, MEM_W), lambda b: (b, 0))),
        out_shape=(jax.ShapeDtypeStruct((H_MLA, rows, V_DIM), BF16),
                   jax.ShapeDtypeStruct((H_SB, rows, SB_DIM), BF16),
                   jax.ShapeDtypeStruct((rows, MEM_W), BF16)),
        compiler_params=_cparams("parallel"),
        name="sample_attn",
    )(q, kn, vn, sq, skn, svn, mq, c_lat, c_rope, c_sbk, c_sbv, c_mk, c_mv,
      p["w_k_t"], p["w_k_nope"], p["ind64"], p["ones_rope"], jnp.asarray(erow, BF16), p["g_k_nope"], p["g_k_rope"],
      p["w_v_h"], _tri(tc), _tri(ds))


def _merge_kernel(x_ref, gattn_ref, wg_ref, omla_ref, osb_ref, omem_ref, woa_ref, wob_ref, wom_ref, wout_ref,
                  gffn_ref, wrh_ref, wrl_ref, br_ref, x1_ref, idx_ref, gate_ref):
    x = x_ref[...]
    h = (_rms(x) * gattn_ref[...]).astype(BF16)
    g = 1.0 / (1.0 + jnp.exp(-_dot(h, wg_ref[...])))
    ua = _dot(jnp.concatenate([omla_ref[hd] for hd in range(H_MLA)], axis=-1), woa_ref[...])
    ub = _dot(jnp.concatenate([osb_ref[hd] for hd in range(H_SB)], axis=-1), wob_ref[...])
    um = _dot(omem_ref[...], wom_ref[...])
    u = g[:, :D_MODEL] * ua + g[:, D_MODEL:2 * D_MODEL] * ub + g[:, 2 * D_MODEL:] * um
    x1 = x + _dot(u.astype(BF16), wout_ref[...])
    x1_ref[...] = x1
    xh, xl = _split(_rms(x1) * gffn_ref[...])
    lg = _dot(xh, wrh_ref[...]) + _dot(xh, wrl_ref[...]) + _dot(xl, wrh_ref[...]) + br_ref[...]
    lane = lax.broadcasted_iota(jnp.int32, lg.shape, 1).astype(F32)
    vals, ids = [], []
    for _ in range(TOP_K):
        m = jnp.max(lg, axis=-1, keepdims=True)
        sel = jnp.min(jnp.where(lg == m, lane, float(LANES)), axis=-1, keepdims=True)
        vals.append(m)
        ids.append(sel)
        lg = jnp.where(lane == sel, NEG_BIG, lg)
    es = [jnp.exp(v - vals[0]) for v in vals]
    den = es[0] + es[1] + es[2] + es[3]
    idx_o = jnp.zeros(lg.shape, F32)
    gate_o = jnp.zeros(lg.shape, F32)
    for k in range(TOP_K):
        idx_o = jnp.where(lane == float(k), ids[k], idx_o)
        gate_o = jnp.where(lane == float(k), es[k] / den, gate_o)
    idx_ref[...] = idx_o.astype(jnp.int32)
    gate_ref[...] = gate_o


def _merge(x2d, omla, osb, omem, p):
    rows = x2d.shape[0]
    tm = TM_MERGE
    row = lambda w: pl.BlockSpec((tm, w), lambda i: (i, 0))
    hm = lambda d: pl.BlockSpec((H_MLA, tm, d), lambda i: (0, i, 0))
    return pl.pallas_call(
        _merge_kernel,
        grid=(rows // tm,),
        in_specs=[row(D_MODEL), _full((1, D_MODEL)), _full((D_MODEL, 3 * D_MODEL)), hm(V_DIM), hm(SB_DIM),
                  row(MEM_W), _full((MLA_W, D_MODEL)), _full((SB_W, D_MODEL)),
                  _full((MEM_W, D_MODEL)), _full((D_MODEL, D_MODEL)), _full((1, D_MODEL)),
                  _full((D_MODEL, LANES)), _full((D_MODEL, LANES)), _full((1, LANES))],
        out_specs=(row(D_MODEL), row(LANES), row(LANES)),
        out_shape=(jax.ShapeDtypeStruct((rows, D_MODEL), F32), jax.ShapeDtypeStruct((rows, LANES), jnp.int32),
                   jax.ShapeDtypeStruct((rows, LANES), F32)),
        compiler_params=_cparams("parallel"),
        name="merge",
    )(x2d, p["g_attn"], p["w_g"], omla, osb, omem, p["w_o_mla"], p["w_o_sb"], p["w_o_mem"], p["w_out"],
      p["g_ffn"], p["w_r_hi"], p["w_r_lo"], p["b_r"])


def _row_dma(src, src_row, dst, dst_row, sem):
    return pltpu.make_async_copy(src.at[pl.ds(src_row, 1), :], dst.at[pl.ds(dst_row, 1), :], sem)


def _row_dma_wait(src, dst, sem, n):
    def body(r, carry):
        _row_dma(src, 0, dst, 0, sem).wait()
        return carry

    lax.fori_loop(0, n, body, 0, unroll=DMA_UNROLL)


def _dispatch_kernel(dest_ref, x_ref, xs_in, xs_out, sem):
    del xs_in
    n = dest_ref.shape[2]
    for r in range(n):
        _row_dma(x_ref, r // TOP_K, xs_out, dest_ref[0, 0, r], sem.at[0]).start(priority=r % DMA_THREADS)
    _row_dma_wait(x_ref, xs_out, sem.at[0], n)


def _moe_dispatch(x1, dest, xs):
    rows = x1.shape[0]
    tt = TT_COMBINE
    nt = rows // tt
    return pl.pallas_call(
        _dispatch_kernel,
        grid=(nt,),
        in_specs=[pl.BlockSpec((1, 1, TOP_K * tt), lambda t: (t, 0, 0), memory_space=pltpu.SMEM),
                  pl.BlockSpec((tt, D_MODEL), lambda t: (t, 0)),
                  pl.BlockSpec(memory_space=pl.ANY)],
        out_specs=pl.BlockSpec(memory_space=pl.ANY),
        out_shape=jax.ShapeDtypeStruct(xs.shape, xs.dtype),
        input_output_aliases={2: 0},
        scratch_shapes=[pltpu.SemaphoreType.DMA((1,))],
        compiler_params=_cparams("arbitrary"),
        name="moe_dispatch",
    )(dest.reshape(nt, 1, TOP_K * tt), x1, xs)


def _moe_kernel(be_ref, nact_ref, xs_ref, gffn_ref, wgu_ref, bgu_ref, wd_ref, bd_ref, o_ref, wgu_sc, wd_sc):
    b = pl.program_id(0)

    @pl.when(b >= nact_ref[0])
    def _():
        o_ref[...] = jnp.zeros(o_ref.shape, o_ref.dtype)

    @pl.when(b < nact_ref[0])
    def _():
        changed = jnp.logical_or(b == 0, be_ref[b] != be_ref[jnp.maximum(b - 1, 0)])

        @pl.when(changed)
        def _():
            wgu_sc[...] = wgu_ref[0].astype(BF16)
            wd_sc[...] = wd_ref[0].astype(BF16)

        xb = (_rms(xs_ref[...]) * gffn_ref[...]).astype(BF16)
        gu = _dot(xb, wgu_sc[...]) + bgu_ref[0]
        g = jnp.minimum(gu[:, :D_FF], SWIGLU_LIMIT)
        u = jnp.clip(gu[:, D_FF:], -SWIGLU_LIMIT, SWIGLU_LIMIT)
        hid = (u + 1.0) * (g / (1.0 + jnp.exp(-SWIGLU_ALPHA * g)))
        o_ref[...] = _dot(hid.astype(BF16), wd_sc[...]) + bd_ref[0]


def _moe_experts(xs, block_expert, n_active, p):
    n_blocks = block_expert.shape[0]
    bm = BM_MOE
    blk = lambda b, be, na: (jnp.minimum(b, na[0] - 1), 0)
    grid_spec = pltpu.PrefetchScalarGridSpec(
        num_scalar_prefetch=2,
        grid=(n_blocks,),
        in_specs=[
            pl.BlockSpec((bm, D_MODEL), blk),
            pl.BlockSpec((1, D_MODEL), lambda b, be, na: (0, 0)),
            pl.BlockSpec((1, D_MODEL, 2 * D_FF), lambda b, be, na: (be[b], 0, 0)),
            pl.BlockSpec((1, 1, 2 * D_FF), lambda b, be, na: (be[b], 0, 0)),
            pl.BlockSpec((1, D_FF, D_MODEL), lambda b, be, na: (be[b], 0, 0)),
            pl.BlockSpec((1, 1, D_MODEL), lambda b, be, na: (be[b], 0, 0)),
        ],
        out_specs=pl.BlockSpec((bm, D_MODEL), lambda b, be, na: (b, 0)),
        scratch_shapes=[pltpu.VMEM((D_MODEL, 2 * D_FF), BF16), pltpu.VMEM((D_FF, D_MODEL), BF16)],
    )
    return pl.pallas_call(
        _moe_kernel,
        grid_spec=grid_spec,
        out_shape=jax.ShapeDtypeStruct((n_blocks * bm, D_MODEL), F32),
        compiler_params=_cparams("arbitrary"),
        name="moe_experts",
    )(block_expert, n_active, xs, p["g_ffn"], p["w_gate_up"], p["b_gate_up"], p["w_down"], p["b_down"])


def _combine_gather(y_hbm, pos_ref, dst, sem, n):
    for r in range(n):
        _row_dma(y_hbm, pos_ref[0, 0, r], dst, r, sem).start(priority=r % DMA_THREADS)


def _combine_kernel(nta, pos_ref, posn_ref, y_hbm, x1a_ref, ga_ref, x1b_ref, gb_ref, oa_ref, ob_ref, buf, sem):
    t = pl.program_id(0)
    nt = pl.num_programs(0)
    tt = oa_ref.shape[0]
    n = TOP_K * tt
    slot = t % 2

    @pl.when(t == 0)
    def _():
        _combine_gather(y_hbm, pos_ref, buf.at[0], sem.at[0], n)

    @pl.when(t + 1 < nt)
    def _():
        _combine_gather(y_hbm, posn_ref, buf.at[1 - slot], sem.at[1 - slot], n)

    _row_dma_wait(y_hbm, buf.at[slot], sem.at[slot], n)

    def combine(x1_ref, gate_ref, o_ref):
        acc = x1_ref[...]
        gate = gate_ref[...]
        for k in range(TOP_K):
            acc = acc + gate[:, k:k + 1] * buf[slot, k * tt:(k + 1) * tt, :]
        o_ref[...] = acc

    @pl.when(t < nta)
    def _():
        combine(x1a_ref, ga_ref, oa_ref)

    @pl.when(t >= nta)
    def _():
        combine(x1b_ref, gb_ref, ob_ref)


def _moe_combine(x1a, gate_a, x1b, gate_b, yb, pos):
    tt = TT_COMBINE
    nta, ntb = x1a.shape[0] // tt, x1b.shape[0] // tt
    nt = nta + ntb
    pos3 = pos.reshape(nt, tt, TOP_K).transpose(0, 2, 1).reshape(nt, 1, TOP_K * tt)
    ia = lambda t: (jnp.minimum(t, nta - 1), 0)
    ib = lambda t: (jnp.maximum(t - nta, 0), 0)
    return pl.pallas_call(
        functools.partial(_combine_kernel, nta),
        grid=(nt,),
        in_specs=[pl.BlockSpec((1, 1, TOP_K * tt), lambda t: (t, 0, 0), memory_space=pltpu.SMEM),
                  pl.BlockSpec((1, 1, TOP_K * tt), lambda t: (jnp.minimum(t + 1, nt - 1), 0, 0),
                               memory_space=pltpu.SMEM),
                  pl.BlockSpec(memory_space=pl.ANY),
                  pl.BlockSpec((tt, D_MODEL), ia), pl.BlockSpec((tt, LANES), ia),
                  pl.BlockSpec((tt, D_MODEL), ib), pl.BlockSpec((tt, LANES), ib)],
        out_specs=(pl.BlockSpec((tt, D_MODEL), ia), pl.BlockSpec((tt, D_MODEL), ib)),
        out_shape=(jax.ShapeDtypeStruct(x1a.shape, F32), jax.ShapeDtypeStruct(x1b.shape, F32)),
        scratch_shapes=[pltpu.VMEM((2, TOP_K * tt, D_MODEL), F32), pltpu.SemaphoreType.DMA((2,))],
        compiler_params=_cparams("arbitrary"),
        name="moe_combine",
    )(pos3, pos3, yb, x1a, gate_a, x1b, gate_b)


def _moe_route(idx):
    rows = idx.shape[0]
    n = rows * TOP_K
    bm = BM_MOE
    e = idx.reshape(n)
    onehot = (e[:, None] == jnp.arange(N_EXPERTS, dtype=jnp.int32)[None, :]).astype(jnp.int32)
    csum = jnp.cumsum(onehot, axis=0)
    rank = jnp.sum(csum * onehot, axis=1) - 1
    counts = csum[-1]
    padded = (counts + bm - 1) // bm * bm
    pend = jnp.cumsum(padded)
    dest = ((pend - padded)[e] + rank).astype(jnp.int32).reshape(rows, TOP_K)
    n_blocks = -(-n // bm) + N_EXPERTS
    starts = jnp.arange(n_blocks, dtype=jnp.int32) * bm
    block_expert = jnp.minimum(
        jnp.sum((pend[None, :] <= starts[:, None]).astype(jnp.int32), axis=1), N_EXPERTS - 1).astype(jnp.int32)
    n_active = (pend[-1:] // bm).astype(jnp.int32)
    return dest, block_expert, n_active


def _pack_layer(l, g_attn, w_in, g_q_lat, w_q_b, g_q_nope, g_q_rope, g_kv_lat, w_kv_b, g_k_nope, g_k_rope, g_mem,
                w_mem_kv, g_mem_q, g_mem_k, w_o_mla, w_o_sb, w_o_mem, w_out, g_ffn, w_router, b_router,
                w_gate_up, b_gate_up, w_down, b_down):
    w = w_in[l]
    off = np.cumsum((Q_LORA, KV_LORA, ROPE_DIM, SB_W, SB_W, SB_W, MEM_W))
    o_kpe, o_sbq = int(off[1]), int(off[2])
    o_gate = int(off[6])
    kpe1 = w[:, o_kpe:o_kpe + ROPE_HALF]
    kpe2 = w[:, o_kpe + ROPE_HALF:o_kpe + ROPE_DIM]
    zl = jnp.zeros((D_MODEL, KPE_LANE), F32)
    zr = jnp.zeros((D_MODEL, LANES - KPE_LANE - ROPE_DIM), F32)
    w_a = jnp.concatenate([w[:, :o_kpe], zl, kpe1, kpe2, zr, zl, -kpe2, kpe1, zr, w[:, o_sbq:o_gate]], axis=1)
    wq = w_q_b[l]
    z_n = jnp.zeros((Q_LORA, H_MLA, NOPE_DIM), F32)
    wq_sw = jnp.concatenate([z_n, -wq[..., NOPE_DIM + ROPE_HALF:], wq[..., NOPE_DIM:NOPE_DIM + ROPE_HALF]], axis=-1)
    lane = np.arange(QK_W)
    ind = np.zeros((QK_W, LANES), np.float32)
    ind[lane, lane // QK_DIM] = 1.0
    ind64 = np.zeros((H_MLA * NOPE_DIM, LANES), np.float32)
    ind64[np.arange(H_MLA * NOPE_DIM), np.arange(H_MLA * NOPE_DIM) // NOPE_DIM] = 1.0
    pk = np.zeros((ROPE_DIM, QK_W), np.float32)
    for hd in range(H_MLA):
        pk[np.arange(ROPE_DIM), hd * QK_DIM + NOPE_DIM + np.arange(ROPE_DIM)] = 1.0
    wkv = w_kv_b[l]
    w_k = jnp.concatenate([wkv[..., :NOPE_DIM], jnp.zeros((KV_LORA, H_MLA, ROPE_DIM), F32)], axis=-1)
    g_q = jnp.tile(jnp.concatenate([g_q_nope[l], g_q_rope[l], g_q_rope[l]]), H_MLA) * (QK_DIM ** -0.5 * LOG2E)
    g_k = jnp.tile(jnp.concatenate([g_k_nope[l], g_k_rope[l], g_k_rope[l]]), H_MLA)
    wm = w_mem_kv[l]
    w_mem = jnp.concatenate([wm[..., :MEM_DIM].reshape(D_MODEL, MEM_W), wm[..., MEM_DIM:].reshape(D_MODEL, MEM_W)], 1)
    w_r = jnp.concatenate([w_router[l], jnp.zeros((D_MODEL, LANES - N_EXPERTS), F32)], axis=1)
    w_r_hi = w_r.astype(BF16)
    b_r = jnp.concatenate([b_router[l].astype(F32), jnp.full((LANES - N_EXPERTS,), NEG_BIG, F32)])
    return {
        "g_attn": g_attn[l][None], "w_a": w_a.astype(BF16), "g_q_lat": g_q_lat[l][None],
        "w_q": wq.reshape(Q_LORA, QK_W).astype(BF16), "w_q_sw": wq_sw.reshape(Q_LORA, QK_W).astype(BF16),
        "ind96": jnp.asarray(ind, BF16), "ind96_t": jnp.asarray(ind.T, BF16), "g_q": g_q[None],
        "g_kv_lat": g_kv_lat[l][None], "g_mem_q": jnp.tile(g_mem_q[l], H_MEM)[None] * (MEM_DIM ** -0.5),
        "w_k": w_k.reshape(KV_LORA, QK_W).astype(BF16), "p_kpe": jnp.asarray(pk, BF16), "g_k": g_k[None],
        "w_v": wkv[..., NOPE_DIM:].reshape(KV_LORA, MLA_W).astype(BF16),
        "w_v_t": wkv[..., NOPE_DIM:].reshape(KV_LORA, MLA_W).T.astype(BF16),
        "w_v_h": wkv[..., NOPE_DIM:].transpose(1, 0, 2).astype(BF16),
        "w_k_t": wkv[..., :NOPE_DIM].transpose(1, 2, 0).astype(BF16),
        "w_k_nope": wkv[..., :NOPE_DIM].reshape(KV_LORA, H_MLA * NOPE_DIM).astype(BF16),
        "ind64": jnp.asarray(ind64, BF16), "ones_rope": jnp.ones((ROPE_DIM, LANES), BF16),
        "g_k_nope": g_k_nope[l][None], "g_k_rope": jnp.concatenate([g_k_rope[l], g_k_rope[l]])[None],
        "g_mem": g_mem[l][None], "w_mem": w_mem.astype(BF16), "g_mem_k": jnp.tile(g_mem_k[l], H_MEM)[None],
        "w_g": w[:, o_gate:].astype(BF16), "w_sbv_t": w[:, o_gate - MEM_W - SB_W:o_gate - MEM_W].T.astype(BF16),
        "w_o_mla": w_o_mla[l].astype(BF16), "w_o_sb": w_o_sb[l].astype(BF16),
        "w_o_mem": w_o_mem[l].astype(BF16), "w_out": w_out[l].astype(BF16), "g_ffn": g_ffn[l][None],
        "w_r_hi": w_r_hi, "w_r_lo": (w_r - w_r_hi.astype(F32)).astype(BF16), "b_r": b_r[None],
        "w_gate_up": w_gate_up[l], "b_gate_up": b_gate_up[l][:, None, :], "w_down": w_down[l],
        "b_down": b_down[l][:, None, :],
    }


def kernel(x_prompt, x_sample, mem_prompt, cache_mla_latent, cache_mla_rope, cache_sb_k, cache_sb_v, cache_mem_k, cache_mem_v, g_attn, w_in, g_q_lat, w_q_b, g_q_nope, g_q_rope, g_kv_lat, w_kv_b, g_k_nope, g_k_rope, g_mem, w_mem_kv, g_mem_q, g_mem_k, w_o_mla, w_o_sb, w_o_mem, w_out, g_ffn, w_router, b_router, w_gate_up, b_gate_up, w_down, b_down):
    depth = g_attn.shape[0]
    bp, sp, _ = x_prompt.shape
    bs, ss, _ = x_sample.shape
    past = cache_mla_latent.shape[2]
    n_mem = mem_prompt.shape[1]
    assert bp == 1 and sp % TQ_MLA == 0 and sp % TM_KV == 0 and (bs * ss) % TM_PROJ == 0 and TM_PROJ % ss == 0
    rows_s = bs * ss
    c_sbk, c_sbv = cache_sb_k, cache_sb_v
    c_mk = cache_mem_k.reshape(depth, bs, n_mem, MEM_W)
    c_mv = cache_mem_v.reshape(depth, bs, n_mem, MEM_W)
    xp = x_prompt.reshape(sp, D_MODEL)
    xs = x_sample.reshape(rows_s, D_MODEL)
    mem2d = mem_prompt.reshape(n_mem, D_MODEL)
    base_p, off_p = np.arange(sp // TM_PROJ) * TM_PROJ, np.arange(TM_PROJ)
    base_s, off_s = np.full((rows_s // TM_PROJ,), past), np.arange(TM_PROJ) % ss
    outs = [[] for _ in range(10)]
    for l in range(depth):
        p = _pack_layer(l, g_attn, w_in, g_q_lat, w_q_b, g_q_nope, g_q_rope, g_kv_lat, w_kv_b, g_k_nope, g_k_rope,
                        g_mem, w_mem_kv, g_mem_q, g_mem_k, w_o_mla, w_o_sb, w_o_mem, w_out, g_ffn, w_router,
                        b_router, w_gate_up, b_gate_up, w_down, b_down)
        q, lat, kpe, sq, sk, sv, skh, _, mq, svt = _proj(xp, base_p, off_p, p)
        kh, _, vth = _kv_expand(lat, kpe, p)
        o_mla = _mla_attn(q, kh, vth)
        o_sb = _sb_attn(sq, skh, svt)
        mk, mv = _mem_kv(mem2d, p)
        o_mem = _mem_attn(mq, mk, mv)
        x1p, idx_p, gate_p = _merge(xp, o_mla, o_sb, o_mem, p)
        q_s, lat_s, kpe_s, sq_s, sk_s, sv_s, skh_s, svh_s, mq_s, _ = _proj(xs, base_s, off_s, p)
        kn, vn, _ = _kv_expand(lat_s, kpe_s, p)
        o_mla_s, o_sb_s, o_mem_s = _sample_attn(q_s, kn, vn, sq_s, skh_s, svh_s, mq_s, cache_mla_latent,
                                                cache_mla_rope, c_sbk, c_sbv, c_mk, c_mv, l, p)
        x1s, idx_s, gate_s = _merge(xs, o_mla_s, o_sb_s, o_mem_s, p)
        dest, block_expert, n_active = _moe_route(jnp.concatenate([idx_p[:, :TOP_K], idx_s[:, :TOP_K]], axis=0))
        slots = jnp.zeros((block_expert.shape[0] * BM_MOE, D_MODEL), F32)
        slots = _moe_dispatch(x1s, dest[sp:], _moe_dispatch(x1p, dest[:sp], slots))
        yb = _moe_experts(slots, block_expert, n_active, p)
        xp, xs = _moe_combine(x1p, gate_p, x1s, gate_s, yb, dest)
        for lst, val in zip(outs, (lat.reshape(bp, sp, KV_LORA), kpe.reshape(bp, sp, ROPE_DIM),
                                   sk.reshape(bp, sp, H_SB, SB_DIM), sv.reshape(bp, sp, H_SB, SB_DIM),
                                   mk.reshape(bp, n_mem, H_MEM, MEM_DIM), mv.reshape(bp, n_mem, H_MEM, MEM_DIM),
                                   lat_s.reshape(bs, ss, KV_LORA), kpe_s.reshape(bs, ss, ROPE_DIM),
                                   sk_s.reshape(bs, ss, H_SB, SB_DIM), sv_s.reshape(bs, ss, H_SB, SB_DIM))):
            lst.append(val)
    return (xp.reshape(bp, sp, D_MODEL), xs.reshape(bs, ss, D_MODEL)) + tuple(jnp.stack(o) for o in outs)
```

```python
import functools

import numpy as np
import jax
import jax.numpy as jnp
from jax import lax
from jax.experimental import pallas as pl
from jax.experimental.pallas import tpu as pltpu

F32 = jnp.float32
BF16 = jnp.bfloat16

D_MODEL = 1024
CHUNK = 64
EPS = 1e-6
H_MLA = 8
NOPE_DIM = 64
ROPE_DIM = 32
ROPE_HALF = ROPE_DIM // 2
V_DIM = 64
Q_LORA = 384
KV_LORA = 256
ROPE_BASE = 10000.0
QK_DIM = NOPE_DIM + ROPE_DIM
QK_W = H_MLA * QK_DIM
H_SB = 8
SB_DIM = 64
H_MEM = 4
MEM_DIM = 128
N_EXPERTS = 32
TOP_K = 4
D_FF = 1024
SWIGLU_LIMIT = 7.0
SWIGLU_ALPHA = 1.702
MLA_W = H_MLA * V_DIM
SB_W = H_SB * SB_DIM
MEM_W = H_MEM * MEM_DIM

LANES = 128
VT_ROWS = LANES
MLA_CHAINS = 4
MLA_KSPLIT = 2
assert MLA_KSPLIT % 2 == 0
LOG2E = 1.4426950408889634
A_QLAT = 0
A_KVLAT = A_QLAT + Q_LORA
A_KPE = A_KVLAT + KV_LORA
A_KPE_SW = A_KPE + LANES
A_SBQ = A_KPE_SW + LANES
A_SBK = A_SBQ + SB_W
A_SBV = A_SBK + SB_W
A_MEMQ = A_SBV + SB_W
A_COLS = A_MEMQ + MEM_W
KPE_LANE = NOPE_DIM

TM_PROJ = 256
TM_KV = 512
TQ_MLA = 1024
TQ_SB = 256
SB_HEADS = 2
TM_MEM = 512
TM_MERGE = 512
BM_MOE = 512
TT_COMBINE = 256
DMA_UNROLL = 8
DMA_THREADS = 2
TOP_K_SHIFT = TOP_K.bit_length() - 1
assert 1 << TOP_K_SHIFT == TOP_K
VMEM_LIMIT = 56 * 1024 * 1024

SB_UNDERFLOW = -120.0
NEG_BIG = -3.0e38


def _cparams(*sem):
    return pltpu.CompilerParams(dimension_semantics=sem, vmem_limit_bytes=VMEM_LIMIT)


def _split(x):
    hi = x.astype(BF16)
    lo = (x - hi.astype(F32)).astype(BF16)
    return hi, lo


def _dot(a, b):
    return jnp.dot(a, b, preferred_element_type=F32)


def _dot2(x, m):
    hi, lo = _split(x)
    return _dot(hi, m) + _dot(lo, m)


def _dot_nt(a, b):
    return lax.dot_general(a, b, (((1,), (1,)), ((), ())), preferred_element_type=F32)


def _rms(x):
    return x * lax.rsqrt(jnp.mean(x * x, axis=-1, keepdims=True) + EPS)


def _softplus(z):
    return jnp.maximum(z, 0.0) + jnp.log(1.0 + jnp.exp(-jnp.abs(z)))


def _full(shape):
    n = len(shape)
    return pl.BlockSpec(shape, lambda *_: (0,) * n)


def _proj_kernel(x_ref, gattn_ref, wa_ref, gqlat_ref, wq_ref, wqs_ref, ind_ref, indt_ref, gq_ref, gkv_ref,
                 bcs_ref, ocos_ref, osin_ref, gmq_ref, wsvt_ref,
                 q_ref, lat_ref, kpe_ref, sq_ref, sk_ref, sv_ref, skh_ref, svh_ref, mq_ref, svt_ref):
    h = (_rms(x_ref[...]) * gattn_ref[...]).astype(BF16)
    z = _dot(h, wa_ref[...])
    svt = _dot_nt(wsvt_ref[...], h)
    pad = jnp.zeros((VT_ROWS - SB_DIM, svt.shape[1]), F32)
    for hd in range(H_SB):
        svt_ref[hd] = jnp.concatenate([svt[hd * SB_DIM:(hd + 1) * SB_DIM, :], pad], axis=0).astype(BF16)
    bc = bcs_ref[0, 0:1, :]
    bs = bcs_ref[0, 1:2, :]
    oc = ocos_ref[...]
    osn = osin_ref[...]
    cos_f = bc * oc - bs * osn
    sin_f = bs * oc + bc * osn
    qn = (_rms(z[:, A_QLAT:A_QLAT + Q_LORA]) * gqlat_ref[...]).astype(BF16)
    qr = _dot(qn, wq_ref[...]) * cos_f + _dot(qn, wqs_ref[...]) * sin_f
    ssq = _dot2(qr * qr, ind_ref[...])
    inv = lax.rsqrt(ssq * (1.0 / QK_DIM) + EPS)
    qo = qr * _dot2(inv, indt_ref[...]) * gq_ref[...]
    for hd in range(H_MLA):
        q_ref[hd] = qo[:, hd * QK_DIM:(hd + 1) * QK_DIM].astype(BF16)
    lat_ref[...] = _rms(z[:, A_KVLAT:A_KVLAT + KV_LORA]) * gkv_ref[...]
    kr = z[:, A_KPE:A_KPE + LANES] * cos_f[:, :LANES] + z[:, A_KPE_SW:A_KPE_SW + LANES] * sin_f[:, :LANES]
    kpe_ref[...] = kr[:, KPE_LANE:KPE_LANE + ROPE_DIM]
    sbq = z[:, A_SBQ:A_SBQ + SB_W] * (SB_DIM ** -0.5)
    sbk = z[:, A_SBK:A_SBK + SB_W]
    sbv = z[:, A_SBV:A_SBV + SB_W]
    sk_ref[...] = sbk
    sv_ref[...] = sbv
    for hd in range(H_SB):
        sl = slice(hd * SB_DIM, (hd + 1) * SB_DIM)
        sq_ref[hd] = sbq[:, sl].astype(BF16)
        skh_ref[hd] = sbk[:, sl].astype(BF16)
        svh_ref[hd] = sbv[:, sl].astype(BF16)
    mqs = []
    for hd in range(H_MEM):
        mqs.append(_rms(z[:, A_MEMQ + hd * MEM_DIM:A_MEMQ + (hd + 1) * MEM_DIM]))
    mq_ref[...] = (jnp.concatenate(mqs, axis=-1) * gmq_ref[...]).astype(BF16)


def _rope_tables(base_pos, off_pos):
    lane = np.arange(QK_W) % QK_DIM
    inv_freq = ROPE_BASE ** (-np.arange(ROPE_HALF, dtype=np.float64) / ROPE_HALF)
    freq = np.where(lane >= NOPE_DIM, inv_freq[(lane - NOPE_DIM) % ROPE_HALF], 0.0)
    ab = np.asarray(base_pos, np.float64)[:, None] * freq
    ao = np.asarray(off_pos, np.float64)[:, None] * freq
    bcs = np.stack([np.cos(ab), np.sin(ab)], axis=1).astype(np.float32)
    return jnp.asarray(bcs), jnp.asarray(np.cos(ao), F32), jnp.asarray(np.sin(ao), F32)


def _proj(x2d, base_pos, off_pos, p):
    rows = x2d.shape[0]
    tm = TM_PROJ
    nt = rows // tm
    bcs, ocos, osin = _rope_tables(base_pos, off_pos)
    row = lambda w: pl.BlockSpec((tm, w), lambda i: (i, 0))
    hm = lambda d: pl.BlockSpec((H_MLA, tm, d), lambda i: (0, i, 0))
    out_shape = (
        jax.ShapeDtypeStruct((H_MLA, rows, QK_DIM), BF16),
        jax.ShapeDtypeStruct((rows, KV_LORA), F32),
        jax.ShapeDtypeStruct((rows, ROPE_DIM), F32),
        jax.ShapeDtypeStruct((H_SB, rows, SB_DIM), BF16),
        jax.ShapeDtypeStruct((rows, SB_W), F32),
        jax.ShapeDtypeStruct((rows, SB_W), F32),
        jax.ShapeDtypeStruct((H_SB, rows, SB_DIM), BF16),
        jax.ShapeDtypeStruct((H_SB, rows, SB_DIM), BF16),
        jax.ShapeDtypeStruct((rows, MEM_W), BF16),
        jax.ShapeDtypeStruct((H_SB, VT_ROWS, rows), BF16),
    )
    return pl.pallas_call(
        _proj_kernel,
        grid=(nt,),
        in_specs=[row(D_MODEL), _full((1, D_MODEL)), _full((D_MODEL, A_COLS)), _full((1, Q_LORA)),
                  _full((Q_LORA, QK_W)), _full((Q_LORA, QK_W)), _full((QK_W, LANES)), _full((LANES, QK_W)),
                  _full((1, QK_W)), _full((1, KV_LORA)),
                  pl.BlockSpec((1, 2, QK_W), lambda i: (i, 0, 0)), _full((tm, QK_W)), _full((tm, QK_W)),
                  _full((1, MEM_W)), _full((SB_W, D_MODEL))],
        out_specs=(hm(QK_DIM), row(KV_LORA), row(ROPE_DIM), hm(SB_DIM), row(SB_W), row(SB_W), hm(SB_DIM),
                   hm(SB_DIM), row(MEM_W), pl.BlockSpec((H_SB, VT_ROWS, tm), lambda i: (0, 0, i))),
        out_shape=out_shape,
        compiler_params=_cparams("parallel"),
        name="proj",
    )(x2d, p["g_attn"], p["w_a"], p["g_q_lat"], p["w_q"], p["w_q_sw"], p["ind96"], p["ind96_t"], p["g_q"],
      p["g_kv_lat"], bcs, ocos, osin, p["g_mem_q"], p["w_sbv_t"])


def _expand_keys(lat, kpe, wk, pk, ind, indt, gk):
    kf = _dot(lat.astype(BF16), wk) + _dot2(kpe, pk)
    inv = lax.rsqrt(_dot2(kf * kf, ind) * (1.0 / QK_DIM) + EPS)
    return kf * _dot2(inv, indt) * gk


def _kv_expand_kernel(lat_ref, kpe_ref, wk_ref, pk_ref, ind_ref, indt_ref, gk_ref, wv_ref, wvt_ref,
                      k_ref, v_ref, vt_ref):
    lat = lat_ref[...]
    latb = lat.astype(BF16)
    ko = _expand_keys(lat, kpe_ref[...], wk_ref[...], pk_ref[...], ind_ref[...], indt_ref[...], gk_ref[...])
    v = _dot(latb, wv_ref[...])
    vt = _dot_nt(wvt_ref[...], latb)
    ones_row = (lax.broadcasted_iota(jnp.int32, (VT_ROWS - V_DIM, vt.shape[1]), 0) == 0).astype(F32)
    for hd in range(H_MLA):
        k_ref[hd] = ko[:, hd * QK_DIM:(hd + 1) * QK_DIM].astype(BF16)
        v_ref[hd] = v[:, hd * V_DIM:(hd + 1) * V_DIM].astype(BF16)
        vt_ref[hd] = jnp.concatenate([vt[hd * V_DIM:(hd + 1) * V_DIM, :], ones_row], axis=0).astype(BF16)


def _kv_expand(lat, kpe, p):
    rows = lat.shape[0]
    tm = min(TM_KV, rows)
    row = lambda w: pl.BlockSpec((tm, w), lambda i: (i, 0))
    hm = lambda d: pl.BlockSpec((H_MLA, tm, d), lambda i: (0, i, 0))
    return pl.pallas_call(
        _kv_expand_kernel,
        grid=(rows // tm,),
        in_specs=[row(KV_LORA), row(ROPE_DIM), _full((KV_LORA, QK_W)), _full((ROPE_DIM, QK_W)),
                  _full((QK_W, LANES)), _full((LANES, QK_W)), _full((1, QK_W)), _full((KV_LORA, MLA_W)),
                  _full((MLA_W, KV_LORA))],
        out_specs=(hm(QK_DIM), hm(V_DIM), pl.BlockSpec((H_MLA, VT_ROWS, tm), lambda i: (0, 0, i))),
        out_shape=(jax.ShapeDtypeStruct((H_MLA, rows, QK_DIM), BF16),
                   jax.ShapeDtypeStruct((H_MLA, rows, V_DIM), BF16),
                   jax.ShapeDtypeStruct((H_MLA, VT_ROWS, rows), BF16)),
        compiler_params=_cparams("parallel"),
        name="kv_expand",
    )(lat, kpe, p["w_k"], p["p_kpe"], p["ind96"], p["ind96_t"], p["g_k"], p["w_v"], p["w_v_t"])


def _mla_update(st, vt, m, acc):
    m_new = jnp.maximum(m, jnp.max(st, axis=0, keepdims=True))
    pr = jnp.exp2(st - m_new).astype(BF16)
    return m_new, acc * jnp.exp2(m - m_new) + _dot(vt, pr)


def _mla_step(qc, k, vt, m, acc, mask):
    st = _dot_nt(k, qc)
    if mask is not None:
        st = jnp.where(mask, st, NEG_BIG)
    return _mla_update(st, vt, m, acc)


def _mla_attn_kernel(q_ref, k_ref, vt_ref, o_ref, s_sc, p_sc):
    i = pl.program_id(1)
    tq = q_ref.shape[1]
    nc = MLA_CHAINS
    tc = tq // nc
    tk = tq // MLA_KSPLIT
    qs = [q_ref[0, c * tc:(c + 1) * tc, :] for c in range(nc)]
    key_chunk = lax.broadcasted_iota(jnp.int32, (tk, tc), 0) // CHUNK
    qry_chunk = lax.broadcasted_iota(jnp.int32, (tk, tc), 1) // CHUNK

    def scores(t, slot):
        k = k_ref[0, pl.ds(pl.multiple_of(t * tk, tk), tk), :]
        out = []
        for c in range(nc):
            st = _dot_nt(k, qs[c])
            s_sc[slot, c] = st
            out.append(jnp.max(st, axis=0, keepdims=True))
        return out

    def accumulate(t, slot, alphas, accs):
        vt = vt_ref[0, :, pl.ds(pl.multiple_of(t * tk, tk), tk)]
        return [accs[c] * alphas[c] + _dot(vt, p_sc[slot, c]) for c in range(nc)]

    def step(t, slot, cmax, alphas, ms, accs, last, mask_u):
        nxt = None if last else scores(t + 1, 1 - slot)
        accs = accumulate(jnp.maximum(t - 1, 0), 1 - slot, alphas, accs)
        m_new, alphas = [], []
        for c in range(nc):
            st = s_sc[slot, c]
            if mask_u is None:
                cm = cmax[c]
            else:
                st = jnp.where(key_chunk + mask_u * (tk // CHUNK) <= qry_chunk + c * (tc // CHUNK), st, NEG_BIG)
                cm = jnp.max(st, axis=0, keepdims=True)
            mn = jnp.maximum(ms[c], cm)
            p_sc[slot, c] = jnp.exp2(st - mn).astype(BF16)
            alphas.append(jnp.exp2(ms[c] - mn))
            m_new.append(mn)
        return nxt, alphas, m_new, accs

    p_sc[1] = jnp.zeros(p_sc.shape[1:], BF16)
    init = (tuple(scores(0, 0)) + (jnp.ones((1, tc), F32),) * nc + (jnp.full((1, tc), NEG_BIG, F32),) * nc
            + (jnp.zeros((VT_ROWS, tc), F32),) * nc)
    unpack = lambda carry: [list(carry[g * nc:(g + 1) * nc]) for g in range(4)]

    def body(j, carry):
        cmax, alphas, ms, accs = unpack(carry)
        for u in range(MLA_KSPLIT):
            cmax, alphas, ms, accs = step(j * MLA_KSPLIT + u, u % 2, cmax, alphas, ms, accs, False, None)
        return tuple(cmax) + tuple(alphas) + tuple(ms) + tuple(accs)

    cmax, alphas, ms, accs = unpack(lax.fori_loop(0, i, body, init))
    for u in range(MLA_KSPLIT):
        cmax, alphas, ms, accs = step(i * MLA_KSPLIT + u, u % 2, cmax, alphas, ms, accs, u + 1 == MLA_KSPLIT, u)
    accs = accumulate((i + 1) * MLA_KSPLIT - 1, (MLA_KSPLIT - 1) % 2, alphas, accs)
    for c in range(nc):
        ot = accs[c].T
        o_ref[0, c * tc:(c + 1) * tc, :] = (ot[:, :V_DIM] / ot[:, V_DIM:V_DIM + 1]).astype(BF16)


def _mla_attn(q, k, vt):
    rows = q.shape[1]
    tq = min(TQ_MLA, rows)
    return pl.pallas_call(
        _mla_attn_kernel,
        grid=(H_MLA, rows // tq),
        in_specs=[pl.BlockSpec((1, tq, QK_DIM), lambda h, i: (h, i, 0)),
                  pl.BlockSpec((1, rows, QK_DIM), lambda h, i: (h, 0, 0)),
                  pl.BlockSpec((1, VT_ROWS, rows), lambda h, i: (h, 0, 0))],
        out_specs=pl.BlockSpec((1, tq, V_DIM), lambda h, i: (h, i, 0)),
        out_shape=jax.ShapeDtypeStruct((H_MLA, rows, V_DIM), BF16),
        scratch_shapes=[pltpu.VMEM((2, MLA_CHAINS, tq // MLA_KSPLIT, tq // MLA_CHAINS), F32),
                        pltpu.VMEM((2, MLA_CHAINS, tq // MLA_KSPLIT, tq // MLA_CHAINS), BF16)],
        compiler_params=_cparams("parallel", "arbitrary"),
        name="mla_attn",
    )(q, k, vt)


def _sb_attn_kernel(q_ref, k_ref, vt_ref, triu_ref, o_ref):
    i = pl.program_id(1)
    nh, tq = q_ref.shape[0], q_ref.shape[1]
    triu = triu_ref[...]
    qs = [q_ref[c] for c in range(nh)]
    before = lax.broadcasted_iota(jnp.int32, (tq, tq), 0) < lax.broadcasted_iota(jnp.int32, (tq, tq), 1)

    def process(tiles, cs, accs):
        starts = [pl.multiple_of(j * tq, tq) for j, _, _ in tiles]
        zs, lks, bts = {}, {}, {}
        for t in range(len(tiles)):
            for c in range(nh):
                zs[t, c] = _dot_nt(k_ref[c, pl.ds(starts[t], tq), :], qs[c])
        for t, (_, diag, live) in enumerate(tiles):
            for c in range(nh):
                lk = -_softplus(zs[t, c])
                if diag:
                    lk = jnp.where(before, lk, 0.0)
                if live is not None:
                    lk = jnp.where(live, lk, 0.0)
                lks[t, c] = lk
        for key, lk in lks.items():
            hi, lo = _split(lk)
            bts[key] = _dot(triu, hi) + _dot(triu, lo)
        cs, accs = list(cs), list(accs)
        for t, (_, diag, live) in enumerate(tiles):
            for c in range(nh):
                a = jnp.exp(zs[t, c] + lks[t, c] + bts[t, c] + cs[c])
                if diag:
                    a = jnp.where(before, a, 0.0)
                if live is not None:
                    a = jnp.where(live, a, 0.0)
                accs[c] = accs[c] + _dot(vt_ref[c, :, pl.ds(starts[t], tq)], a.astype(BF16))
                cs[c] = cs[c] + jnp.sum(lks[t, c], axis=0, keepdims=True)
        return cs, accs

    def cmax(cs):
        out = jnp.max(cs[0])
        for c in cs[1:]:
            out = jnp.maximum(out, jnp.max(c))
        return out

    cs = [jnp.zeros((1, tq), F32)] * nh
    accs = [jnp.zeros((VT_ROWS, tq), F32)] * nh
    cs, accs = process([(i, True, None), (jnp.maximum(i - 1, 0), False, i > 0)], cs, accs)

    def cond(carry):
        return jnp.logical_and(carry[0] >= 0, carry[1] > SB_UNDERFLOW)

    def body(carry):
        j = carry[0]
        cs, accs = process([(j, False, None)], carry[2:2 + nh], carry[2 + nh:])
        return (j - 1, cmax(cs)) + tuple(cs) + tuple(accs)

    carry = lax.while_loop(cond, body, (i - 2, cmax(cs)) + tuple(cs) + tuple(accs))
    for c in range(nh):
        o_ref[c] = carry[2 + nh + c].T[:, :SB_DIM].astype(BF16)


def _tri(n):
    return jnp.asarray(np.tril(np.ones((n, n), np.float32), -1), BF16)


def _sb_attn(q, k, vt):
    rows = q.shape[1]
    tq = min(TQ_SB, rows)
    nh = SB_HEADS
    return pl.pallas_call(
        _sb_attn_kernel,
        grid=(H_SB // nh, rows // tq),
        in_specs=[pl.BlockSpec((nh, tq, SB_DIM), lambda h, i: (h, i, 0)),
                  pl.BlockSpec((nh, rows, SB_DIM), lambda h, i: (h, 0, 0)),
                  pl.BlockSpec((nh, VT_ROWS, rows), lambda h, i: (h, 0, 0)),
                  _full((tq, tq))],
        out_specs=pl.BlockSpec((nh, tq, SB_DIM), lambda h, i: (h, i, 0)),
        out_shape=jax.ShapeDtypeStruct((H_SB, rows, SB_DIM), BF16),
        compiler_params=_cparams("parallel", "arbitrary"),
        name="sb_attn",
    )(q, k, vt, _tri(tq).T)


def _mem_kv_kernel(mem_ref, gmem_ref, w_ref, gk_ref, mk_ref, mv_ref):
    mn = (_rms(mem_ref[...]) * gmem_ref[...]).astype(BF16)
    kv = _dot(mn, w_ref[...])
    ks = [_rms(kv[:, hd * MEM_DIM:(hd + 1) * MEM_DIM]) for hd in range(H_MEM)]
    mk_ref[...] = jnp.concatenate(ks, axis=-1) * gk_ref[...]
    mv_ref[...] = kv[:, MEM_W:]


def _mem_kv(mem2d, p):
    n = mem2d.shape[0]
    return pl.pallas_call(
        _mem_kv_kernel,
        grid=(1,),
        in_specs=[_full((n, D_MODEL)), _full((1, D_MODEL)), _full((D_MODEL, 2 * MEM_W)), _full((1, MEM_W))],
        out_specs=(_full((n, MEM_W)), _full((n, MEM_W))),
        out_shape=(jax.ShapeDtypeStruct((n, MEM_W), F32), jax.ShapeDtypeStruct((n, MEM_W), F32)),
        compiler_params=_cparams("arbitrary"),
        name="mem_kv",
    )(mem2d, p["g_mem"], p["w_mem"], p["g_mem_k"])


def _mem_heads(mq, mk, mv):
    outs = []
    for hd in range(H_MEM):
        sl = slice(hd * MEM_DIM, (hd + 1) * MEM_DIM)
        s = _dot_nt(mq[:, sl], mk[:, sl])
        pr = jnp.exp(s - jnp.max(s, axis=-1, keepdims=True))
        o = _dot(pr.astype(BF16), mv[:, sl])
        outs.append(o / jnp.sum(pr, axis=-1, keepdims=True))
    return jnp.concatenate(outs, axis=-1)


def _mem_attn_kernel(mq_ref, mk_ref, mv_ref, o_ref):
    o_ref[...] = _mem_heads(mq_ref[...], mk_ref[...].astype(BF16), mv_ref[...].astype(BF16)).astype(BF16)


def _mem_attn(mq, mk, mv):
    rows = mq.shape[0]
    tm = min(TM_MEM, rows)
    n = mk.shape[0]
    return pl.pallas_call(
        _mem_attn_kernel,
        grid=(rows // tm,),
        in_specs=[pl.BlockSpec((tm, MEM_W), lambda i: (i, 0)), _full((n, MEM_W)), _full((n, MEM_W))],
        out_specs=pl.BlockSpec((tm, MEM_W), lambda i: (i, 0)),
        out_shape=jax.ShapeDtypeStruct((rows, MEM_W), BF16),
        compiler_params=_cparams("parallel"),
        name="mem_attn",
    )(mq, mk, mv)


def _sample_attn_kernel(past, q_ref, kn_ref, vn_ref, sq_ref, skn_ref, svn_ref, mq_ref,
                        clat_ref, crope_ref, csk_ref, csv_ref, cmk_ref, cmv_ref,
                        wkt_ref, wkn_ref, ind_ref, ones_ref, erow_ref, gkn_ref, gkr_ref, wvh_ref, rep_ref,
                        tri_ref, tris_ref, omla_ref, osb_ref, omem_ref):
    ds = q_ref.shape[1]
    hs = H_MLA * ds
    ds_shift = ds.bit_length() - 1
    tc = tri_ref.shape[0]
    row = lax.broadcasted_iota(jnp.int32, (hs, 1), 0)
    qpos = past + (row & (ds - 1))
    rows_of = lambda x, hd: x[hd * ds:(hd + 1) * ds]

    latb = clat_ref[0, 0].astype(BF16)
    rope = crope_ref[0, 0]
    ropeb = rope.astype(BF16)
    kn = _dot(latb, wkn_ref[...])
    ssq = _dot((kn * kn).astype(BF16), ind_ref[...]) + _dot((rope * rope).astype(BF16), ones_ref[...])
    inv_hi, inv_lo = _split(lax.rsqrt(ssq * (1.0 / QK_DIM) + EPS))
    inv_rows = _dot_nt(erow_ref[...], inv_hi) + _dot_nt(erow_ref[...], inv_lo)
    qabs, qrope, s2 = [], [], []
    for hd in range(H_MLA):
        qh = q_ref[hd].astype(F32)
        qabs.append(_dot((qh[:, :NOPE_DIM] * gkn_ref[...]).astype(BF16), wkt_ref[hd]))
        qrope.append((qh[:, NOPE_DIM:] * gkr_ref[...]).astype(BF16))
        s2.append(_dot_nt(q_ref[hd], kn_ref[hd]))
    s1 = _dot_nt(jnp.concatenate(qabs, axis=0).astype(BF16), latb) + _dot_nt(jnp.concatenate(qrope, axis=0), ropeb)
    s1 = s1 * inv_rows
    s2 = jnp.concatenate(s2, axis=0)
    q_chunk = qpos // CHUNK
    s1 = jnp.where((lax.broadcasted_iota(jnp.int32, s1.shape, 1) // CHUNK) <= q_chunk, s1, NEG_BIG)
    s2 = jnp.where(((past + lax.broadcasted_iota(jnp.int32, s2.shape, 1)) // CHUNK) <= q_chunk, s2, NEG_BIG)
    m = jnp.maximum(jnp.max(s1, axis=-1, keepdims=True), jnp.max(s2, axis=-1, keepdims=True))
    p1 = jnp.exp2(s1 - m)
    p2 = jnp.exp2(s2 - m)
    den = jnp.sum(p1, axis=-1, keepdims=True) + jnp.sum(p2, axis=-1, keepdims=True)
    olat = _dot(p1.astype(BF16), latb)
    for hd in range(H_MLA):
        o = _dot(rows_of(olat, hd).astype(BF16), wvh_ref[hd]) + _dot(rows_of(p2, hd).astype(BF16), vn_ref[hd])
        omla_ref[hd] = (o / rows_of(den, hd)).astype(BF16)

    sq_all = jnp.concatenate([sq_ref[hd] for hd in range(H_SB)], axis=0)
    sqbd = _dot(sq_all, rep_ref[...])
    lane_head = lax.broadcasted_iota(jnp.int32, sqbd.shape, 1) // SB_DIM
    sqbd = jnp.where(lane_head == lax.shift_right_logical(row, ds_shift), sqbd, 0.0).astype(BF16)
    z1 = _dot_nt(sqbd, csk_ref[0, 0].astype(BF16))
    z2 = jnp.concatenate([_dot_nt(sq_ref[hd], skn_ref[hd]) for hd in range(H_SB)], axis=0)
    before_n = lax.broadcasted_iota(jnp.int32, z2.shape, 1) < (row & (ds - 1))
    l2 = jnp.where(before_n, -_softplus(z2), 0.0)
    a2 = jnp.where(before_n, jnp.exp(z2 + l2 + _dot2(l2, tris_ref[...])), 0.0)
    c = jnp.sum(l2, axis=-1, keepdims=True)
    l1 = -_softplus(z1)
    tri = tri_ref[...]
    a1 = [None] * (past // tc)
    for cb in reversed(range(past // tc)):
        cs = slice(cb * tc, (cb + 1) * tc)
        lc = l1[:, cs]
        a1[cb] = jnp.exp(z1[:, cs] + lc + _dot2(lc, tri) + c).astype(BF16)
        c = c + jnp.sum(lc, axis=-1, keepdims=True)
    osb = _dot(jnp.concatenate(a1, axis=1), csv_ref[0, 0].astype(BF16))
    for hd in range(H_SB):
        o = rows_of(osb, hd)[:, hd * SB_DIM:(hd + 1) * SB_DIM] + _dot(rows_of(a2, hd).astype(BF16), svn_ref[hd])
        osb_ref[hd] = o.astype(BF16)

    omem_ref[...] = _mem_heads(mq_ref[...], cmk_ref[0, 0].astype(BF16), cmv_ref[0, 0].astype(BF16)).astype(BF16)


def _sample_attn(q, kn, vn, sq, skn, svn, mq, c_lat, c_rope, c_sbk, c_sbv, c_mk, c_mv, layer, p):
    nb, past = c_lat.shape[1], c_lat.shape[2]
    ds = q.shape[1] // nb
    assert ds & (ds - 1) == 0, "row -> query index uses a bit mask"
    n_mem = c_mk.shape[2]
    tc = min(256, past)
    assert past % tc == 0
    hs = H_MLA * ds
    erow = np.zeros((hs, LANES), np.float32)
    erow[np.arange(hs), np.arange(hs) // ds] = 1.0
    rep = np.tile(np.eye(SB_DIM, dtype=np.float32), (1, H_SB))
    hm = lambda d: pl.BlockSpec((H_MLA, ds, d), lambda b: (0, b, 0))
    cache = lambda n, w: pl.BlockSpec((1, 1, n, w), lambda b: (layer, b, 0, 0))
    rows = q.shape[1]
    return pl.pallas_call(
        functools.partial(_sample_attn_kernel, past),
        grid=(nb,),
        in_specs=[hm(QK_DIM), hm(QK_DIM), hm(V_DIM), hm(SB_DIM), hm(SB_DIM), hm(SB_DIM),
                  pl.BlockSpec((ds, MEM_W), lambda b: (b, 0)),
                  cache(past, KV_LORA), cache(past, ROPE_DIM), cache(past, SB_W), cache(past, SB_W),
                  cache(n_mem, MEM_W), cache(n_mem, MEM_W),
                  _full((H_MLA, NOPE_DIM, KV_LORA)), _full((KV_LORA, H_MLA * NOPE_DIM)),
                  _full((H_MLA * NOPE_DIM, LANES)), _full((ROPE_DIM, LANES)), _full((hs, LANES)),
                  _full((1, NOPE_DIM)), _full((1, ROPE_DIM)), _full((H_MLA, KV_LORA, V_DIM)),
                  _full((SB_DIM, SB_W)), _full((tc, tc)), _full((ds, ds))],
        out_specs=(hm(V_DIM), hm(SB_DIM), pl.BlockSpec((ds, MEM_W), lambda b: (b, 0))),
        out_shape=(jax.ShapeDtypeStruct((H_MLA, rows, V_DIM), BF16),
                   jax.ShapeDtypeStruct((H_SB, rows, SB_DIM), BF16),
                   jax.ShapeDtypeStruct((rows, MEM_W), BF16)),
        compiler_params=_cparams("parallel"),
        name="sample_attn",
    )(q, kn, vn, sq, skn, svn, mq, c_lat, c_rope, c_sbk, c_sbv, c_mk, c_mv,
      p["w_k_t"], p["w_k_nope"], p["ind64"], p["ones_rope"], jnp.asarray(erow, BF16), p["g_k_nope"], p["g_k_rope"],
      p["w_v_h"], jnp.asarray(rep, BF16), _tri(tc), _tri(ds))


def _merge_kernel(x_ref, gattn_ref, wg_ref, omla_ref, osb_ref, omem_ref, woa_ref, wob_ref, wom_ref, wout_ref,
                  gffn_ref, wrh_ref, wrl_ref, br_ref, x1_ref, idx_ref, gate_ref):
    x = x_ref[...]
    h = (_rms(x) * gattn_ref[...]).astype(BF16)
    g = 1.0 / (1.0 + jnp.exp(-_dot(h, wg_ref[...])))
    ua = _dot(jnp.concatenate([omla_ref[hd] for hd in range(H_MLA)], axis=-1), woa_ref[...])
    ub = _dot(jnp.concatenate([osb_ref[hd] for hd in range(H_SB)], axis=-1), wob_ref[...])
    um = _dot(omem_ref[...], wom_ref[...])
    u = g[:, :D_MODEL] * ua + g[:, D_MODEL:2 * D_MODEL] * ub + g[:, 2 * D_MODEL:] * um
    x1 = x + _dot(u.astype(BF16), wout_ref[...])
    x1_ref[...] = x1
    xh, xl = _split(_rms(x1) * gffn_ref[...])
    lg = _dot(xh, wrh_ref[...]) + _dot(xh, wrl_ref[...]) + _dot(xl, wrh_ref[...]) + br_ref[...]
    lane = lax.broadcasted_iota(jnp.int32, lg.shape, 1).astype(F32)
    vals, ids = [], []
    for _ in range(TOP_K):
        m = jnp.max(lg, axis=-1, keepdims=True)
        sel = jnp.min(jnp.where(lg == m, lane, float(LANES)), axis=-1, keepdims=True)
        vals.append(m)
        ids.append(sel)
        lg = jnp.where(lane == sel, NEG_BIG, lg)
    es = [jnp.exp(v - vals[0]) for v in vals]
    den = es[0] + es[1] + es[2] + es[3]
    idx_o = jnp.zeros(lg.shape, F32)
    gate_o = jnp.zeros(lg.shape, F32)
    for k in range(TOP_K):
        idx_o = jnp.where(lane == float(k), ids[k], idx_o)
        gate_o = jnp.where(lane == float(k), es[k] / den, gate_o)
    idx_ref[...] = idx_o.astype(jnp.int32)
    gate_ref[...] = gate_o


def _merge(x2d, omla, osb, omem, p):
    rows = x2d.shape[0]
    tm = TM_MERGE
    row = lambda w: pl.BlockSpec((tm, w), lambda i: (i, 0))
    hm = lambda d: pl.BlockSpec((H_MLA, tm, d), lambda i: (0, i, 0))
    return pl.pallas_call(
        _merge_kernel,
        grid=(rows // tm,),
        in_specs=[row(D_MODEL), _full((1, D_MODEL)), _full((D_MODEL, 3 * D_MODEL)), hm(V_DIM), hm(SB_DIM),
                  row(MEM_W), _full((MLA_W, D_MODEL)), _full((SB_W, D_MODEL)),
                  _full((MEM_W, D_MODEL)), _full((D_MODEL, D_MODEL)), _full((1, D_MODEL)),
                  _full((D_MODEL, LANES)), _full((D_MODEL, LANES)), _full((1, LANES))],
        out_specs=(row(D_MODEL), row(LANES), row(LANES)),
        out_shape=(jax.ShapeDtypeStruct((rows, D_MODEL), F32), jax.ShapeDtypeStruct((rows, LANES), jnp.int32),
                   jax.ShapeDtypeStruct((rows, LANES), F32)),
        compiler_params=_cparams("parallel"),
        name="merge",
    )(x2d, p["g_attn"], p["w_g"], omla, osb, omem, p["w_o_mla"], p["w_o_sb"], p["w_o_mem"], p["w_out"],
      p["g_ffn"], p["w_r_hi"], p["w_r_lo"], p["b_r"])


def _row_dma(src, src_row, dst, dst_row, sem):
    return pltpu.make_async_copy(src.at[pl.ds(src_row, 1), :], dst.at[pl.ds(dst_row, 1), :], sem)


def _row_dma_wait(src, dst, sem, n):
    def body(r, carry):
        _row_dma(src, 0, dst, 0, sem).wait()
        return carry

    lax.fori_loop(0, n, body, 0, unroll=DMA_UNROLL)


def _dispatch_kernel(dest_ref, x_ref, xs_in, xs_out, sem):
    del xs_in
    n = dest_ref.shape[2]
    for r in range(n):
        _row_dma(x_ref, r // TOP_K, xs_out, dest_ref[0, 0, r], sem.at[0]).start(priority=r % DMA_THREADS)
    _row_dma_wait(x_ref, xs_out, sem.at[0], n)


def _moe_dispatch(x1, dest, xs):
    rows = x1.shape[0]
    tt = TT_COMBINE
    nt = rows // tt
    return pl.pallas_call(
        _dispatch_kernel,
        grid=(nt,),
        in_specs=[pl.BlockSpec((1, 1, TOP_K * tt), lambda t: (t, 0, 0), memory_space=pltpu.SMEM),
                  pl.BlockSpec((tt, D_MODEL), lambda t: (t, 0)),
                  pl.BlockSpec(memory_space=pl.ANY)],
        out_specs=pl.BlockSpec(memory_space=pl.ANY),
        out_shape=jax.ShapeDtypeStruct(xs.shape, xs.dtype),
        input_output_aliases={2: 0},
        scratch_shapes=[pltpu.SemaphoreType.DMA((1,))],
        compiler_params=_cparams("arbitrary"),
        name="moe_dispatch",
    )(dest.reshape(nt, 1, TOP_K * tt), x1, xs)


def _moe_kernel(be_ref, nact_ref, xs_ref, gffn_ref, wgu_ref, bgu_ref, wd_ref, bd_ref, o_ref, wgu_sc, wd_sc):
    b = pl.program_id(0)

    @pl.when(b >= nact_ref[0])
    def _():
        o_ref[...] = jnp.zeros(o_ref.shape, o_ref.dtype)

    @pl.when(b < nact_ref[0])
    def _():
        changed = jnp.logical_or(b == 0, be_ref[b] != be_ref[jnp.maximum(b - 1, 0)])

        @pl.when(changed)
        def _():
            wgu_sc[...] = wgu_ref[0].astype(BF16)
            wd_sc[...] = wd_ref[0].astype(BF16)

        xb = (_rms(xs_ref[...]) * gffn_ref[...]).astype(BF16)
        gu = _dot(xb, wgu_sc[...]) + bgu_ref[0]
        g = jnp.minimum(gu[:, :D_FF], SWIGLU_LIMIT)
        u = jnp.clip(gu[:, D_FF:], -SWIGLU_LIMIT, SWIGLU_LIMIT)
        hid = (u + 1.0) * (g / (1.0 + jnp.exp(-SWIGLU_ALPHA * g)))
        o_ref[...] = _dot(hid.astype(BF16), wd_sc[...]) + bd_ref[0]


def _moe_experts(xs, block_expert, n_active, p):
    n_blocks = block_expert.shape[0]
    bm = BM_MOE
    blk = lambda b, be, na: (jnp.minimum(b, na[0] - 1), 0)
    grid_spec = pltpu.PrefetchScalarGridSpec(
        num_scalar_prefetch=2,
        grid=(n_blocks,),
        in_specs=[
            pl.BlockSpec((bm, D_MODEL), blk),
            pl.BlockSpec((1, D_MODEL), lambda b, be, na: (0, 0)),
            pl.BlockSpec((1, D_MODEL, 2 * D_FF), lambda b, be, na: (be[b], 0, 0)),
            pl.BlockSpec((1, 1, 2 * D_FF), lambda b, be, na: (be[b], 0, 0)),
            pl.BlockSpec((1, D_FF, D_MODEL), lambda b, be, na: (be[b], 0, 0)),
            pl.BlockSpec((1, 1, D_MODEL), lambda b, be, na: (be[b], 0, 0)),
        ],
        out_specs=pl.BlockSpec((bm, D_MODEL), lambda b, be, na: (b, 0)),
        scratch_shapes=[pltpu.VMEM((D_MODEL, 2 * D_FF), BF16), pltpu.VMEM((D_FF, D_MODEL), BF16)],
    )
    return pl.pallas_call(
        _moe_kernel,
        grid_spec=grid_spec,
        out_shape=jax.ShapeDtypeStruct((n_blocks * bm, D_MODEL), F32),
        compiler_params=_cparams("arbitrary"),
        name="moe_experts",
    )(block_expert, n_active, xs, p["g_ffn"], p["w_gate_up"], p["b_gate_up"], p["w_down"], p["b_down"])


def _combine_gather(y_hbm, pos_ref, dst, sem, n):
    for r in range(n):
        _row_dma(y_hbm, pos_ref[0, 0, r], dst, r, sem).start(priority=r % DMA_THREADS)


def _combine_kernel(nta, pos_ref, posn_ref, y_hbm, x1a_ref, ga_ref, x1b_ref, gb_ref, oa_ref, ob_ref, buf, sem):
    t = pl.program_id(0)
    nt = pl.num_programs(0)
    tt = oa_ref.shape[0]
    n = TOP_K * tt
    slot = t % 2

    @pl.when(t == 0)
    def _():
        _combine_gather(y_hbm, pos_ref, buf.at[0], sem.at[0], n)

    @pl.when(t + 1 < nt)
    def _():
        _combine_gather(y_hbm, posn_ref, buf.at[1 - slot], sem.at[1 - slot], n)

    _row_dma_wait(y_hbm, buf.at[slot], sem.at[slot], n)

    def combine(x1_ref, gate_ref, o_ref):
        acc = x1_ref[...]
        gate = gate_ref[...]
        for k in range(TOP_K):
            acc = acc + gate[:, k:k + 1] * buf[slot, k * tt:(k + 1) * tt, :]
        o_ref[...] = acc

    @pl.when(t < nta)
    def _():
        combine(x1a_ref, ga_ref, oa_ref)

    @pl.when(t >= nta)
    def _():
        combine(x1b_ref, gb_ref, ob_ref)


def _moe_combine(x1a, gate_a, x1b, gate_b, yb, pos):
    tt = TT_COMBINE
    nta, ntb = x1a.shape[0] // tt, x1b.shape[0] // tt
    nt = nta + ntb
    pos3 = pos.reshape(nt, tt, TOP_K).transpose(0, 2, 1).reshape(nt, 1, TOP_K * tt)
    ia = lambda t: (jnp.minimum(t, nta - 1), 0)
    ib = lambda t: (jnp.maximum(t - nta, 0), 0)
    return pl.pallas_call(
        functools.partial(_combine_kernel, nta),
        grid=(nt,),
        in_specs=[pl.BlockSpec((1, 1, TOP_K * tt), lambda t: (t, 0, 0), memory_space=pltpu.SMEM),
                  pl.BlockSpec((1, 1, TOP_K * tt), lambda t: (jnp.minimum(t + 1, nt - 1), 0, 0),
                               memory_space=pltpu.SMEM),
                  pl.BlockSpec(memory_space=pl.ANY),
                  pl.BlockSpec((tt, D_MODEL), ia), pl.BlockSpec((tt, LANES), ia),
                  pl.BlockSpec((tt, D_MODEL), ib), pl.BlockSpec((tt, LANES), ib)],
        out_specs=(pl.BlockSpec((tt, D_MODEL), ia), pl.BlockSpec((tt, D_MODEL), ib)),
        out_shape=(jax.ShapeDtypeStruct(x1a.shape, F32), jax.ShapeDtypeStruct(x1b.shape, F32)),
        scratch_shapes=[pltpu.VMEM((2, TOP_K * tt, D_MODEL), F32), pltpu.SemaphoreType.DMA((2,))],
        compiler_params=_cparams("arbitrary"),
        name="moe_combine",
    )(pos3, pos3, yb, x1a, gate_a, x1b, gate_b)


def _moe_route(idx):
    rows = idx.shape[0]
    n = rows * TOP_K
    bm = BM_MOE
    e = idx.reshape(n)
    onehot = (e[:, None] == jnp.arange(N_EXPERTS, dtype=jnp.int32)[None, :]).astype(jnp.int32)
    csum = jnp.cumsum(onehot, axis=0)
    rank = jnp.sum(csum * onehot, axis=1) - 1
    counts = csum[-1]
    padded = (counts + bm - 1) // bm * bm
    pend = jnp.cumsum(padded)
    dest = ((pend - padded)[e] + rank).astype(jnp.int32).reshape(rows, TOP_K)
    n_blocks = -(-n // bm) + N_EXPERTS
    starts = jnp.arange(n_blocks, dtype=jnp.int32) * bm
    block_expert = jnp.minimum(
        jnp.sum((pend[None, :] <= starts[:, None]).astype(jnp.int32), axis=1), N_EXPERTS - 1).astype(jnp.int32)
    n_active = (pend[-1:] // bm).astype(jnp.int32)
    return dest, block_expert, n_active


def _pack_layer(l, g_attn, w_in, g_q_lat, w_q_b, g_q_nope, g_q_rope, g_kv_lat, w_kv_b, g_k_nope, g_k_rope, g_mem,
                w_mem_kv, g_mem_q, g_mem_k, w_o_mla, w_o_sb, w_o_mem, w_out, g_ffn, w_router, b_router,
                w_gate_up, b_gate_up, w_down, b_down):
    w = w_in[l]
    off = np.cumsum((Q_LORA, KV_LORA, ROPE_DIM, SB_W, SB_W, SB_W, MEM_W))
    o_kpe, o_sbq = int(off[1]), int(off[2])
    o_gate = int(off[6])
    kpe1 = w[:, o_kpe:o_kpe + ROPE_HALF]
    kpe2 = w[:, o_kpe + ROPE_HALF:o_kpe + ROPE_DIM]
    zl = jnp.zeros((D_MODEL, KPE_LANE), F32)
    zr = jnp.zeros((D_MODEL, LANES - KPE_LANE - ROPE_DIM), F32)
    w_a = jnp.concatenate([w[:, :o_kpe], zl, kpe1, kpe2, zr, zl, -kpe2, kpe1, zr, w[:, o_sbq:o_gate]], axis=1)
    wq = w_q_b[l]
    z_n = jnp.zeros((Q_LORA, H_MLA, NOPE_DIM), F32)
    wq_sw = jnp.concatenate([z_n, -wq[..., NOPE_DIM + ROPE_HALF:], wq[..., NOPE_DIM:NOPE_DIM + ROPE_HALF]], axis=-1)
    lane = np.arange(QK_W)
    ind = np.zeros((QK_W, LANES), np.float32)
    ind[lane, lane // QK_DIM] = 1.0
    ind64 = np.zeros((H_MLA * NOPE_DIM, LANES), np.float32)
    ind64[np.arange(H_MLA * NOPE_DIM), np.arange(H_MLA * NOPE_DIM) // NOPE_DIM] = 1.0
    pk = np.zeros((ROPE_DIM, QK_W), np.float32)
    for hd in range(H_MLA):
        pk[np.arange(ROPE_DIM), hd * QK_DIM + NOPE_DIM + np.arange(ROPE_DIM)] = 1.0
    wkv = w_kv_b[l]
    w_k = jnp.concatenate([wkv[..., :NOPE_DIM], jnp.zeros((KV_LORA, H_MLA, ROPE_DIM), F32)], axis=-1)
    g_q = jnp.tile(jnp.concatenate([g_q_nope[l], g_q_rope[l], g_q_rope[l]]), H_MLA) * (QK_DIM ** -0.5 * LOG2E)
    g_k = jnp.tile(jnp.concatenate([g_k_nope[l], g_k_rope[l], g_k_rope[l]]), H_MLA)
    wm = w_mem_kv[l]
    w_mem = jnp.concatenate([wm[..., :MEM_DIM].reshape(D_MODEL, MEM_W), wm[..., MEM_DIM:].reshape(D_MODEL, MEM_W)], 1)
    w_r = jnp.concatenate([w_router[l], jnp.zeros((D_MODEL, LANES - N_EXPERTS), F32)], axis=1)
    w_r_hi = w_r.astype(BF16)
    b_r = jnp.concatenate([b_router[l].astype(F32), jnp.full((LANES - N_EXPERTS,), NEG_BIG, F32)])
    return {
        "g_attn": g_attn[l][None], "w_a": w_a.astype(BF16), "g_q_lat": g_q_lat[l][None],
        "w_q": wq.reshape(Q_LORA, QK_W).astype(BF16), "w_q_sw": wq_sw.reshape(Q_LORA, QK_W).astype(BF16),
        "ind96": jnp.asarray(ind, BF16), "ind96_t": jnp.asarray(ind.T, BF16), "g_q": g_q[None],
        "g_kv_lat": g_kv_lat[l][None], "g_mem_q": jnp.tile(g_mem_q[l], H_MEM)[None] * (MEM_DIM ** -0.5),
        "w_k": w_k.reshape(KV_LORA, QK_W).astype(BF16), "p_kpe": jnp.asarray(pk, BF16), "g_k": g_k[None],
        "w_v": wkv[..., NOPE_DIM:].reshape(KV_LORA, MLA_W).astype(BF16),
        "w_v_t": wkv[..., NOPE_DIM:].reshape(KV_LORA, MLA_W).T.astype(BF16),
        "w_v_h": wkv[..., NOPE_DIM:].transpose(1, 0, 2).astype(BF16),
        "w_k_t": wkv[..., :NOPE_DIM].transpose(1, 2, 0).astype(BF16),
        "w_k_nope": wkv[..., :NOPE_DIM].reshape(KV_LORA, H_MLA * NOPE_DIM).astype(BF16),
        "ind64": jnp.asarray(ind64, BF16), "ones_rope": jnp.ones((ROPE_DIM, LANES), BF16),
        "g_k_nope": g_k_nope[l][None], "g_k_rope": jnp.concatenate([g_k_rope[l], g_k_rope[l]])[None],
        "g_mem": g_mem[l][None], "w_mem": w_mem.astype(BF16), "g_mem_k": jnp.tile(g_mem_k[l], H_MEM)[None],
        "w_g": w[:, o_gate:].astype(BF16), "w_sbv_t": w[:, o_gate - MEM_W - SB_W:o_gate - MEM_W].T.astype(BF16),
        "w_o_mla": w_o_mla[l].astype(BF16), "w_o_sb": w_o_sb[l].astype(BF16),
        "w_o_mem": w_o_mem[l].astype(BF16), "w_out": w_out[l].astype(BF16), "g_ffn": g_ffn[l][None],
        "w_r_hi": w_r_hi, "w_r_lo": (w_r - w_r_hi.astype(F32)).astype(BF16), "b_r": b_r[None],
        "w_gate_up": w_gate_up[l], "b_gate_up": b_gate_up[l][:, None, :], "w_down": w_down[l],
        "b_down": b_down[l][:, None, :],
    }


def kernel(x_prompt, x_sample, mem_prompt, cache_mla_latent, cache_mla_rope, cache_sb_k, cache_sb_v, cache_mem_k, cache_mem_v, g_attn, w_in, g_q_lat, w_q_b, g_q_nope, g_q_rope, g_kv_lat, w_kv_b, g_k_nope, g_k_rope, g_mem, w_mem_kv, g_mem_q, g_mem_k, w_o_mla, w_o_sb, w_o_mem, w_out, g_ffn, w_router, b_router, w_gate_up, b_gate_up, w_down, b_down):
    depth = g_attn.shape[0]
    bp, sp, _ = x_prompt.shape
    bs, ss, _ = x_sample.shape
    past = cache_mla_latent.shape[2]
    n_mem = mem_prompt.shape[1]
    assert bp == 1 and sp % TQ_MLA == 0 and sp % TM_KV == 0 and (bs * ss) % TM_PROJ == 0 and TM_PROJ % ss == 0
    rows_s = bs * ss
    c_sbk = cache_sb_k.reshape(depth, bs, past, SB_W)
    c_sbv = cache_sb_v.reshape(depth, bs, past, SB_W)
    c_mk = cache_mem_k.reshape(depth, bs, n_mem, MEM_W)
    c_mv = cache_mem_v.reshape(depth, bs, n_mem, MEM_W)
    xp = x_prompt.reshape(sp, D_MODEL)
    xs = x_sample.reshape(rows_s, D_MODEL)
    mem2d = mem_prompt.reshape(n_mem, D_MODEL)
    base_p, off_p = np.arange(sp // TM_PROJ) * TM_PROJ, np.arange(TM_PROJ)
    base_s, off_s = np.full((rows_s // TM_PROJ,), past), np.arange(TM_PROJ) % ss
    outs = [[] for _ in range(10)]
    for l in range(depth):
        p = _pack_layer(l, g_attn, w_in, g_q_lat, w_q_b, g_q_nope, g_q_rope, g_kv_lat, w_kv_b, g_k_nope, g_k_rope,
                        g_mem, w_mem_kv, g_mem_q, g_mem_k, w_o_mla, w_o_sb, w_o_mem, w_out, g_ffn, w_router,
                        b_router, w_gate_up, b_gate_up, w_down, b_down)
        q, lat, kpe, sq, sk, sv, skh, _, mq, svt = _proj(xp, base_p, off_p, p)
        kh, _, vth = _kv_expand(lat, kpe, p)
        o_mla = _mla_attn(q, kh, vth)
        o_sb = _sb_attn(sq, skh, svt)
        mk, mv = _mem_kv(mem2d, p)
        o_mem = _mem_attn(mq, mk, mv)
        x1p, idx_p, gate_p = _merge(xp, o_mla, o_sb, o_mem, p)
        q_s, lat_s, kpe_s, sq_s, sk_s, sv_s, skh_s, svh_s, mq_s, _ = _proj(xs, base_s, off_s, p)
        kn, vn, _ = _kv_expand(lat_s, kpe_s, p)
        o_mla_s, o_sb_s, o_mem_s = _sample_attn(q_s, kn, vn, sq_s, skh_s, svh_s, mq_s, cache_mla_latent,
                                                cache_mla_rope, c_sbk, c_sbv, c_mk, c_mv, l, p)
        x1s, idx_s, gate_s = _merge(xs, o_mla_s, o_sb_s, o_mem_s, p)
        dest, block_expert, n_active = _moe_route(jnp.concatenate([idx_p[:, :TOP_K], idx_s[:, :TOP_K]], axis=0))
        slots = jnp.zeros((block_expert.shape[0] * BM_MOE, D_MODEL), F32)
        slots = _moe_dispatch(x1s, dest[sp:], _moe_dispatch(x1p, dest[:sp], slots))
        yb = _moe_experts(slots, block_expert, n_active, p)
        xp, xs = _moe_combine(x1p, gate_p, x1s, gate_s, yb, dest)
        for lst, val in zip(outs, (lat.reshape(bp, sp, KV_LORA), kpe.reshape(bp, sp, ROPE_DIM),
                                   sk.reshape(bp, sp, H_SB, SB_DIM), sv.reshape(bp, sp, H_SB, SB_DIM),
                                   mk.reshape(bp, n_mem, H_MEM, MEM_DIM), mv.reshape(bp, n_mem, H_MEM, MEM_DIM),
                                   lat_s.reshape(bs, ss, KV_LORA), kpe_s.reshape(bs, ss, ROPE_DIM),
                                   sk_s.reshape(bs, ss, H_SB, SB_DIM), sv_s.reshape(bs, ss, H_SB, SB_DIM))):
            lst.append(val)
    return (xp.reshape(bp, sp, D_MODEL), xs.reshape(bs, ss, D_MODEL)) + tuple(jnp.stack(o) for o in outs)
```

```python
import functools

import numpy as np
import jax
import jax.numpy as jnp
from jax import lax
from jax.experimental import pallas as pl
from jax.experimental.pallas import tpu as pltpu

F32 = jnp.float32
BF16 = jnp.bfloat16

D_MODEL = 1024
CHUNK = 64
EPS = 1e-6
H_MLA = 8
NOPE_DIM = 64
ROPE_DIM = 32
ROPE_HALF = ROPE_DIM // 2
V_DIM = 64
Q_LORA = 384
KV_LORA = 256
ROPE_BASE = 10000.0
QK_DIM = NOPE_DIM + ROPE_DIM
QK_W = H_MLA * QK_DIM
H_SB = 8
SB_DIM = 64
H_MEM = 4
MEM_DIM = 128
N_EXPERTS = 32
TOP_K = 4
D_FF = 1024
SWIGLU_LIMIT = 7.0
SWIGLU_ALPHA = 1.702
MLA_W = H_MLA * V_DIM
SB_W = H_SB * SB_DIM
MEM_W = H_MEM * MEM_DIM

LANES = 128
VT_ROWS = LANES
MLA_CHAINS = 4
MLA_KSPLIT = 2
assert MLA_KSPLIT % 2 == 0
LOG2E = 1.4426950408889634
A_QLAT = 0
A_KVLAT = A_QLAT + Q_LORA
A_KPE = A_KVLAT + KV_LORA
A_KPE_SW = A_KPE + LANES
A_SBQ = A_KPE_SW + LANES
A_SBK = A_SBQ + SB_W
A_SBV = A_SBK + SB_W
A_MEMQ = A_SBV + SB_W
A_COLS = A_MEMQ + MEM_W
KPE_LANE = NOPE_DIM

TM_PROJ = 256
TM_KV = 512
TQ_MLA = 1024
TQ_SB = 256
SB_HEADS = 2
TM_MEM = 512
TM_MERGE = 512
BM_MOE = 512
TT_COMBINE = 256
DMA_UNROLL = 8
DMA_THREADS = 2
TOP_K_SHIFT = TOP_K.bit_length() - 1
assert 1 << TOP_K_SHIFT == TOP_K
VMEM_LIMIT = 56 * 1024 * 1024

SB_UNDERFLOW = -120.0
NEG_BIG = -3.0e38


def _cparams(*sem):
    return pltpu.CompilerParams(dimension_semantics=sem, vmem_limit_bytes=VMEM_LIMIT)


def _split(x):
    hi = x.astype(BF16)
    lo = (x - hi.astype(F32)).astype(BF16)
    return hi, lo


def _dot(a, b):
    return jnp.dot(a, b, preferred_element_type=F32)


def _dot2(x, m):
    hi, lo = _split(x)
    return _dot(hi, m) + _dot(lo, m)


def _dot_nt(a, b):
    return lax.dot_general(a, b, (((1,), (1,)), ((), ())), preferred_element_type=F32)


def _rms(x):
    return x * lax.rsqrt(jnp.mean(x * x, axis=-1, keepdims=True) + EPS)


def _softplus(z):
    return jnp.maximum(z, 0.0) + jnp.log(1.0 + jnp.exp(-jnp.abs(z)))


def _full(shape):
    n = len(shape)
    return pl.BlockSpec(shape, lambda *_: (0,) * n)


def _proj_kernel(x_ref, gattn_ref, wa_ref, gqlat_ref, wq_ref, wqs_ref, ind_ref, indt_ref, gq_ref, gkv_ref,
                 bcs_ref, ocos_ref, osin_ref, gmq_ref, wsvt_ref,
                 q_ref, lat_ref, kpe_ref, sq_ref, sk_ref, sv_ref, skh_ref, svh_ref, mq_ref, svt_ref):
    h = (_rms(x_ref[...]) * gattn_ref[...]).astype(BF16)
    z = _dot(h, wa_ref[...])
    svt = _dot_nt(wsvt_ref[...], h)
    pad = jnp.zeros((VT_ROWS - SB_DIM, svt.shape[1]), F32)
    for hd in range(H_SB):
        svt_ref[hd] = jnp.concatenate([svt[hd * SB_DIM:(hd + 1) * SB_DIM, :], pad], axis=0).astype(BF16)
    bc = bcs_ref[0, 0:1, :]
    bs = bcs_ref[0, 1:2, :]
    oc = ocos_ref[...]
    osn = osin_ref[...]
    cos_f = bc * oc - bs * osn
    sin_f = bs * oc + bc * osn
    qn = (_rms(z[:, A_QLAT:A_QLAT + Q_LORA]) * gqlat_ref[...]).astype(BF16)
    qr = _dot(qn, wq_ref[...]) * cos_f + _dot(qn, wqs_ref[...]) * sin_f
    ssq = _dot2(qr * qr, ind_ref[...])
    inv = lax.rsqrt(ssq * (1.0 / QK_DIM) + EPS)
    qo = qr * _dot2(inv, indt_ref[...]) * gq_ref[...]
    for hd in range(H_MLA):
        q_ref[hd] = qo[:, hd * QK_DIM:(hd + 1) * QK_DIM].astype(BF16)
    lat_ref[...] = _rms(z[:, A_KVLAT:A_KVLAT + KV_LORA]) * gkv_ref[...]
    kr = z[:, A_KPE:A_KPE + LANES] * cos_f[:, :LANES] + z[:, A_KPE_SW:A_KPE_SW + LANES] * sin_f[:, :LANES]
    kpe_ref[...] = kr[:, KPE_LANE:KPE_LANE + ROPE_DIM]
    sbq = z[:, A_SBQ:A_SBQ + SB_W] * (SB_DIM ** -0.5)
    sbk = z[:, A_SBK:A_SBK + SB_W]
    sbv = z[:, A_SBV:A_SBV + SB_W]
    sk_ref[...] = sbk
    sv_ref[...] = sbv
    for hd in range(H_SB):
        sl = slice(hd * SB_DIM, (hd + 1) * SB_DIM)
        sq_ref[hd] = sbq[:, sl].astype(BF16)
        skh_ref[hd] = sbk[:, sl].astype(BF16)
        svh_ref[hd] = sbv[:, sl].astype(BF16)
    mqs = []
    for hd in range(H_MEM):
        mqs.append(_rms(z[:, A_MEMQ + hd * MEM_DIM:A_MEMQ + (hd + 1) * MEM_DIM]))
    mq_ref[...] = (jnp.concatenate(mqs, axis=-1) * gmq_ref[...]).astype(BF16)


def _rope_tables(base_pos, off_pos):
    lane = np.arange(QK_W) % QK_DIM
    inv_freq = ROPE_BASE ** (-np.arange(ROPE_HALF, dtype=np.float64) / ROPE_HALF)
    freq = np.where(lane >= NOPE_DIM, inv_freq[(lane - NOPE_DIM) % ROPE_HALF], 0.0)
    ab = np.asarray(base_pos, np.float64)[:, None] * freq
    ao = np.asarray(off_pos, np.float64)[:, None] * freq
    bcs = np.stack([np.cos(ab), np.sin(ab)], axis=1).astype(np.float32)
    return jnp.asarray(bcs), jnp.asarray(np.cos(ao), F32), jnp.asarray(np.sin(ao), F32)


def _proj(x2d, base_pos, off_pos, p):
    rows = x2d.shape[0]
    tm = TM_PROJ
    nt = rows // tm
    bcs, ocos, osin = _rope_tables(base_pos, off_pos)
    row = lambda w: pl.BlockSpec((tm, w), lambda i: (i, 0))
    hm = lambda d: pl.BlockSpec((H_MLA, tm, d), lambda i: (0, i, 0))
    out_shape = (
        jax.ShapeDtypeStruct((H_MLA, rows, QK_DIM), BF16),
        jax.ShapeDtypeStruct((rows, KV_LORA), F32),
        jax.ShapeDtypeStruct((rows, ROPE_DIM), F32),
        jax.ShapeDtypeStruct((H_SB, rows, SB_DIM), BF16),
        jax.ShapeDtypeStruct((rows, SB_W), F32),
        jax.ShapeDtypeStruct((rows, SB_W), F32),
        jax.ShapeDtypeStruct((H_SB, rows, SB_DIM), BF16),
        jax.ShapeDtypeStruct((H_SB, rows, SB_DIM), BF16),
        jax.ShapeDtypeStruct((rows, MEM_W), BF16),
        jax.ShapeDtypeStruct((H_SB, VT_ROWS, rows), BF16),
    )
    return pl.pallas_call(
        _proj_kernel,
        grid=(nt,),
        in_specs=[row(D_MODEL), _full((1, D_MODEL)), _full((D_MODEL, A_COLS)), _full((1, Q_LORA)),
                  _full((Q_LORA, QK_W)), _full((Q_LORA, QK_W)), _full((QK_W, LANES)), _full((LANES, QK_W)),
                  _full((1, QK_W)), _full((1, KV_LORA)),
                  pl.BlockSpec((1, 2, QK_W), lambda i: (i, 0, 0)), _full((tm, QK_W)), _full((tm, QK_W)),
                  _full((1, MEM_W)), _full((SB_W, D_MODEL))],
        out_specs=(hm(QK_DIM), row(KV_LORA), row(ROPE_DIM), hm(SB_DIM), row(SB_W), row(SB_W), hm(SB_DIM),
                   hm(SB_DIM), row(MEM_W), pl.BlockSpec((H_SB, VT_ROWS, tm), lambda i: (0, 0, i))),
        out_shape=out_shape,
        compiler_params=_cparams("parallel"),
        name="proj",
    )(x2d, p["g_attn"], p["w_a"], p["g_q_lat"], p["w_q"], p["w_q_sw"], p["ind96"], p["ind96_t"], p["g_q"],
      p["g_kv_lat"], bcs, ocos, osin, p["g_mem_q"], p["w_sbv_t"])


def _expand_keys(lat, kpe, wk, pk, ind, indt, gk):
    kf = _dot(lat.astype(BF16), wk) + _dot2(kpe, pk)
    inv = lax.rsqrt(_dot2(kf * kf, ind) * (1.0 / QK_DIM) + EPS)
    return kf * _dot2(inv, indt) * gk


def _kv_expand_kernel(lat_ref, kpe_ref, wk_ref, pk_ref, ind_ref, indt_ref, gk_ref, wv_ref, wvt_ref,
                      k_ref, v_ref, vt_ref):
    lat = lat_ref[...]
    latb = lat.astype(BF16)
    ko = _expand_keys(lat, kpe_ref[...], wk_ref[...], pk_ref[...], ind_ref[...], indt_ref[...], gk_ref[...])
    v = _dot(latb, wv_ref[...])
    vt = _dot_nt(wvt_ref[...], latb)
    ones_row = (lax.broadcasted_iota(jnp.int32, (VT_ROWS - V_DIM, vt.shape[1]), 0) == 0).astype(F32)
    for hd in range(H_MLA):
        k_ref[hd] = ko[:, hd * QK_DIM:(hd + 1) * QK_DIM].astype(BF16)
        v_ref[hd] = v[:, hd * V_DIM:(hd + 1) * V_DIM].astype(BF16)
        vt_ref[hd] = jnp.concatenate([vt[hd * V_DIM:(hd + 1) * V_DIM, :], ones_row], axis=0).astype(BF16)


def _kv_expand(lat, kpe, p):
    rows = lat.shape[0]
    tm = min(TM_KV, rows)
    row = lambda w: pl.BlockSpec((tm, w), lambda i: (i, 0))
    hm = lambda d: pl.BlockSpec((H_MLA, tm, d), lambda i: (0, i, 0))
    return pl.pallas_call(
        _kv_expand_kernel,
        grid=(rows // tm,),
        in_specs=[row(KV_LORA), row(ROPE_DIM), _full((KV_LORA, QK_W)), _full((ROPE_DIM, QK_W)),
                  _full((QK_W, LANES)), _full((LANES, QK_W)), _full((1, QK_W)), _full((KV_LORA, MLA_W)),
                  _full((MLA_W, KV_LORA))],
        out_specs=(hm(QK_DIM), hm(V_DIM), pl.BlockSpec((H_MLA, VT_ROWS, tm), lambda i: (0, 0, i))),
        out_shape=(jax.ShapeDtypeStruct((H_MLA, rows, QK_DIM), BF16),
                   jax.ShapeDtypeStruct((H_MLA, rows, V_DIM), BF16),
                   jax.ShapeDtypeStruct((H_MLA, VT_ROWS, rows), BF16)),
        compiler_params=_cparams("parallel"),
        name="kv_expand",
    )(lat, kpe, p["w_k"], p["p_kpe"], p["ind96"], p["ind96_t"], p["g_k"], p["w_v"], p["w_v_t"])


def _mla_update(st, vt, m, acc):
    m_new = jnp.maximum(m, jnp.max(st, axis=0, keepdims=True))
    pr = jnp.exp2(st - m_new).astype(BF16)
    return m_new, acc * jnp.exp2(m - m_new) + _dot(vt, pr)


def _mla_step(qc, k, vt, m, acc, mask):
    st = _dot_nt(k, qc)
    if mask is not None:
        st = jnp.where(mask, st, NEG_BIG)
    return _mla_update(st, vt, m, acc)


def _mla_attn_kernel(q_ref, k_ref, vt_ref, o_ref, s_sc, p_sc):
    i = pl.program_id(1)
    tq = q_ref.shape[1]
    nc = MLA_CHAINS
    tc = tq // nc
    tk = tq // MLA_KSPLIT
    qs = [q_ref[0, c * tc:(c + 1) * tc, :] for c in range(nc)]
    key_chunk = lax.broadcasted_iota(jnp.int32, (tk, tc), 0) // CHUNK
    qry_chunk = lax.broadcasted_iota(jnp.int32, (tk, tc), 1) // CHUNK

    def scores(t, slot):
        k = k_ref[0, pl.ds(pl.multiple_of(t * tk, tk), tk), :]
        out = []
        for c in range(nc):
            st = _dot_nt(k, qs[c])
            s_sc[slot, c] = st
            out.append(jnp.max(st, axis=0, keepdims=True))
        return out

    def accumulate(t, slot, alphas, accs):
        vt = vt_ref[0, :, pl.ds(pl.multiple_of(t * tk, tk), tk)]
        return [accs[c] * alphas[c] + _dot(vt, p_sc[slot, c]) for c in range(nc)]

    def step(t, slot, cmax, alphas, ms, accs, last, mask_u):
        nxt = None if last else scores(t + 1, 1 - slot)
        accs = accumulate(jnp.maximum(t - 1, 0), 1 - slot, alphas, accs)
        m_new, alphas = [], []
        for c in range(nc):
            st = s_sc[slot, c]
            if mask_u is None:
                cm = cmax[c]
            else:
                st = jnp.where(key_chunk + mask_u * (tk // CHUNK) <= qry_chunk + c * (tc // CHUNK), st, NEG_BIG)
                cm = jnp.max(st, axis=0, keepdims=True)
            mn = jnp.maximum(ms[c], cm)
            p_sc[slot, c] = jnp.exp2(st - mn).astype(BF16)
            alphas.append(jnp.exp2(ms[c] - mn))
            m_new.append(mn)
        return nxt, alphas, m_new, accs

    p_sc[1] = jnp.zeros(p_sc.shape[1:], BF16)
    init = (tuple(scores(0, 0)) + (jnp.ones((1, tc), F32),) * nc + (jnp.full((1, tc), NEG_BIG, F32),) * nc
            + (jnp.zeros((VT_ROWS, tc), F32),) * nc)
    unpack = lambda carry: [list(carry[g * nc:(g + 1) * nc]) for g in range(4)]

    def body(j, carry):
        cmax, alphas, ms, accs = unpack(carry)
        for u in range(MLA_KSPLIT):
            cmax, alphas, ms, accs = step(j * MLA_KSPLIT + u, u % 2, cmax, alphas, ms, accs, False, None)
        return tuple(cmax) + tuple(alphas) + tuple(ms) + tuple(accs)

    cmax, alphas, ms, accs = unpack(lax.fori_loop(0, i, body, init))
    for u in range(MLA_KSPLIT):
        cmax, alphas, ms, accs = step(i * MLA_KSPLIT + u, u % 2, cmax, alphas, ms, accs, u + 1 == MLA_KSPLIT, u)
    accs = accumulate((i + 1) * MLA_KSPLIT - 1, (MLA_KSPLIT - 1) % 2, alphas, accs)
    for c in range(nc):
        ot = accs[c].T
        o_ref[0, c * tc:(c + 1) * tc, :] = (ot[:, :V_DIM] / ot[:, V_DIM:V_DIM + 1]).astype(BF16)


def _mla_attn(q, k, vt):
    rows = q.shape[1]
    tq = min(TQ_MLA, rows)
    return pl.pallas_call(
        _mla_attn_kernel,
        grid=(H_MLA, rows // tq),
        in_specs=[pl.BlockSpec((1, tq, QK_DIM), lambda h, i: (h, i, 0)),
                  pl.BlockSpec((1, rows, QK_DIM), lambda h, i: (h, 0, 0)),
                  pl.BlockSpec((1, VT_ROWS, rows), lambda h, i: (h, 0, 0))],
        out_specs=pl.BlockSpec((1, tq, V_DIM), lambda h, i: (h, i, 0)),
        out_shape=jax.ShapeDtypeStruct((H_MLA, rows, V_DIM), BF16),
        scratch_shapes=[pltpu.VMEM((2, MLA_CHAINS, tq // MLA_KSPLIT, tq // MLA_CHAINS), F32),
                        pltpu.VMEM((2, MLA_CHAINS, tq // MLA_KSPLIT, tq // MLA_CHAINS), BF16)],
        compiler_params=_cparams("parallel", "arbitrary"),
        name="mla_attn",
    )(q, k, vt)


def _sb_attn_kernel(q_ref, k_ref, vt_ref, triu_ref, o_ref):
    i = pl.program_id(1)
    nh, tq = q_ref.shape[0], q_ref.shape[1]
    triu = triu_ref[...]
    qs = [q_ref[c] for c in range(nh)]
    before = lax.broadcasted_iota(jnp.int32, (tq, tq), 0) < lax.broadcasted_iota(jnp.int32, (tq, tq), 1)

    def process(tiles, cs, accs):
        starts = [pl.multiple_of(j * tq, tq) for j, _, _ in tiles]
        zs, lks, bts = {}, {}, {}
        for t in range(len(tiles)):
            for c in range(nh):
                zs[t, c] = _dot_nt(k_ref[c, pl.ds(starts[t], tq), :], qs[c])
        for t, (_, diag, live) in enumerate(tiles):
            for c in range(nh):
                lk = -_softplus(zs[t, c])
                if diag:
                    lk = jnp.where(before, lk, 0.0)
                if live is not None:
                    lk = jnp.where(live, lk, 0.0)
                lks[t, c] = lk
        for key, lk in lks.items():
            hi, lo = _split(lk)
            bts[key] = _dot(triu, hi) + _dot(triu, lo)
        cs, accs = list(cs), list(accs)
        for t, (_, diag, live) in enumerate(tiles):
            for c in range(nh):
                a = jnp.exp(zs[t, c] + lks[t, c] + bts[t, c] + cs[c])
                if diag:
                    a = jnp.where(before, a, 0.0)
                if live is not None:
                    a = jnp.where(live, a, 0.0)
                accs[c] = accs[c] + _dot(vt_ref[c, :, pl.ds(starts[t], tq)], a.astype(BF16))
                cs[c] = cs[c] + jnp.sum(lks[t, c], axis=0, keepdims=True)
        return cs, accs

    def cmax(cs):
        out = jnp.max(cs[0])
        for c in cs[1:]:
            out = jnp.maximum(out, jnp.max(c))
        return out

    cs = [jnp.zeros((1, tq), F32)] * nh
    accs = [jnp.zeros((VT_ROWS, tq), F32)] * nh
    cs, accs = process([(i, True, None), (jnp.maximum(i - 1, 0), False, i > 0)], cs, accs)

    def cond(carry):
        return jnp.logical_and(carry[0] >= 0, carry[1] > SB_UNDERFLOW)

    def body(carry):
        j = carry[0]
        cs, accs = process([(j, False, None)], carry[2:2 + nh], carry[2 + nh:])
        return (j - 1, cmax(cs)) + tuple(cs) + tuple(accs)

    carry = lax.while_loop(cond, body, (i - 2, cmax(cs)) + tuple(cs) + tuple(accs))
    for c in range(nh):
        o_ref[c] = carry[2 + nh + c].T[:, :SB_DIM].astype(BF16)


def _tri(n):
    return jnp.asarray(np.tril(np.ones((n, n), np.float32), -1), BF16)


def _sb_attn(q, k, vt):
    rows = q.shape[1]
    tq = min(TQ_SB, rows)
    nh = SB_HEADS
    return pl.pallas_call(
        _sb_attn_kernel,
        grid=(H_SB // nh, rows // tq),
        in_specs=[pl.BlockSpec((nh, tq, SB_DIM), lambda h, i: (h, i, 0)),
                  pl.BlockSpec((nh, rows, SB_DIM), lambda h, i: (h, 0, 0)),
                  pl.BlockSpec((nh, VT_ROWS, rows), lambda h, i: (h, 0, 0)),
                  _full((tq, tq))],
        out_specs=pl.BlockSpec((nh, tq, SB_DIM), lambda h, i: (h, i, 0)),
        out_shape=jax.ShapeDtypeStruct((H_SB, rows, SB_DIM), BF16),
        compiler_params=_cparams("parallel", "arbitrary"),
        name="sb_attn",
    )(q, k, vt, _tri(tq).T)


def _mem_kv_kernel(mem_ref, gmem_ref, w_ref, gk_ref, mk_ref, mv_ref):
    mn = (_rms(mem_ref[...]) * gmem_ref[...]).astype(BF16)
    kv = _dot(mn, w_ref[...])
    ks = [_rms(kv[:, hd * MEM_DIM:(hd + 1) * MEM_DIM]) for hd in range(H_MEM)]
    mk_ref[...] = jnp.concatenate(ks, axis=-1) * gk_ref[...]
    mv_ref[...] = kv[:, MEM_W:]


def _mem_kv(mem2d, p):
    n = mem2d.shape[0]
    return pl.pallas_call(
        _mem_kv_kernel,
        grid=(1,),
        in_specs=[_full((n, D_MODEL)), _full((1, D_MODEL)), _full((D_MODEL, 2 * MEM_W)), _full((1, MEM_W))],
        out_specs=(_full((n, MEM_W)), _full((n, MEM_W))),
        out_shape=(jax.ShapeDtypeStruct((n, MEM_W), F32), jax.ShapeDtypeStruct((n, MEM_W), F32)),
        compiler_params=_cparams("arbitrary"),
        name="mem_kv",
    )(mem2d, p["g_mem"], p["w_mem"], p["g_mem_k"])


def _mem_heads(mq, mk, mv):
    outs = []
    for hd in range(H_MEM):
        sl = slice(hd * MEM_DIM, (hd + 1) * MEM_DIM)
        s = _dot_nt(mq[:, sl], mk[:, sl])
        pr = jnp.exp(s - jnp.max(s, axis=-1, keepdims=True))
        o = _dot(pr.astype(BF16), mv[:, sl])
        outs.append(o / jnp.sum(pr, axis=-1, keepdims=True))
    return jnp.concatenate(outs, axis=-1)


def _mem_attn_kernel(mq_ref, mk_ref, mv_ref, o_ref):
    o_ref[...] = _mem_heads(mq_ref[...], mk_ref[...].astype(BF16), mv_ref[...].astype(BF16)).astype(BF16)


def _mem_attn(mq, mk, mv):
    rows = mq.shape[0]
    tm = min(TM_MEM, rows)
    n = mk.shape[0]
    return pl.pallas_call(
        _mem_attn_kernel,
        grid=(rows // tm,),
        in_specs=[pl.BlockSpec((tm, MEM_W), lambda i: (i, 0)), _full((n, MEM_W)), _full((n, MEM_W))],
        out_specs=pl.BlockSpec((tm, MEM_W), lambda i: (i, 0)),
        out_shape=jax.ShapeDtypeStruct((rows, MEM_W), BF16),
        compiler_params=_cparams("parallel"),
        name="mem_attn",
    )(mq, mk, mv)


def _sample_attn_kernel(past, q_ref, kn_ref, vn_ref, sq_ref, skn_ref, svn_ref, mq_ref,
                        clat_ref, crope_ref, csk_ref, csv_ref, cmk_ref, cmv_ref,
                        wkt_ref, wkn_ref, ind_ref, ones_ref, erow_ref, gkn_ref, gkr_ref, wvh_ref, rep_ref,
                        tri_ref, tris_ref, omla_ref, osb_ref, omem_ref):
    ds = q_ref.shape[1]
    hs = H_MLA * ds
    ds_shift = ds.bit_length() - 1
    tc = tri_ref.shape[0]
    row = lax.broadcasted_iota(jnp.int32, (hs, 1), 0)
    qpos = past + (row & (ds - 1))
    rows_of = lambda x, hd: x[hd * ds:(hd + 1) * ds]

    latb = clat_ref[0, 0].astype(BF16)
    rope = crope_ref[0, 0]
    ropeb = rope.astype(BF16)
    kn = _dot(latb, wkn_ref[...])
    ssq = _dot((kn * kn).astype(BF16), ind_ref[...]) + _dot((rope * rope).astype(BF16), ones_ref[...])
    inv_hi, inv_lo = _split(lax.rsqrt(ssq * (1.0 / QK_DIM) + EPS))
    inv_rows = _dot_nt(erow_ref[...], inv_hi) + _dot_nt(erow_ref[...], inv_lo)
    qabs, qrope, s2 = [], [], []
    for hd in range(H_MLA):
        qh = q_ref[hd].astype(F32)
        qabs.append(_dot((qh[:, :NOPE_DIM] * gkn_ref[...]).astype(BF16), wkt_ref[hd]))
        qrope.append((qh[:, NOPE_DIM:] * gkr_ref[...]).astype(BF16))
        s2.append(_dot_nt(q_ref[hd], kn_ref[hd]))
    s1 = _dot_nt(jnp.concatenate(qabs, axis=0).astype(BF16), latb) + _dot_nt(jnp.concatenate(qrope, axis=0), ropeb)
    s1 = s1 * inv_rows
    s2 = jnp.concatenate(s2, axis=0)
    q_chunk = qpos // CHUNK
    s1 = jnp.where((lax.broadcasted_iota(jnp.int32, s1.shape, 1) // CHUNK) <= q_chunk, s1, NEG_BIG)
    s2 = jnp.where(((past + lax.broadcasted_iota(jnp.int32, s2.shape, 1)) // CHUNK) <= q_chunk, s2, NEG_BIG)
    m = jnp.maximum(jnp.max(s1, axis=-1, keepdims=True), jnp.max(s2, axis=-1, keepdims=True))
    p1 = jnp.exp2(s1 - m)
    p2 = jnp.exp2(s2 - m)
    den = jnp.sum(p1, axis=-1, keepdims=True) + jnp.sum(p2, axis=-1, keepdims=True)
    olat = _dot(p1.astype(BF16), latb)
    for hd in range(H_MLA):
        o = _dot(rows_of(olat, hd).astype(BF16), wvh_ref[hd]) + _dot(rows_of(p2, hd).astype(BF16), vn_ref[hd])
        omla_ref[hd] = (o / rows_of(den, hd)).astype(BF16)

    sq_all = jnp.concatenate([sq_ref[hd] for hd in range(H_SB)], axis=0)
    sqbd = _dot(sq_all, rep_ref[...])
    lane_head = lax.broadcasted_iota(jnp.int32, sqbd.shape, 1) // SB_DIM
    sqbd = jnp.where(lane_head == lax.shift_right_logical(row, ds_shift), sqbd, 0.0).astype(BF16)
    z1 = _dot_nt(sqbd, csk_ref[0, 0].astype(BF16))
    z2 = jnp.concatenate([_dot_nt(sq_ref[hd], skn_ref[hd]) for hd in range(H_SB)], axis=0)
    before_n = lax.broadcasted_iota(jnp.int32, z2.shape, 1) < (row & (ds - 1))
    l2 = jnp.where(before_n, -_softplus(z2), 0.0)
    a2 = jnp.where(before_n, jnp.exp(z2 + l2 + _dot2(l2, tris_ref[...])), 0.0)
    c = jnp.sum(l2, axis=-1, keepdims=True)
    l1 = -_softplus(z1)
    tri = tri_ref[...]
    a1 = [None] * (past // tc)
    for cb in reversed(range(past // tc)):
        cs = slice(cb * tc, (cb + 1) * tc)
        lc = l1[:, cs]
        a1[cb] = jnp.exp(z1[:, cs] + lc + _dot2(lc, tri) + c).astype(BF16)
        c = c + jnp.sum(lc, axis=-1, keepdims=True)
    osb = _dot(jnp.concatenate(a1, axis=1), csv_ref[0, 0].astype(BF16))
    for hd in range(H_SB):
        o = rows_of(osb, hd)[:, hd * SB_DIM:(hd + 1) * SB_DIM] + _dot(rows_of(a2, hd).astype(BF16), svn_ref[hd])
        osb_ref[hd] = o.astype(BF16)

    omem_ref[...] = _mem_heads(mq_ref[...], cmk_ref[0, 0].astype(BF16), cmv_ref[0, 0].astype(BF16)).astype(BF16)


def _sample_attn(q, kn, vn, sq, skn, svn, mq, c_lat, c_rope, c_sbk, c_sbv, c_mk, c_mv, layer, p):
    nb, past = c_lat.shape[1], c_lat.shape[2]
    ds = q.shape[1] // nb
    assert ds & (ds - 1) == 0, "row -> query index uses a bit mask"
    n_mem = c_mk.shape[2]
    tc = min(256, past)
    assert past % tc == 0
    hs = H_MLA * ds
    erow = np.zeros((hs, LANES), np.float32)
    erow[np.arange(hs), np.arange(hs) // ds] = 1.0
    rep = np.tile(np.eye(SB_DIM, dtype=np.float32), (1, H_SB))
    hm = lambda d: pl.BlockSpec((H_MLA, ds, d), lambda b: (0, b, 0))
    cache = lambda n, w: pl.BlockSpec((1, 1, n, w), lambda b: (layer, b, 0, 0))
    rows = q.shape[1]
    return pl.pallas_call(
        functools.partial(_sample_attn_kernel, past),
        grid=(nb,),
        in_specs=[hm(QK_DIM), hm(QK_DIM), hm(V_DIM), hm(SB_DIM), hm(SB_DIM), hm(SB_DIM),
                  pl.BlockSpec((ds, MEM_W), lambda b: (b, 0)),
                  cache(past, KV_LORA), cache(past, ROPE_DIM), cache(past, SB_W), cache(past, SB_W),
                  cache(n_mem, MEM_W), cache(n_mem, MEM_W),
                  _full((H_MLA, NOPE_DIM, KV_LORA)), _full((KV_LORA, H_MLA * NOPE_DIM)),
                  _full((H_MLA * NOPE_DIM, LANES)), _full((ROPE_DIM, LANES)), _full((hs, LANES)),
                  _full((1, NOPE_DIM)), _full((1, ROPE_DIM)), _full((H_MLA, KV_LORA, V_DIM)),
                  _full((SB_DIM, SB_W)), _full((tc, tc)), _full((ds, ds))],
        out_specs=(hm(V_DIM), hm(SB_DIM), pl.BlockSpec((ds, MEM_W), lambda b: (b, 0))),
        out_shape=(jax.ShapeDtypeStruct((H_MLA, rows, V_DIM), BF16),
                   jax.ShapeDtypeStruct((H_SB, rows, SB_DIM), BF16),
                   jax.ShapeDtypeStruct((rows, MEM_W), BF16)),
        compiler_params=_cparams("parallel"),
        name="sample_attn",
    )(q, kn, vn, sq, skn, svn, mq, c_lat, c_rope, c_sbk, c_sbv, c_mk, c_mv,
      p["w_k_t"], p["w_k_nope"], p["ind64"], p["ones_rope"], jnp.asarray(erow, BF16), p["g_k_nope"], p["g_k_rope"],
      p["w_v_h"], jnp.asarray(rep, BF16), _tri(tc), _tri(ds))


def _merge_kernel(fuse_mem, x_ref, gattn_ref, wg_ref, omla_ref, osb_ref, omem_ref, mk_ref, mv_ref, woa_ref, wob_ref,
                  wom_ref, wout_ref, gffn_ref, wrh_ref, wrl_ref, br_ref, x1_ref, idx_ref, gate_ref):
    x = x_ref[...]
    h = (_rms(x) * gattn_ref[...]).astype(BF16)
    g = 1.0 / (1.0 + jnp.exp(-_dot(h, wg_ref[...])))
    ua = _dot(jnp.concatenate([omla_ref[hd] for hd in range(H_MLA)], axis=-1), woa_ref[...])
    ub = _dot(jnp.concatenate([osb_ref[hd] for hd in range(H_SB)], axis=-1), wob_ref[...])
    if fuse_mem:
        omem = _mem_heads(omem_ref[...], mk_ref[...].astype(BF16), mv_ref[...].astype(BF16)).astype(BF16)
    else:
        omem = omem_ref[...]
    um = _dot(omem, wom_ref[...])
    u = g[:, :D_MODEL] * ua + g[:, D_MODEL:2 * D_MODEL] * ub + g[:, 2 * D_MODEL:] * um
    x1 = x + _dot(u.astype(BF16), wout_ref[...])
    x1_ref[...] = x1
    xh, xl = _split(_rms(x1) * gffn_ref[...])
    lg = _dot(xh, wrh_ref[...]) + _dot(xh, wrl_ref[...]) + _dot(xl, wrh_ref[...]) + br_ref[...]
    lane = lax.broadcasted_iota(jnp.int32, lg.shape, 1).astype(F32)
    vals, ids = [], []
    for _ in range(TOP_K):
        m = jnp.max(lg, axis=-1, keepdims=True)
        sel = jnp.min(jnp.where(lg == m, lane, float(LANES)), axis=-1, keepdims=True)
        vals.append(m)
        ids.append(sel)
        lg = jnp.where(lane == sel, NEG_BIG, lg)
    es = [jnp.exp(v - vals[0]) for v in vals]
    den = es[0] + es[1] + es[2] + es[3]
    idx_o = jnp.zeros(lg.shape, F32)
    gate_o = jnp.zeros(lg.shape, F32)
    for k in range(TOP_K):
        idx_o = jnp.where(lane == float(k), ids[k], idx_o)
        gate_o = jnp.where(lane == float(k), es[k] / den, gate_o)
    idx_ref[...] = idx_o.astype(jnp.int32)
    gate_ref[...] = gate_o


def _merge(x2d, omla, osb, omem, mk, mv, fuse_mem, p):
    rows = x2d.shape[0]
    tm = TM_MERGE
    n = mk.shape[0]
    row = lambda w: pl.BlockSpec((tm, w), lambda i: (i, 0))
    hm = lambda d: pl.BlockSpec((H_MLA, tm, d), lambda i: (0, i, 0))
    return pl.pallas_call(
        functools.partial(_merge_kernel, fuse_mem),
        grid=(rows // tm,),
        in_specs=[row(D_MODEL), _full((1, D_MODEL)), _full((D_MODEL, 3 * D_MODEL)), hm(V_DIM), hm(SB_DIM),
                  row(MEM_W), _full((n, MEM_W)), _full((n, MEM_W)), _full((MLA_W, D_MODEL)), _full((SB_W, D_MODEL)),
                  _full((MEM_W, D_MODEL)), _full((D_MODEL, D_MODEL)), _full((1, D_MODEL)),
                  _full((D_MODEL, LANES)), _full((D_MODEL, LANES)), _full((1, LANES))],
        out_specs=(row(D_MODEL), row(LANES), row(LANES)),
        out_shape=(jax.ShapeDtypeStruct((rows, D_MODEL), F32), jax.ShapeDtypeStruct((rows, LANES), jnp.int32),
                   jax.ShapeDtypeStruct((rows, LANES), F32)),
        compiler_params=_cparams("parallel"),
        name="merge",
    )(x2d, p["g_attn"], p["w_g"], omla, osb, omem, mk, mv, p["w_o_mla"], p["w_o_sb"], p["w_o_mem"], p["w_out"],
      p["g_ffn"], p["w_r_hi"], p["w_r_lo"], p["b_r"])


def _row_dma(src, src_row, dst, dst_row, sem):
    return pltpu.make_async_copy(src.at[pl.ds(src_row, 1), :], dst.at[pl.ds(dst_row, 1), :], sem)


def _row_dma_wait(src, dst, sem, n):
    def body(r, carry):
        _row_dma(src, 0, dst, 0, sem).wait()
        return carry

    lax.fori_loop(0, n, body, 0, unroll=DMA_UNROLL)


def _dispatch_kernel(dest_ref, x_ref, xs_in, xs_out, sem):
    del xs_in
    n = dest_ref.shape[2]
    for r in range(n):
        _row_dma(x_ref, r // TOP_K, xs_out, dest_ref[0, 0, r], sem.at[0]).start(priority=r % DMA_THREADS)
    _row_dma_wait(x_ref, xs_out, sem.at[0], n)


def _moe_dispatch(x1, dest, xs):
    rows = x1.shape[0]
    tt = TT_COMBINE
    nt = rows // tt
    return pl.pallas_call(
        _dispatch_kernel,
        grid=(nt,),
        in_specs=[pl.BlockSpec((1, 1, TOP_K * tt), lambda t: (t, 0, 0), memory_space=pltpu.SMEM),
                  pl.BlockSpec((tt, D_MODEL), lambda t: (t, 0)),
                  pl.BlockSpec(memory_space=pl.ANY)],
        out_specs=pl.BlockSpec(memory_space=pl.ANY),
        out_shape=jax.ShapeDtypeStruct(xs.shape, xs.dtype),
        input_output_aliases={2: 0},
        scratch_shapes=[pltpu.SemaphoreType.DMA((1,))],
        compiler_params=_cparams("arbitrary"),
        name="moe_dispatch",
    )(dest.reshape(nt, 1, TOP_K * tt), x1, xs)


def _moe_kernel(be_ref, nact_ref, xs_ref, gffn_ref, wgu_ref, bgu_ref, wd_ref, bd_ref, o_ref, wgu_sc, wd_sc):
    b = pl.program_id(0)

    @pl.when(b >= nact_ref[0])
    def _():
        o_ref[...] = jnp.zeros(o_ref.shape, o_ref.dtype)

    @pl.when(b < nact_ref[0])
    def _():
        changed = jnp.logical_or(b == 0, be_ref[b] != be_ref[jnp.maximum(b - 1, 0)])

        @pl.when(changed)
        def _():
            wgu_sc[...] = wgu_ref[0].astype(BF16)
            wd_sc[...] = wd_ref[0].astype(BF16)

        xb = (_rms(xs_ref[...]) * gffn_ref[...]).astype(BF16)
        gu = _dot(xb, wgu_sc[...]) + bgu_ref[0]
        g = jnp.minimum(gu[:, :D_FF], SWIGLU_LIMIT)
        u = jnp.clip(gu[:, D_FF:], -SWIGLU_LIMIT, SWIGLU_LIMIT)
        hid = (u + 1.0) * (g / (1.0 + jnp.exp(-SWIGLU_ALPHA * g)))
        o_ref[...] = _dot(hid.astype(BF16), wd_sc[...]) + bd_ref[0]


def _moe_experts(xs, block_expert, n_active, p):
    n_blocks = block_expert.shape[0]
    bm = BM_MOE
    blk = lambda b, be, na: (jnp.minimum(b, na[0] - 1), 0)
    grid_spec = pltpu.PrefetchScalarGridSpec(
        num_scalar_prefetch=2,
        grid=(n_blocks,),
        in_specs=[
            pl.BlockSpec((bm, D_MODEL), blk),
            pl.BlockSpec((1, D_MODEL), lambda b, be, na: (0, 0)),
            pl.BlockSpec((1, D_MODEL, 2 * D_FF), lambda b, be, na: (be[b], 0, 0)),
            pl.BlockSpec((1, 1, 2 * D_FF), lambda b, be, na: (be[b], 0, 0)),
            pl.BlockSpec((1, D_FF, D_MODEL), lambda b, be, na: (be[b], 0, 0)),
            pl.BlockSpec((1, 1, D_MODEL), lambda b, be, na: (be[b], 0, 0)),
        ],
        out_specs=pl.BlockSpec((bm, D_MODEL), lambda b, be, na: (b, 0)),
        scratch_shapes=[pltpu.VMEM((D_MODEL, 2 * D_FF), BF16), pltpu.VMEM((D_FF, D_MODEL), BF16)],
    )
    return pl.pallas_call(
        _moe_kernel,
        grid_spec=grid_spec,
        out_shape=jax.ShapeDtypeStruct((n_blocks * bm, D_MODEL), F32),
        compiler_params=_cparams("arbitrary"),
        name="moe_experts",
    )(block_expert, n_active, xs, p["g_ffn"], p["w_gate_up"], p["b_gate_up"], p["w_down"], p["b_down"])


def _combine_gather(y_hbm, pos_ref, dst, sem, n):
    for r in range(n):
        _row_dma(y_hbm, pos_ref[0, 0, r], dst, r, sem).start(priority=r % DMA_THREADS)


def _combine_kernel(nta, pos_ref, posn_ref, y_hbm, x1a_ref, ga_ref, x1b_ref, gb_ref, oa_ref, ob_ref, buf, sem):
    t = pl.program_id(0)
    nt = pl.num_programs(0)
    tt = oa_ref.shape[0]
    n = TOP_K * tt
    slot = t % 2

    @pl.when(t == 0)
    def _():
        _combine_gather(y_hbm, pos_ref, buf.at[0], sem.at[0], n)

    @pl.when(t + 1 < nt)
    def _():
        _combine_gather(y_hbm, posn_ref, buf.at[1 - slot], sem.at[1 - slot], n)

    _row_dma_wait(y_hbm, buf.at[slot], sem.at[slot], n)

    def combine(x1_ref, gate_ref, o_ref):
        acc = x1_ref[...]
        gate = gate_ref[...]
        for k in range(TOP_K):
            acc = acc + gate[:, k:k + 1] * buf[slot, k * tt:(k + 1) * tt, :]
        o_ref[...] = acc

    @pl.when(t < nta)
    def _():
        combine(x1a_ref, ga_ref, oa_ref)

    @pl.when(t >= nta)
    def _():
        combine(x1b_ref, gb_ref, ob_ref)


def _moe_combine(x1a, gate_a, x1b, gate_b, yb, pos):
    tt = TT_COMBINE
    nta, ntb = x1a.shape[0] // tt, x1b.shape[0] // tt
    nt = nta + ntb
    pos3 = pos.reshape(nt, tt, TOP_K).transpose(0, 2, 1).reshape(nt, 1, TOP_K * tt)
    ia = lambda t: (jnp.minimum(t, nta - 1), 0)
    ib = lambda t: (jnp.maximum(t - nta, 0), 0)
    return pl.pallas_call(
        functools.partial(_combine_kernel, nta),
        grid=(nt,),
        in_specs=[pl.BlockSpec((1, 1, TOP_K * tt), lambda t: (t, 0, 0), memory_space=pltpu.SMEM),
                  pl.BlockSpec((1, 1, TOP_K * tt), lambda t: (jnp.minimum(t + 1, nt - 1), 0, 0),
                               memory_space=pltpu.SMEM),
                  pl.BlockSpec(memory_space=pl.ANY),
                  pl.BlockSpec((tt, D_MODEL), ia), pl.BlockSpec((tt, LANES), ia),
                  pl.BlockSpec((tt, D_MODEL), ib), pl.BlockSpec((tt, LANES), ib)],
        out_specs=(pl.BlockSpec((tt, D_MODEL), ia), pl.BlockSpec((tt, D_MODEL), ib)),
        out_shape=(jax.ShapeDtypeStruct(x1a.shape, F32), jax.ShapeDtypeStruct(x1b.shape, F32)),
        scratch_shapes=[pltpu.VMEM((2, TOP_K * tt, D_MODEL), F32), pltpu.SemaphoreType.DMA((2,))],
        compiler_params=_cparams("arbitrary"),
        name="moe_combine",
    )(pos3, pos3, yb, x1a, gate_a, x1b, gate_b)


def _moe_route(idx):
    rows = idx.shape[0]
    n = rows * TOP_K
    bm = BM_MOE
    e = idx.reshape(n)
    onehot = (e[:, None] == jnp.arange(N_EXPERTS, dtype=jnp.int32)[None, :]).astype(jnp.int32)
    csum = jnp.cumsum(onehot, axis=0)
    rank = jnp.sum(csum * onehot, axis=1) - 1
    counts = csum[-1]
    padded = (counts + bm - 1) // bm * bm
    pend = jnp.cumsum(padded)
    dest = ((pend - padded)[e] + rank).astype(jnp.int32).reshape(rows, TOP_K)
    n_blocks = -(-n // bm) + N_EXPERTS
    starts = jnp.arange(n_blocks, dtype=jnp.int32) * bm
    block_expert = jnp.minimum(
        jnp.sum((pend[None, :] <= starts[:, None]).astype(jnp.int32), axis=1), N_EXPERTS - 1).astype(jnp.int32)
    n_active = (pend[-1:] // bm).astype(jnp.int32)
    return dest, block_expert, n_active


def _pack_layer(l, g_attn, w_in, g_q_lat, w_q_b, g_q_nope, g_q_rope, g_kv_lat, w_kv_b, g_k_nope, g_k_rope, g_mem,
                w_mem_kv, g_mem_q, g_mem_k, w_o_mla, w_o_sb, w_o_mem, w_out, g_ffn, w_router, b_router,
                w_gate_up, b_gate_up, w_down, b_down):
    w = w_in[l]
    off = np.cumsum((Q_LORA, KV_LORA, ROPE_DIM, SB_W, SB_W, SB_W, MEM_W))
    o_kpe, o_sbq = int(off[1]), int(off[2])
    o_gate = int(off[6])
    kpe1 = w[:, o_kpe:o_kpe + ROPE_HALF]
    kpe2 = w[:, o_kpe + ROPE_HALF:o_kpe + ROPE_DIM]
    zl = jnp.zeros((D_MODEL, KPE_LANE), F32)
    zr = jnp.zeros((D_MODEL, LANES - KPE_LANE - ROPE_DIM), F32)
    w_a = jnp.concatenate([w[:, :o_kpe], zl, kpe1, kpe2, zr, zl, -kpe2, kpe1, zr, w[:, o_sbq:o_gate]], axis=1)
    wq = w_q_b[l]
    z_n = jnp.zeros((Q_LORA, H_MLA, NOPE_DIM), F32)
    wq_sw = jnp.concatenate([z_n, -wq[..., NOPE_DIM + ROPE_HALF:], wq[..., NOPE_DIM:NOPE_DIM + ROPE_HALF]], axis=-1)
    lane = np.arange(QK_W)
    ind = np.zeros((QK_W, LANES), np.float32)
    ind[lane, lane // QK_DIM] = 1.0
    ind64 = np.zeros((H_MLA * NOPE_DIM, LANES), np.float32)
    ind64[np.arange(H_MLA * NOPE_DIM), np.arange(H_MLA * NOPE_DIM) // NOPE_DIM] = 1.0
    pk = np.zeros((ROPE_DIM, QK_W), np.float32)
    for hd in range(H_MLA):
        pk[np.arange(ROPE_DIM), hd * QK_DIM + NOPE_DIM + np.arange(ROPE_DIM)] = 1.0
    wkv = w_kv_b[l]
    w_k = jnp.concatenate([wkv[..., :NOPE_DIM], jnp.zeros((KV_LORA, H_MLA, ROPE_DIM), F32)], axis=-1)
    g_q = jnp.tile(jnp.concatenate([g_q_nope[l], g_q_rope[l], g_q_rope[l]]), H_MLA) * (QK_DIM ** -0.5 * LOG2E)
    g_k = jnp.tile(jnp.concatenate([g_k_nope[l], g_k_rope[l], g_k_rope[l]]), H_MLA)
    wm = w_mem_kv[l]
    w_mem = jnp.concatenate([wm[..., :MEM_DIM].reshape(D_MODEL, MEM_W), wm[..., MEM_DIM:].reshape(D_MODEL, MEM_W)], 1)
    w_r = jnp.concatenate([w_router[l], jnp.zeros((D_MODEL, LANES - N_EXPERTS), F32)], axis=1)
    w_r_hi = w_r.astype(BF16)
    b_r = jnp.concatenate([b_router[l].astype(F32), jnp.full((LANES - N_EXPERTS,), NEG_BIG, F32)])
    return {
        "g_attn": g_attn[l][None], "w_a": w_a.astype(BF16), "g_q_lat": g_q_lat[l][None],
        "w_q": wq.reshape(Q_LORA, QK_W).astype(BF16), "w_q_sw": wq_sw.reshape(Q_LORA, QK_W).astype(BF16),
        "ind96": jnp.asarray(ind, BF16), "ind96_t": jnp.asarray(ind.T, BF16), "g_q": g_q[None],
        "g_kv_lat": g_kv_lat[l][None], "g_mem_q": jnp.tile(g_mem_q[l], H_MEM)[None] * (MEM_DIM ** -0.5),
        "w_k": w_k.reshape(KV_LORA, QK_W).astype(BF16), "p_kpe": jnp.asarray(pk, BF16), "g_k": g_k[None],
        "w_v": wkv[..., NOPE_DIM:].reshape(KV_LORA, MLA_W).astype(BF16),
        "w_v_t": wkv[..., NOPE_DIM:].reshape(KV_LORA, MLA_W).T.astype(BF16),
        "w_v_h": wkv[..., NOPE_DIM:].transpose(1, 0, 2).astype(BF16),
        "w_k_t": wkv[..., :NOPE_DIM].transpose(1, 2, 0).astype(BF16),
        "w_k_nope": wkv[..., :NOPE_DIM].reshape(KV_LORA, H_MLA * NOPE_DIM).astype(BF16),
        "ind64": jnp.asarray(ind64, BF16), "ones_rope": jnp.ones((ROPE_DIM, LANES), BF16),
        "g_k_nope": g_k_nope[l][None], "g_k_rope": jnp.concatenate([g_k_rope[l], g_k_rope[l]])[None],
        "g_mem": g_mem[l][None], "w_mem": w_mem.astype(BF16), "g_mem_k": jnp.tile(g_mem_k[l], H_MEM)[None],
        "w_g": w[:, o_gate:].astype(BF16), "w_sbv_t": w[:, o_gate - MEM_W - SB_W:o_gate - MEM_W].T.astype(BF16),
        "w_o_mla": w_o_mla[l].astype(BF16), "w_o_sb": w_o_sb[l].astype(BF16),
        "w_o_mem": w_o_mem[l].astype(BF16), "w_out": w_out[l].astype(BF16), "g_ffn": g_ffn[l][None],
        "w_r_hi": w_r_hi, "w_r_lo": (w_r - w_r_hi.astype(F32)).astype(BF16), "b_r": b_r[None],
        "w_gate_up": w_gate_up[l], "b_gate_up": b_gate_up[l][:, None, :], "w_down": w_down[l],
        "b_down": b_down[l][:, None, :],
    }


def kernel(x_prompt, x_sample, mem_prompt, cache_mla_latent, cache_mla_rope, cache_sb_k, cache_sb_v, cache_mem_k, cache_mem_v, g_attn, w_in, g_q_lat, w_q_b, g_q_nope, g_q_rope, g_kv_lat, w_kv_b, g_k_nope, g_k_rope, g_mem, w_mem_kv, g_mem_q, g_mem_k, w_o_mla, w_o_sb, w_o_mem, w_out, g_ffn, w_router, b_router, w_gate_up, b_gate_up, w_down, b_down):
    depth = g_attn.shape[0]
    bp, sp, _ = x_prompt.shape
    bs, ss, _ = x_sample.shape
    past = cache_mla_latent.shape[2]
    n_mem = mem_prompt.shape[1]
    assert bp == 1 and sp % TQ_MLA == 0 and sp % TM_KV == 0 and (bs * ss) % TM_PROJ == 0 and TM_PROJ % ss == 0
    rows_s = bs * ss
    c_sbk = cache_sb_k.reshape(depth, bs, past, SB_W)
    c_sbv = cache_sb_v.reshape(depth, bs, past, SB_W)
    c_mk = cache_mem_k.reshape(depth, bs, n_mem, MEM_W)
    c_mv = cache_mem_v.reshape(depth, bs, n_mem, MEM_W)
    xp = x_prompt.reshape(sp, D_MODEL)
    xs = x_sample.reshape(rows_s, D_MODEL)
    mem2d = mem_prompt.reshape(n_mem, D_MODEL)
    base_p, off_p = np.arange(sp // TM_PROJ) * TM_PROJ, np.arange(TM_PROJ)
    base_s, off_s = np.full((rows_s // TM_PROJ,), past), np.arange(TM_PROJ) % ss
    outs = [[] for _ in range(10)]
    for l in range(depth):
        p = _pack_layer(l, g_attn, w_in, g_q_lat, w_q_b, g_q_nope, g_q_rope, g_kv_lat, w_kv_b, g_k_nope, g_k_rope,
                        g_mem, w_mem_kv, g_mem_q, g_mem_k, w_o_mla, w_o_sb, w_o_mem, w_out, g_ffn, w_router,
                        b_router, w_gate_up, b_gate_up, w_down, b_down)
        q, lat, kpe, sq, sk, sv, skh, _, mq, svt = _proj(xp, base_p, off_p, p)
        kh, _, vth = _kv_expand(lat, kpe, p)
        o_mla = _mla_attn(q, kh, vth)
        o_sb = _sb_attn(sq, skh, svt)
        mk, mv = _mem_kv(mem2d, p)
        x1p, idx_p, gate_p = _merge(xp, o_mla, o_sb, mq, mk, mv, True, p)
        q_s, lat_s, kpe_s, sq_s, sk_s, sv_s, skh_s, svh_s, mq_s, _ = _proj(xs, base_s, off_s, p)
        kn, vn, _ = _kv_expand(lat_s, kpe_s, p)
        o_mla_s, o_sb_s, o_mem_s = _sample_attn(q_s, kn, vn, sq_s, skh_s, svh_s, mq_s, cache_mla_latent,
                                                cache_mla_rope, c_sbk, c_sbv, c_mk, c_mv, l, p)
        x1s, idx_s, gate_s = _merge(xs, o_mla_s, o_sb_s, o_mem_s, mk, mv, False, p)
        dest, block_expert, n_active = _moe_route(jnp.concatenate([idx_p[:, :TOP_K], idx_s[:, :TOP_K]], axis=0))
        slots = jnp.zeros((block_expert.shape[0] * BM_MOE, D_MODEL), F32)
        slots = _moe_dispatch(x1s, dest[sp:], _moe_dispatch(x1p, dest[:sp], slots))
        yb = _moe_experts(slots, block_expert, n_active, p)
        xp, xs = _moe_combine(x1p, gate_p, x1s, gate_s, yb, dest)
        for lst, val in zip(outs, (lat.reshape(bp, sp, KV_LORA), kpe.reshape(bp, sp, ROPE_DIM),
                                   sk.reshape(bp, sp, H_SB, SB_DIM), sv.reshape(bp, sp, H_SB, SB_DIM),
                                   mk.reshape(bp, n_mem, H_MEM, MEM_DIM), mv.reshape(bp, n_mem, H_MEM, MEM_DIM),
                                   lat_s.reshape(bs, ss, KV_LORA), kpe_s.reshape(bs, ss, ROPE_DIM),
                                   sk_s.reshape(bs, ss, H_SB, SB_DIM), sv_s.reshape(bs, ss, H_SB, SB_DIM))):
            lst.append(val)
    return (xp.reshape(bp, sp, D_MODEL), xs.reshape(bs, ss, D_MODEL)) + tuple(jnp.stack(o) for o in outs)
```
